```python
import jax, jax.numpy as jnp
from jax import lax
import numpy as np

D_MODEL = 1024
BATCH = 4
SEQ = 4096
DEPTH = 2
DEC_BATCH = 128
DEC_SEQ = 1
PAST_LEN = 16384
PAGE_SIZE = 128

N_A_LAYERS = DEPTH // 2
N_B_LAYERS = DEPTH - N_A_LAYERS
HEAD_DIM = 64
N_HEADS = D_MODEL // HEAD_DIM
N_KV_HEADS = 4
GROUP = N_HEADS // N_KV_HEADS
ROT_DIM = HEAD_DIM // 4
ROPE_THETA = 500000.0
WINDOW = 128
BLOCK = 128
CONV_WIDTH = 3
FFN_HIDDEN = -(-8 * D_MODEL // (3 * 256)) * 256
PLE_DIM = 256
RMS_EPS = 1e-6
NEG_INF = -1e30

kernel_name = "yoco_shortconv_swa_sink_decoder_step"


def rmsnorm(x, g):
    xf = x.astype(jnp.float32)
    y = xf * lax.rsqrt(jnp.mean(xf * xf, axis=-1, keepdims=True) + RMS_EPS)
    return (y * g.astype(jnp.float32)).astype(x.dtype)


def rope_partial(x, pos):
    half = ROT_DIM // 2
    inv_freq = jnp.power(jnp.float32(ROPE_THETA), -jnp.arange(half, dtype=jnp.float32) / half)
    ang = pos.astype(jnp.float32)[:, None] * inv_freq[None, :]
    cos = jnp.cos(ang)[:, None, :]
    sin = jnp.sin(ang)[:, None, :]
    xf = x.astype(jnp.float32)
    x1 = xf[..., :half]
    x2 = xf[..., half:ROT_DIM]
    out = jnp.concatenate([x1 * cos - x2 * sin, x2 * cos + x1 * sin, xf[..., ROT_DIM:]], axis=-1)
    return out.astype(x.dtype)


def short_conv_mixer(h, conv_state, w_in, w_conv, w_out):
    bcx = h @ w_in
    b_gate, c_gate, xin = jnp.split(bcx, 3, axis=-1)
    u = c_gate * xin
    ext = jnp.concatenate([conv_state.astype(u.dtype), u], axis=1)
    T = u.shape[1]
    conv = w_conv[0] * ext[:, 0:T]
    for j in range(1, CONV_WIDTH):
        conv = conv + w_conv[j] * ext[:, j:j + T]
    y = (b_gate * conv) @ w_out
    return y, ext[:, -(CONV_WIDTH - 1):]


def sink_softmax(s, sink_hg):
    sink = jnp.broadcast_to(sink_hg.astype(jnp.float32)[:, :, None, None], s.shape[:-1] + (1,))
    return jax.nn.softmax(jnp.concatenate([s, sink], axis=-1), axis=-1)[..., :-1]


def swa_prompt(q, k, v, sinks_h):
    B, S = q.shape[:2]
    nb = S // BLOCK
    scale = HEAD_DIM ** -0.5
    qb = q.reshape(B, nb, BLOCK, N_KV_HEADS, GROUP, HEAD_DIM).astype(jnp.float32)
    kb = k.reshape(B, nb, BLOCK, N_KV_HEADS, HEAD_DIM)
    vb = v.reshape(B, nb, BLOCK, N_KV_HEADS, HEAD_DIM)
    kk = jnp.concatenate([jnp.concatenate([jnp.zeros_like(kb[:, :1]), kb[:, :-1]], 1), kb], 2).astype(jnp.float32)
    vv = jnp.concatenate([jnp.concatenate([jnp.zeros_like(vb[:, :1]), vb[:, :-1]], 1), vb], 2).astype(jnp.float32)
    s = jnp.einsum('bnqhgd,bnkhd->bnhgqk', qb, kk) * scale
    blk = jnp.arange(nb, dtype=jnp.int32)[:, None] * BLOCK
    qpos = blk + jnp.arange(BLOCK, dtype=jnp.int32)[None, :]
    kpos = blk - BLOCK + jnp.arange(2 * BLOCK, dtype=jnp.int32)[None, :]
    diff = qpos[:, :, None] - kpos[:, None, :]
    valid = (diff >= 0) & (diff <= WINDOW) & (kpos[:, None, :] >= 0)
    s = jnp.where(valid[None, :, None, None, :, :], s, NEG_INF)
    p = sink_softmax(s, sinks_h.reshape(N_KV_HEADS, GROUP))
    o = jnp.einsum('bnhgqk,bnkhd->bnqhgd', p, vv)
    return o.reshape(B, S, N_HEADS * HEAD_DIM).astype(q.dtype)


def swa_sample(q, k_all, v_all, qpos, kpos, sinks_h):
    Bd, S = q.shape[:2]
    scale = HEAD_DIM ** -0.5
    qg = q.reshape(Bd, S, N_KV_HEADS, GROUP, HEAD_DIM).astype(jnp.float32)
    s = jnp.einsum('bqhgd,bkhd->bhgqk', qg, k_all.astype(jnp.float32)) * scale
    diff = qpos[:, None] - kpos[None, :]
    valid = (diff >= 0) & (diff <= WINDOW)
    s = jnp.where(valid, s, NEG_INF)
    p = sink_softmax(s, sinks_h.reshape(N_KV_HEADS, GROUP))
    o = jnp.einsum('bhgqk,bkhd->bqhgd', p, v_all.astype(jnp.float32))
    return o.reshape(Bd, S, N_HEADS * HEAD_DIM).astype(q.dtype)


def run_trunk(x, p, pos, conv_states, k_buf, v_buf, kpos, w_buf,
              norm_mix_g, norm_ffn_g, norm_ple_g, kv_norm_g, final_norm_g,
              conv_w_in, conv_w, conv_w_out, w_k, w_v, w_q, sinks, w_o,
              ffn_w_gate, ffn_w_up, ffn_w_down, ple_w_proj, ple_w_gate):
    B, T, _ = x.shape
    h = x
    new_conv = []
    k_state = v_state = None
    k_all = v_all = k = v = None
    for i in range(DEPTH):
        if i == N_A_LAYERS:
            hk = rmsnorm(h, kv_norm_g)
            k = rope_partial((hk @ w_k).reshape(B, T, N_KV_HEADS, HEAD_DIM), pos)
            v = (hk @ w_v).reshape(B, T, N_KV_HEADS, HEAD_DIM)
            if k_buf is None:
                k_state, v_state = k[:, -w_buf:], v[:, -w_buf:]
            else:
                k_all = jnp.concatenate([k_buf.astype(k.dtype), k], axis=1)
                v_all = jnp.concatenate([v_buf.astype(v.dtype), v], axis=1)
                k_state, v_state = k_all[:, -w_buf:], v_all[:, -w_buf:]
        hn = rmsnorm(h, norm_mix_g[i])
        if i < N_A_LAYERS:
            y, st = short_conv_mixer(hn, conv_states[i], conv_w_in[i], conv_w[i], conv_w_out[i])
            new_conv.append(st)
        else:
            j = i - N_A_LAYERS
            q = rope_partial((hn @ w_q[j]).reshape(B, T, N_HEADS, HEAD_DIM), pos)
            if k_buf is None:
                o = swa_prompt(q, k, v, sinks[j])
            else:
                o = swa_sample(q, k_all, v_all, pos, kpos, sinks[j])
            y = o @ w_o[j]
        h = h + y
        hf = rmsnorm(h, norm_ffn_g[i])
        h = h + (jax.nn.silu(hf @ ffn_w_gate[i]) * (hf @ ffn_w_up[i])) @ ffn_w_down[i]
        gate = jax.nn.sigmoid(rmsnorm(h, norm_ple_g[i]) @ ple_w_gate[i])
        h = h + gate * (p[i].astype(h.dtype) @ ple_w_proj[i])
    return rmsnorm(h, final_norm_g), jnp.stack(new_conv, axis=0), k_state, v_state


def setup_inputs(seed: int = 0) -> dict:
    key = jax.random.key(seed)
    ks = iter(jax.random.split(key, 40))
    f32 = jnp.float32

    def nrm(shape, scale):
        return jax.random.normal(next(ks), shape, f32) * scale

    w_buf = min(WINDOW, PAST_LEN)
    qw = N_HEADS * HEAD_DIM
    kvw = N_KV_HEADS * HEAD_DIM
    return {
        "x_prompt": nrm((BATCH, SEQ, D_MODEL), 1.0),
        "x_sample": nrm((DEC_BATCH, DEC_SEQ, D_MODEL), 1.0),
        "state_conv": nrm((N_A_LAYERS, DEC_BATCH, CONV_WIDTH - 1, D_MODEL), 1.0),
        "cache_k_win": nrm((DEC_BATCH, w_buf, N_KV_HEADS, HEAD_DIM), 1.0),
        "cache_v_win": nrm((DEC_BATCH, w_buf, N_KV_HEADS, HEAD_DIM), 1.0),
        "p_prompt": nrm((DEPTH, BATCH, SEQ, PLE_DIM), 1.0),
        "p_sample": nrm((DEPTH, DEC_BATCH, DEC_SEQ, PLE_DIM), 1.0),
        "norm_mix_g": 1.0 + nrm((DEPTH, D_MODEL), 0.02),
        "norm_ffn_g": 1.0 + nrm((DEPTH, D_MODEL), 0.02),
        "norm_ple_g": 1.0 + nrm((DEPTH, D_MODEL), 0.02),
        "kv_norm_g": 1.0 + nrm((D_MODEL,), 0.02),
        "final_norm_g": 1.0 + nrm((D_MODEL,), 0.02),
        "conv_w_in": nrm((N_A_LAYERS, D_MODEL, 3 * D_MODEL), D_MODEL ** -0.5),
        "conv_w": nrm((N_A_LAYERS, CONV_WIDTH, D_MODEL), CONV_WIDTH ** -0.5),
        "conv_w_out": nrm((N_A_LAYERS, D_MODEL, D_MODEL), D_MODEL ** -0.5),
        "w_k": nrm((D_MODEL, kvw), D_MODEL ** -0.5),
        "w_v": nrm((D_MODEL, kvw), D_MODEL ** -0.5),
        "w_q": nrm((N_B_LAYERS, D_MODEL, qw), D_MODEL ** -0.5),
        "sinks": nrm((N_B_LAYERS, N_HEADS), 0.5),
        "w_o": nrm((N_B_LAYERS, qw, D_MODEL), qw ** -0.5),
        "ffn_w_gate": nrm((DEPTH, D_MODEL, FFN_HIDDEN), D_MODEL ** -0.5),
        "ffn_w_up": nrm((DEPTH, D_MODEL, FFN_HIDDEN), D_MODEL ** -0.5),
        "ffn_w_down": nrm((DEPTH, FFN_HIDDEN, D_MODEL), FFN_HIDDEN ** -0.5),
        "ple_w_proj": nrm((DEPTH, PLE_DIM, D_MODEL), PLE_DIM ** -0.5),
        "ple_w_gate": nrm((DEPTH, D_MODEL, D_MODEL), D_MODEL ** -0.5),
    }


def reference(x_prompt, x_sample, state_conv, cache_k_win, cache_v_win, p_prompt, p_sample,
              norm_mix_g, norm_ffn_g, norm_ple_g, kv_norm_g, final_norm_g,
              conv_w_in, conv_w, conv_w_out, w_k, w_v, w_q, sinks, w_o,
              ffn_w_gate, ffn_w_up, ffn_w_down, ple_w_proj, ple_w_gate):
    weights = (norm_mix_g, norm_ffn_g, norm_ple_g, kv_norm_g, final_norm_g,
               conv_w_in, conv_w, conv_w_out, w_k, w_v, w_q, sinks, w_o,
               ffn_w_gate, ffn_w_up, ffn_w_down, ple_w_proj, ple_w_gate)
    w_buf = cache_k_win.shape[1]
    s_len = x_prompt.shape[1]
    d_len = x_sample.shape[1]
    pos_p = jnp.arange(s_len, dtype=jnp.int32)
    conv0 = jnp.zeros((N_A_LAYERS, x_prompt.shape[0], CONV_WIDTH - 1, D_MODEL), x_prompt.dtype)
    y_prompt, conv_state_prompt, k_win_prompt, v_win_prompt = run_trunk(
        x_prompt, p_prompt, pos_p, conv0, None, None, None, w_buf, *weights)
    pos_s = PAST_LEN + jnp.arange(d_len, dtype=jnp.int32)
    kpos_s = PAST_LEN - w_buf + jnp.arange(w_buf + d_len, dtype=jnp.int32)
    y_sample, conv_state_sample, k_win_sample, v_win_sample = run_trunk(
        x_sample, p_sample, pos_s, state_conv, cache_k_win, cache_v_win, kpos_s, w_buf, *weights)
    return (y_prompt, y_sample, conv_state_prompt, conv_state_sample,
            k_win_prompt, v_win_prompt, k_win_sample, v_win_sample)
```

```python
import functools

import jax
import jax.numpy as jnp
import numpy as np
from jax import lax
from jax.experimental import pallas as pl
from jax.experimental.pallas import tpu as pltpu

D_MODEL = 1024
HEAD_DIM = 64
N_HEADS = 16
N_KV_HEADS = 4
GROUP = N_HEADS // N_KV_HEADS
KV_WIDTH = N_KV_HEADS * HEAD_DIM
ROT_DIM = HEAD_DIM // 4
ROPE_THETA = 500000.0
WINDOW = 128
BLOCK = 128
CONV_WIDTH = 3
PAST_LEN = 16384
RMS_EPS = 1e-6
NEG_INF = -1e30

LANES = 128
SUBLANES = 8
VMEM_LIMIT_BYTES = 60000 * 1024

PROMPT_TILE = 256
SAMPLE_CHUNK = 8

_BF16 = jnp.bfloat16
_F32 = jnp.float32


def _rms_scale(x):
    return lax.rsqrt(jnp.mean(x * x, axis=-1, keepdims=True) + RMS_EPS)


def _sigmoid(x):
    return 1.0 / (1.0 + jnp.exp(-x))


def _bdot(a, w_ref):
    return jnp.dot(a.astype(_BF16), w_ref[...], preferred_element_type=_F32)


def _rope(x, cos, sin_lo, sin_hi):
    half = ROT_DIM // 2
    out = []
    for c in range(x.shape[1] // LANES):
        slab = x[:, c * LANES:(c + 1) * LANES]
        out.append(slab * cos
                   + pltpu.roll(slab, LANES - half, 1) * sin_lo
                   + pltpu.roll(slab, half, 1) * sin_hi)
    return jnp.concatenate(out, axis=1)


def _ffn_and_ple(h, p, g_ffn, g_ple, w_gate, w_up, w_down, ple_gate, ple_proj):
    hf = (h * _rms_scale(h)) * g_ffn[...]
    g = _bdot(hf, w_gate)
    u = _bdot(hf, w_up)
    h = h + _bdot((g * _sigmoid(g)) * u, w_down)
    hp = (h * _rms_scale(h)) * g_ple[...]
    gate = _sigmoid(_bdot(hp, ple_gate))
    return h + gate * _bdot(p, ple_proj)


def _conv_inputs(x, g_mix, w_in):
    hn = (x * _rms_scale(x)) * g_mix[...]
    bcx = _bdot(hn, w_in)
    b_gate = bcx[:, :D_MODEL]
    u = bcx[:, D_MODEL:2 * D_MODEL] * bcx[:, 2 * D_MODEL:]
    return b_gate, u


def _qkv(h, cos, sin_lo, sin_hi, g_kv, g_mix, w_kv, w_q):
    hs = h * _rms_scale(h)
    kv = _bdot(hs * g_kv[...], w_kv)
    k = _rope(kv[:, :KV_WIDTH], cos, sin_lo, sin_hi)
    v = kv[:, KV_WIDTH:]
    q = _rope(_bdot(hs * g_mix[...], w_q), cos, sin_lo, sin_hi) * (HEAD_DIM ** -0.5)
    return q, k, v


def _layer0_prompt_kernel(x_ref, p_ref, g_mix, g_ffn, g_ple, w_in, conv_w, w_out,
                          w_gate, w_up, w_down, ple_gate, ple_proj,
                          h_out, state_out, ubuf):
    tm = x_ref.shape[0]

    @pl.when(pl.program_id(1) == 0)
    def _():
        ubuf[0:SUBLANES, :] = jnp.zeros((SUBLANES, D_MODEL), _F32)

    x = x_ref[...]
    b_gate, u = _conv_inputs(x, g_mix, w_in)
    ubuf[SUBLANES:SUBLANES + tm, :] = u
    conv = conv_w[0:1, :] * ubuf[SUBLANES - 2:SUBLANES - 2 + tm, :]
    conv = conv + conv_w[1:2, :] * ubuf[SUBLANES - 1:SUBLANES - 1 + tm, :]
    conv = conv + conv_w[2:3, :] * u
    h = x + _bdot(b_gate * conv, w_out)
    h_out[...] = _ffn_and_ple(h, p_ref[...], g_ffn, g_ple, w_gate, w_up, w_down, ple_gate, ple_proj)
    state_out[...] = u[tm - (CONV_WIDTH - 1):, :]
    ubuf[0:SUBLANES, :] = ubuf[tm:tm + SUBLANES, :]


def _layer0_sample_kernel(x_ref, p_ref, s0_ref, s1_ref, g_mix, g_ffn, g_ple, w_in, conv_w, w_out,
                          w_gate, w_up, w_down, ple_gate, ple_proj,
                          h_out, u_out):
    x = x_ref[...]
    b_gate, u = _conv_inputs(x, g_mix, w_in)
    conv = conv_w[0:1, :] * s0_ref[...]
    conv = conv + conv_w[1:2, :] * s1_ref[...]
    conv = conv + conv_w[2:3, :] * u
    h = x + _bdot(b_gate * conv, w_out)
    h_out[...] = _ffn_and_ple(h, p_ref[...], g_ffn, g_ple, w_gate, w_up, w_down, ple_gate, ple_proj)
    u_out[...] = u


def _layer1_prompt_kernel(h_ref, p_ref, cos_ref, slo_ref, shi_ref, sinks_ref,
                          g_kv, g_mix, g_ffn, g_ple, g_final,
                          w_kv, w_q, w_o, w_gate, w_up, w_down, ple_gate, ple_proj,
                          y_out, kwin_out, vwin_out, k2buf, v2buf):
    tm = h_ref.shape[0]
    t = pl.program_id(1)

    @pl.when(t == 0)
    def _():
        zeros = jnp.zeros((N_KV_HEADS, BLOCK, LANES), _BF16)
        k2buf[:, 0:BLOCK, :] = zeros
        v2buf[:, 0:BLOCK, :] = zeros

    h = h_ref[...]
    q, k, v = _qkv(h, cos_ref[...], slo_ref[...], shi_ref[...], g_kv, g_mix, w_kv, w_q)
    kwin_out[...] = k[tm - WINDOW:, :]
    vwin_out[...] = v[tm - WINDOW:, :]

    lane = lax.broadcasted_iota(jnp.int32, (1, LANES), 1)
    low = lane < HEAD_DIM

    for arr, buf in ((k, k2buf), (v, v2buf)):
        for s in range(KV_WIDTH // LANES):
            slab = arr[:, s * LANES:(s + 1) * LANES]
            swapped = pltpu.roll(slab, HEAD_DIM, 1)
            buf[2 * s, BLOCK:BLOCK + tm, :] = jnp.where(low, slab, swapped).astype(_BF16)
            buf[2 * s + 1, BLOCK:BLOCK + tm, :] = jnp.where(low, swapped, slab).astype(_BF16)

    row = lax.broadcasted_iota(jnp.int32, (BLOCK, 2 * BLOCK), 0)
    col = lax.broadcasted_iota(jnp.int32, (BLOCK, 2 * BLOCK), 1)
    diff = row + BLOCK - col
    band = (diff >= 0) & (diff <= WINDOW)
    first_col = jnp.where(t == 0, BLOCK, 0)

    qb = q.astype(_BF16)
    zero = jnp.zeros((), _BF16)
    o_blocks = []
    for jb in range(tm // BLOCK):
        valid = band & (col >= first_col) if jb == 0 else band
        r0 = jb * BLOCK
        o_slabs = []
        for g in range(N_KV_HEADS):
            parts = []
            for s in (2 * g, 2 * g + 1):
                slab = qb[r0:r0 + BLOCK, s * LANES:(s + 1) * LANES]
                parts.append(jnp.where(low, slab, zero))
                parts.append(jnp.where(low, zero, slab))
            qs = jnp.concatenate(parts, axis=0)
            sc = lax.dot_general(qs, k2buf[g, r0:r0 + 2 * BLOCK, :],
                                 (((1,), (1,)), ((), ())), preferred_element_type=_F32)
            es, inv = [], []
            for i in range(GROUP):
                sink = sinks_ref[GROUP * g + i]
                sp = jnp.where(valid, sc[i * BLOCK:(i + 1) * BLOCK], NEG_INF)
                m = jnp.maximum(jnp.max(sp, axis=-1, keepdims=True), sink)
                e = jnp.exp(sp - m)
                inv.append(1.0 / (jnp.sum(e, axis=-1, keepdims=True) + jnp.exp(sink - m)))
                es.append(e.astype(_BF16))
            o2 = jnp.dot(jnp.concatenate(es, axis=0), v2buf[g, r0:r0 + 2 * BLOCK, :],
                         preferred_element_type=_F32)
            o2 = [o2[i * BLOCK:(i + 1) * BLOCK] * inv[i] for i in range(GROUP)]
            o_slabs.append(jnp.where(low, o2[0], o2[1]))
            o_slabs.append(jnp.where(low, o2[2], o2[3]))
        o_blocks.append(jnp.concatenate(o_slabs, axis=1))
    o = jnp.concatenate(o_blocks, axis=0)

    h = h + _bdot(o, w_o)
    h = _ffn_and_ple(h, p_ref[...], g_ffn, g_ple, w_gate, w_up, w_down, ple_gate, ple_proj)
    y_out[...] = (h * _rms_scale(h)) * g_final[...]

    k2buf[:, 0:BLOCK, :] = k2buf[:, tm:tm + BLOCK, :]
    v2buf[:, 0:BLOCK, :] = v2buf[:, tm:tm + BLOCK, :]


def _sample_qkv_kernel(h_ref, cos_ref, slo_ref, shi_ref, g_kv, g_mix, w_kv, w_q,
                       q_out, k_out, v_out):
    q, k, v = _qkv(h_ref[...], cos_ref[...], slo_ref[...], shi_ref[...], g_kv, g_mix, w_kv, w_q)
    q_out[...] = q
    k_out[...] = k
    v_out[...] = v


def _sample_attn_kernel(q_ref, kn_ref, vn_ref, ck_ref, cv_ref, sink_ref,
                        o_out, kwin_out, vwin_out):
    lane = lax.broadcasted_iota(jnp.int32, (N_HEADS, KV_WIDTH), 1)
    r16 = lax.broadcasted_iota(jnp.int32, (N_HEADS, KV_WIDTH), 0)
    slot = r16 // N_KV_HEADS
    kvh = r16 - slot * N_KV_HEADS
    own = (lane >= kvh * HEAD_DIM) & (lane < (kvh + 1) * HEAD_DIM)
    sink = sink_ref[...]
    w = ck_ref.shape[1]
    for b in range(q_ref.shape[0]):
        kn = kn_ref[b:b + 1, :]
        vn = vn_ref[b:b + 1, :]
        kwin_out[b, 0:w - 1, :] = ck_ref[b, 1:w, :]
        kwin_out[b, w - 1:w, :] = kn
        vwin_out[b, 0:w - 1, :] = cv_ref[b, 1:w, :]
        vwin_out[b, w - 1:w, :] = vn
        lhs = jnp.zeros((N_HEADS, KV_WIDTH), _F32)
        for i in range(GROUP):
            q_i = jnp.broadcast_to(q_ref[b:b + 1, i * KV_WIDTH:(i + 1) * KV_WIDTH], (N_HEADS, KV_WIDTH))
            lhs = jnp.where(own & (slot == i), q_i, lhs)
        lhs = lhs.astype(_BF16)
        s_old = lax.dot_general(lhs, ck_ref[b].astype(_BF16), (((1,), (1,)), ((), ())),
                                preferred_element_type=_F32)
        s_new = jnp.sum(lhs.astype(_F32) * kn.astype(_BF16).astype(_F32), axis=-1, keepdims=True)
        m = jnp.maximum(jnp.maximum(jnp.max(s_old, axis=-1, keepdims=True), s_new), sink)
        e_old = jnp.exp(s_old - m)
        e_new = jnp.exp(s_new - m)
        inv = 1.0 / (jnp.sum(e_old, axis=-1, keepdims=True) + e_new + jnp.exp(sink - m))
        o16 = jnp.dot(e_old.astype(_BF16), cv_ref[b].astype(_BF16), preferred_element_type=_F32)
        o16 = (o16 + e_new.astype(_BF16).astype(_F32) * vn.astype(_BF16).astype(_F32)) * inv
        for i in range(GROUP):
            picked = jnp.where(own & (slot == i), o16, 0.0)
            o_out[b:b + 1, i * KV_WIDTH:(i + 1) * KV_WIDTH] = jnp.sum(picked, axis=0, keepdims=True)


def _sample_post_kernel(h_ref, o_ref, p_ref, g_ffn, g_ple, g_final,
                        w_o, w_gate, w_up, w_down, ple_gate, ple_proj, y_out):
    h = h_ref[...] + _bdot(o_ref[...], w_o)
    h = _ffn_and_ple(h, p_ref[...], g_ffn, g_ple, w_gate, w_up, w_down, ple_gate, ple_proj)
    y_out[...] = (h * _rms_scale(h)) * g_final[...]


def _resident(arr):
    nd = arr.ndim
    return pl.BlockSpec(arr.shape, lambda *_: (0,) * nd, pipeline_mode=pl.Buffered(1))


def _params(n_grid_dims):
    return pltpu.CompilerParams(dimension_semantics=("arbitrary",) * n_grid_dims,
                                vmem_limit_bytes=VMEM_LIMIT_BYTES)


def _rope_tables(pos):
    half = ROT_DIM // 2
    inv_freq = jnp.power(jnp.float32(ROPE_THETA), -jnp.arange(half, dtype=_F32) / half)
    ang = pos.astype(_F32)[:, None] * inv_freq[None, :]
    cos, sin = jnp.cos(ang), jnp.sin(ang)
    n = pos.shape[0]
    pad = jnp.zeros((n, HEAD_DIM - ROT_DIM), _F32)
    zero = jnp.zeros((n, half), _F32)
    cos_h = jnp.concatenate([cos, cos, pad + 1.0], axis=1)
    lo_h = jnp.concatenate([-sin, zero, pad], axis=1)
    hi_h = jnp.concatenate([zero, sin, pad], axis=1)
    rep = LANES // HEAD_DIM
    return tuple(jnp.tile(a, (1, rep)) for a in (cos_h, lo_h, hi_h))


def _slot_major_perm():
    idx = np.arange(N_HEADS * HEAD_DIM).reshape(N_KV_HEADS, GROUP, HEAD_DIM)
    return idx.transpose(1, 0, 2).reshape(-1)


def kernel(x_prompt, x_sample, state_conv, cache_k_win, cache_v_win, p_prompt, p_sample,
           norm_mix_g, norm_ffn_g, norm_ple_g, kv_norm_g, final_norm_g,
           conv_w_in, conv_w, conv_w_out, w_k, w_v, w_q, sinks, w_o,
           ffn_w_gate, ffn_w_up, ffn_w_down, ple_w_proj, ple_w_gate):
    batch, seq, d = x_prompt.shape
    dec = x_sample.shape[0]
    w_buf = cache_k_win.shape[1]
    tm = PROMPT_TILE
    assert d == D_MODEL and seq % tm == 0 and tm % BLOCK == 0 and tm >= WINDOW
    assert x_sample.shape[1] == 1 and w_buf == WINDOW and dec % SAMPLE_CHUNK == 0

    bf = lambda a: a.astype(_BF16)
    row = lambda a: a.reshape(1, -1)
    l0 = dict(g_mix=row(norm_mix_g[0]), g_ffn=row(norm_ffn_g[0]), g_ple=row(norm_ple_g[0]),
              w_in=bf(conv_w_in[0]), conv_w=conv_w[0], w_out=bf(conv_w_out[0]),
              w_gate=bf(ffn_w_gate[0]), w_up=bf(ffn_w_up[0]), w_down=bf(ffn_w_down[0]),
              ple_gate=bf(ple_w_gate[0]), ple_proj=bf(ple_w_proj[0]))
    l1 = dict(g_kv=row(kv_norm_g), g_mix=row(norm_mix_g[1]), g_ffn=row(norm_ffn_g[1]),
              g_ple=row(norm_ple_g[1]), g_final=row(final_norm_g),
              w_kv=bf(jnp.concatenate([w_k, w_v], axis=1)), w_q=bf(w_q[0]), w_o=bf(w_o[0]),
              w_gate=bf(ffn_w_gate[1]), w_up=bf(ffn_w_up[1]), w_down=bf(ffn_w_down[1]),
              ple_gate=bf(ple_w_gate[1]), ple_proj=bf(ple_w_proj[1]))
    l0_names = ("g_mix", "g_ffn", "g_ple", "w_in", "conv_w", "w_out",
                "w_gate", "w_up", "w_down", "ple_gate", "ple_proj")
    l0_args = [l0[n] for n in l0_names]

    grid = (batch, seq // tm)
    tile = lambda width: pl.BlockSpec((None, tm, width), lambda b, t: (b, t, 0))
    ple_tile = lambda layer: pl.BlockSpec((None, None, tm, p_prompt.shape[-1]),
                                          lambda b, t: (layer, b, t, 0))
    h1, conv_state_prompt = pl.pallas_call(
        _layer0_prompt_kernel,
        grid=grid,
        in_specs=[tile(d), ple_tile(0)] + [_resident(a) for a in l0_args],
        out_specs=[tile(d), pl.BlockSpec((None, CONV_WIDTH - 1, d), lambda b, t: (b, 0, 0))],
        out_shape=[jax.ShapeDtypeStruct((batch, seq, d), _F32),
                   jax.ShapeDtypeStruct((batch, CONV_WIDTH - 1, d), _F32)],
        scratch_shapes=[pltpu.VMEM((tm + SUBLANES, d), _F32)],
        compiler_params=_params(2),
        name="layer0_prompt",
    )(x_prompt, p_prompt, *l0_args)

    cos_p, lo_p, hi_p = _rope_tables(jnp.arange(seq, dtype=jnp.int32))
    table = pl.BlockSpec((tm, LANES), lambda b, t: (t, 0))
    l1_names = ("g_kv", "g_mix", "g_ffn", "g_ple", "g_final", "w_kv", "w_q", "w_o",
                "w_gate", "w_up", "w_down", "ple_gate", "ple_proj")
    l1_args = [l1[n] for n in l1_names]
    win = pl.BlockSpec((None, WINDOW, KV_WIDTH), lambda b, t: (b, 0, 0))
    y_prompt, k_win_prompt, v_win_prompt = pl.pallas_call(
        _layer1_prompt_kernel,
        grid=grid,
        in_specs=[tile(d), ple_tile(1), table, table, table,
                  pl.BlockSpec(memory_space=pltpu.SMEM)] + [_resident(a) for a in l1_args],
        out_specs=[tile(d), win, win],
        out_shape=[jax.ShapeDtypeStruct((batch, seq, d), _F32),
                   jax.ShapeDtypeStruct((batch, WINDOW, KV_WIDTH), _F32),
                   jax.ShapeDtypeStruct((batch, WINDOW, KV_WIDTH), _F32)],
        scratch_shapes=[pltpu.VMEM((N_KV_HEADS, BLOCK + tm, LANES), _BF16),
                        pltpu.VMEM((N_KV_HEADS, BLOCK + tm, LANES), _BF16)],
        compiler_params=_params(2),
        name="layer1_prompt",
    )(h1, p_prompt, cos_p, lo_p, hi_p, sinks[0], *l1_args)

    xs = x_sample.reshape(dec, d)
    ps = p_sample.reshape(p_sample.shape[0], dec, p_sample.shape[-1])
    whole = lambda a: pl.BlockSpec(a.shape, lambda i: (0,) * a.ndim)
    s_in = [xs, ps[0], state_conv[0, :, 0, :], state_conv[0, :, 1, :]]
    h1s, u_s = pl.pallas_call(
        _layer0_sample_kernel,
        grid=(1,),
        in_specs=[whole(a) for a in s_in] + [_resident(a) for a in l0_args],
        out_specs=[pl.BlockSpec((dec, d), lambda i: (0, 0))] * 2,
        out_shape=[jax.ShapeDtypeStruct((dec, d), _F32)] * 2,
        compiler_params=_params(1),
        name="layer0_sample",
    )(*s_in, *l0_args)
    conv_state_sample = jnp.stack([state_conv[0, :, 1, :], u_s], axis=1)[None]

    perm = _slot_major_perm()
    cos_s, lo_s, hi_s = _rope_tables(jnp.full((dec,), PAST_LEN, dtype=jnp.int32))
    qkv_in = [h1s, cos_s, lo_s, hi_s, l1["g_kv"], l1["g_mix"], l1["w_kv"], bf(w_q[0][:, perm])]
    q_s, k_s, v_s = pl.pallas_call(
        _sample_qkv_kernel,
        grid=(1,),
        in_specs=[whole(a) for a in qkv_in[:6]] + [_resident(a) for a in qkv_in[6:]],
        out_specs=[pl.BlockSpec((dec, d), lambda i: (0, 0)),
                   pl.BlockSpec((dec, KV_WIDTH), lambda i: (0, 0)),
                   pl.BlockSpec((dec, KV_WIDTH), lambda i: (0, 0))],
        out_shape=[jax.ShapeDtypeStruct((dec, d), _F32),
                   jax.ShapeDtypeStruct((dec, KV_WIDTH), _F32),
                   jax.ShapeDtypeStruct((dec, KV_WIDTH), _F32)],
        compiler_params=_params(1),
        name="sample_qkv",
    )(*qkv_in)

    bc = SAMPLE_CHUNK
    sink_rows = sinks[0].reshape(N_KV_HEADS, GROUP).T.reshape(N_HEADS, 1)
    ck = cache_k_win.reshape(dec, w_buf, KV_WIDTH)
    cv = cache_v_win.reshape(dec, w_buf, KV_WIDTH)
    rows = lambda width: pl.BlockSpec((bc, width), lambda i: (i, 0))
    cache = pl.BlockSpec((bc, w_buf, KV_WIDTH), lambda i: (i, 0, 0))
    o_s, k_win_sample, v_win_sample = pl.pallas_call(
        _sample_attn_kernel,
        grid=(dec // bc,),
        in_specs=[rows(d), rows(KV_WIDTH), rows(KV_WIDTH), cache, cache,
                  pl.BlockSpec((N_HEADS, 1), lambda i: (0, 0))],
        out_specs=[rows(d), cache, cache],
        out_shape=[jax.ShapeDtypeStruct((dec, d), _F32),
                   jax.ShapeDtypeStruct((dec, w_buf, KV_WIDTH), _F32),
                   jax.ShapeDtypeStruct((dec, w_buf, KV_WIDTH), _F32)],
        compiler_params=_params(1),
        name="sample_attn",
    )(q_s, k_s, v_s, ck, cv, sink_rows)

    post_names = ("g_ffn", "g_ple", "g_final", "w_o", "w_gate", "w_up", "w_down", "ple_gate", "ple_proj")
    post_args = [bf(w_o[0][perm, :]) if n == "w_o" else l1[n] for n in post_names]
    post_in = [h1s, o_s, ps[1]]
    y_sample = pl.pallas_call(
        _sample_post_kernel,
        grid=(1,),
        in_specs=[whole(a) for a in post_in] + [_resident(a) for a in post_args],
        out_specs=pl.BlockSpec((dec, d), lambda i: (0, 0)),
        out_shape=jax.ShapeDtypeStruct((dec, d), _F32),
        compiler_params=_params(1),
        name="sample_post",
    )(*post_in, *post_args)

    kv_shape = (N_KV_HEADS, HEAD_DIM)
    return (y_prompt, y_sample.reshape(dec, 1, d),
            conv_state_prompt[None], conv_state_sample,
            k_win_prompt.reshape(batch, w_buf, *kv_shape), v_win_prompt.reshape(batch, w_buf, *kv_shape),
            k_win_sample.reshape(dec, w_buf, *kv_shape), v_win_sample.reshape(dec, w_buf, *kv_shape))
```

```python
import functools

import jax
import jax.numpy as jnp
import numpy as np
from jax import lax
from jax.experimental import pallas as pl
from jax.experimental.pallas import tpu as pltpu

D_MODEL = 1024
HEAD_DIM = 64
N_HEADS = 16
N_KV_HEADS = 4
GROUP = N_HEADS // N_KV_HEADS
KV_WIDTH = N_KV_HEADS * HEAD_DIM
ROT_DIM = HEAD_DIM // 4
ROPE_THETA = 500000.0
WINDOW = 128
BLOCK = 128
CONV_WIDTH = 3
PAST_LEN = 16384
RMS_EPS = 1e-6
NEG_INF = -1e30

LANES = 128
SUBLANES = 8
MXU_COLS = 256
VMEM_LIMIT_BYTES = 60000 * 1024

PROMPT_TILE = 256
SAMPLE_CHUNK = 8
DENSE_STEPS_BEFORE_SCORES = 6
DENSE_STEPS_AFTER_UNITS = 0

_BF16 = jnp.bfloat16
_F32 = jnp.float32


def _rms_scale(x):
    return lax.rsqrt(jnp.mean(x * x, axis=-1, keepdims=True) + RMS_EPS)


def _sigmoid(x):
    return 1.0 / (1.0 + jnp.exp(-x))


def _bdot(a, w_ref):
    return jnp.dot(a.astype(_BF16), w_ref[...], preferred_element_type=_F32)


def _rope(x, cos, sin_lo, sin_hi):
    half = ROT_DIM // 2
    out = []
    for c in range(x.shape[1] // LANES):
        slab = x[:, c * LANES:(c + 1) * LANES]
        out.append(slab * cos
                   + pltpu.roll(slab, LANES - half, 1) * sin_lo
                   + pltpu.roll(slab, half, 1) * sin_hi)
    return jnp.concatenate(out, axis=1)


def _run(steps):
    try:
        while True:
            next(steps)
    except StopIteration as done:
        return done.value


def _dot_cols(a, w_ref):
    a = a.astype(_BF16)
    cols = []
    for c in range(0, w_ref.shape[1], MXU_COLS):
        cols.append(jnp.dot(a, w_ref[:, c:c + MXU_COLS], preferred_element_type=_F32))
        yield
    return jnp.concatenate(cols, axis=1)


def _ffn_and_ple_steps(h, p, g_ffn, g_ple, w_gate, w_up, w_down, ple_gate, ple_proj):
    hf = ((h * _rms_scale(h)) * g_ffn[...]).astype(_BF16)
    acts = []
    for c in range(0, w_gate.shape[1], MXU_COLS):
        g = jnp.dot(hf, w_gate[:, c:c + MXU_COLS], preferred_element_type=_F32)
        yield
        u = jnp.dot(hf, w_up[:, c:c + MXU_COLS], preferred_element_type=_F32)
        yield
        acts.append(((g * _sigmoid(g)) * u).astype(_BF16))
    h = h + (yield from _dot_cols(jnp.concatenate(acts, axis=1), w_down))
    hp = (h * _rms_scale(h)) * g_ple[...]
    gate = _sigmoid((yield from _dot_cols(hp, ple_gate)))
    return h + gate * (yield from _dot_cols(p, ple_proj))


def _ffn_and_ple(*args):
    return _run(_ffn_and_ple_steps(*args))


def _conv_inputs(x, g_mix, w_in):
    hn = (x * _rms_scale(x)) * g_mix[...]
    bcx = _bdot(hn, w_in)
    b_gate = bcx[:, :D_MODEL]
    u = bcx[:, D_MODEL:2 * D_MODEL] * bcx[:, 2 * D_MODEL:]
    return b_gate, u


def _qkv(h, cos, sin_lo, sin_hi, g_kv, g_mix, w_kv, w_q):
    hs = h * _rms_scale(h)
    kv = _bdot(hs * g_kv[...], w_kv)
    k = _rope(kv[:, :KV_WIDTH], cos, sin_lo, sin_hi)
    v = kv[:, KV_WIDTH:]
    q = _rope(_bdot(hs * g_mix[...], w_q), cos, sin_lo, sin_hi) * (HEAD_DIM ** -0.5)
    return q, k, v


def _layer0_prompt_kernel(x_ref, p_ref, g_mix, g_ffn, g_ple, w_in, conv_w, w_out,
                          w_gate, w_up, w_down, ple_gate, ple_proj,
                          h_out, state_out, ubuf):
    tm = x_ref.shape[0]

    @pl.when(pl.program_id(1) == 0)
    def _():
        ubuf[0:SUBLANES, :] = jnp.zeros((SUBLANES, D_MODEL), _F32)

    x = x_ref[...]
    b_gate, u = _conv_inputs(x, g_mix, w_in)
    ubuf[SUBLANES:SUBLANES + tm, :] = u
    conv = conv_w[0:1, :] * ubuf[SUBLANES - 2:SUBLANES - 2 + tm, :]
    conv = conv + conv_w[1:2, :] * ubuf[SUBLANES - 1:SUBLANES - 1 + tm, :]
    conv = conv + conv_w[2:3, :] * u
    h = x + _bdot(b_gate * conv, w_out)
    h_out[...] = _ffn_and_ple(h, p_ref[...], g_ffn, g_ple, w_gate, w_up, w_down, ple_gate, ple_proj)
    state_out[...] = u[tm - (CONV_WIDTH - 1):, :]
    ubuf[0:SUBLANES, :] = ubuf[tm:tm + SUBLANES, :]


def _layer0_sample_kernel(x_ref, p_ref, s0_ref, s1_ref, g_mix, g_ffn, g_ple, w_in, conv_w, w_out,
                          w_gate, w_up, w_down, ple_gate, ple_proj,
                          h_out, u_out):
    x = x_ref[...]
    b_gate, u = _conv_inputs(x, g_mix, w_in)
    conv = conv_w[0:1, :] * s0_ref[...]
    conv = conv + conv_w[1:2, :] * s1_ref[...]
    conv = conv + conv_w[2:3, :] * u
    h = x + _bdot(b_gate * conv, w_out)
    h_out[...] = _ffn_and_ple(h, p_ref[...], g_ffn, g_ple, w_gate, w_up, w_down, ple_gate, ple_proj)
    u_out[...] = u


def _layer1_prompt_kernel(h_ref, p_ref, cos_ref, slo_ref, shi_ref, sinks_ref,
                          g_kv, g_mix, g_ffn, g_ple, g_final,
                          w_kv, w_q, w_o, w_gate, w_up, w_down, ple_gate, ple_proj,
                          y_out, kwin_out, vwin_out, k2buf, v2buf, hbuf, obuf, *, tiles_per_seq, n_tiles):
    tm = h_ref.shape[0]
    step = pl.program_id(0)
    t = lax.rem(jnp.minimum(step, n_tiles - 1), tiles_per_seq)

    @pl.when(step == 0)
    def _():
        zeros = jnp.zeros((N_KV_HEADS, BLOCK, LANES), _BF16)
        k2buf[:, 0:BLOCK, :] = zeros
        v2buf[:, 0:BLOCK, :] = zeros
        hbuf[...] = jnp.zeros(hbuf.shape, hbuf.dtype)
        obuf[...] = jnp.zeros(obuf.shape, obuf.dtype)

    def dense_half():
        hp = hbuf[...] + (yield from _dot_cols(obuf[...], w_o))
        hp = yield from _ffn_and_ple_steps(hp, p_ref[...], g_ffn, g_ple, w_gate, w_up, w_down,
                                           ple_gate, ple_proj)
        y_out[...] = (hp * _rms_scale(hp)) * g_final[...]

    dense = dense_half()

    def advance(n):
        for _ in range(n):
            next(dense, None)

    w_o_steps = w_o.shape[1] // MXU_COLS
    advance(w_o_steps)

    h = h_ref[...]
    q, k, v = _qkv(h, cos_ref[...], slo_ref[...], shi_ref[...], g_kv, g_mix, w_kv, w_q)
    kwin_out[...] = k[tm - WINDOW:, :]
    vwin_out[...] = v[tm - WINDOW:, :]

    lane = lax.broadcasted_iota(jnp.int32, (1, LANES), 1)
    low = lane < HEAD_DIM

    for arr, buf in ((k, k2buf), (v, v2buf)):
        for s in range(KV_WIDTH // LANES):
            slab = arr[:, s * LANES:(s + 1) * LANES]
            swapped = pltpu.roll(slab, HEAD_DIM, 1)
            buf[2 * s, BLOCK:BLOCK + tm, :] = jnp.where(low, slab, swapped).astype(_BF16)
            buf[2 * s + 1, BLOCK:BLOCK + tm, :] = jnp.where(low, swapped, slab).astype(_BF16)

    row = lax.broadcasted_iota(jnp.int32, (BLOCK, 2 * BLOCK), 0)
    col = lax.broadcasted_iota(jnp.int32, (BLOCK, 2 * BLOCK), 1)
    diff = row + BLOCK - col
    band = (diff >= 0) & (diff <= WINDOW)
    first_col = jnp.where(t == 0, BLOCK, 0)

    qb = q.astype(_BF16)
    zero = jnp.zeros((), _BF16)
    ones = jnp.ones((2 * BLOCK, LANES), _BF16)
    units = [(jb, g) for jb in range(tm // BLOCK) for g in range(N_KV_HEADS)]

    def scores(jb, g):
        r0 = jb * BLOCK
        parts = []
        for s in (2 * g, 2 * g + 1):
            slab = qb[r0:r0 + BLOCK, s * LANES:(s + 1) * LANES]
            parts.append(jnp.where(low, slab, zero))
            parts.append(jnp.where(low, zero, slab))
        qs = jnp.concatenate(parts, axis=0)
        return lax.dot_general(qs, k2buf[g, r0:r0 + 2 * BLOCK, :],
                               (((1,), (1,)), ((), ())), preferred_element_type=_F32)

    def weights(jb, g, sc):
        valid = band & (col >= first_col) if jb == 0 else band
        es, tail = [], []
        for i in range(GROUP):
            sink = sinks_ref[GROUP * g + i]
            sp = jnp.where(valid, sc[i * BLOCK:(i + 1) * BLOCK], NEG_INF)
            m = jnp.maximum(jnp.max(sp, axis=-1, keepdims=True), sink)
            es.append(jnp.exp(sp - m).astype(_BF16))
            tail.append(jnp.exp(sink - m))
        return jnp.concatenate(es, axis=0), tail

    def values(jb, g, e, tail):
        r0 = jb * BLOCK
        vv = jnp.concatenate([v2buf[g, r0:r0 + 2 * BLOCK, :], ones], axis=1)
        o3 = jnp.dot(e, vv, preferred_element_type=_F32)
        o2 = []
        for i in range(GROUP):
            piece = o3[i * BLOCK:(i + 1) * BLOCK]
            o2.append(piece[:, :LANES] * (1.0 / (piece[:, LANES:] + tail[i])))
        obuf[r0:r0 + BLOCK, (2 * g) * LANES:(2 * g + 1) * LANES] = (
            jnp.where(low, o2[0], o2[1]).astype(_BF16))
        obuf[r0:r0 + BLOCK, (2 * g + 1) * LANES:(2 * g + 2) * LANES] = (
            jnp.where(low, o2[2], o2[3]).astype(_BF16))

    dense_steps = (2 * w_gate.shape[1] + w_down.shape[1]
                   + ple_gate.shape[1] + ple_proj.shape[1]) // MXU_COLS
    advance(DENSE_STEPS_BEFORE_SCORES)
    share = (dense_steps - DENSE_STEPS_BEFORE_SCORES - DENSE_STEPS_AFTER_UNITS) // (len(units) - 1)
    sc_next = scores(*units[0])
    for n, unit in enumerate(units):
        sc = sc_next
        if n + 1 < len(units):
            sc_next = scores(*units[n + 1])
        e, tail = weights(*unit, sc)
        if n + 1 < len(units):
            advance(share)
        values(*unit, e, tail)
    _run(dense)

    hbuf[...] = h
    k2buf[:, 0:BLOCK, :] = k2buf[:, tm:tm + BLOCK, :]
    v2buf[:, 0:BLOCK, :] = v2buf[:, tm:tm + BLOCK, :]


def _sample_qkv_kernel(h_ref, cos_ref, slo_ref, shi_ref, g_kv, g_mix, w_kv, w_q,
                       q_out, k_out, v_out):
    q, k, v = _qkv(h_ref[...], cos_ref[...], slo_ref[...], shi_ref[...], g_kv, g_mix, w_kv, w_q)
    q_out[...] = q
    k_out[...] = k
    v_out[...] = v


def _sample_attn_kernel(q_ref, kn_ref, vn_ref, ck_ref, cv_ref, sink_ref,
                        o_out, kwin_out, vwin_out):
    lane = lax.broadcasted_iota(jnp.int32, (N_HEADS, KV_WIDTH), 1)
    r16 = lax.broadcasted_iota(jnp.int32, (N_HEADS, KV_WIDTH), 0)
    slot = r16 // N_KV_HEADS
    kvh = r16 - slot * N_KV_HEADS
    own = (lane >= kvh * HEAD_DIM) & (lane < (kvh + 1) * HEAD_DIM)
    sink = sink_ref[...]
    w = ck_ref.shape[1]
    for b in range(q_ref.shape[0]):
        kn = kn_ref[b:b + 1, :]
        vn = vn_ref[b:b + 1, :]
        kwin_out[b, 0:w - 1, :] = ck_ref[b, 1:w, :]
        kwin_out[b, w - 1:w, :] = kn
        vwin_out[b, 0:w - 1, :] = cv_ref[b, 1:w, :]
        vwin_out[b, w - 1:w, :] = vn
        lhs = jnp.zeros((N_HEADS, KV_WIDTH), _F32)
        for i in range(GROUP):
            q_i = jnp.broadcast_to(q_ref[b:b + 1, i * KV_WIDTH:(i + 1) * KV_WIDTH], (N_HEADS, KV_WIDTH))
            lhs = jnp.where(own & (slot == i), q_i, lhs)
        lhs = lhs.astype(_BF16)
        s_old = lax.dot_general(lhs, ck_ref[b].astype(_BF16), (((1,), (1,)), ((), ())),
                                preferred_element_type=_F32)
        s_new = jnp.sum(lhs.astype(_F32) * kn.astype(_BF16).astype(_F32), axis=-1, keepdims=True)
        m = jnp.maximum(jnp.maximum(jnp.max(s_old, axis=-1, keepdims=True), s_new), sink)
        e_old = jnp.exp(s_old - m)
        e_new = jnp.exp(s_new - m)
        inv = 1.0 / (jnp.sum(e_old, axis=-1, keepdims=True) + e_new + jnp.exp(sink - m))
        o16 = jnp.dot(e_old.astype(_BF16), cv_ref[b].astype(_BF16), preferred_element_type=_F32)
        o16 = (o16 + e_new.astype(_BF16).astype(_F32) * vn.astype(_BF16).astype(_F32)) * inv
        for i in range(GROUP):
            picked = jnp.where(own & (slot == i), o16, 0.0)
            o_out[b:b + 1, i * KV_WIDTH:(i + 1) * KV_WIDTH] = jnp.sum(picked, axis=0, keepdims=True)


def _sample_post_kernel(h_ref, o_ref, p_ref, g_ffn, g_ple, g_final,
                        w_o, w_gate, w_up, w_down, ple_gate, ple_proj, y_out):
    h = h_ref[...] + _bdot(o_ref[...], w_o)
    h = _ffn_and_ple(h, p_ref[...], g_ffn, g_ple, w_gate, w_up, w_down, ple_gate, ple_proj)
    y_out[...] = (h * _rms_scale(h)) * g_final[...]


def _resident(arr):
    nd = arr.ndim
    return pl.BlockSpec(arr.shape, lambda *_: (0,) * nd, pipeline_mode=pl.Buffered(1))


def _params(n_grid_dims):
    return pltpu.CompilerParams(dimension_semantics=("arbitrary",) * n_grid_dims,
                                vmem_limit_bytes=VMEM_LIMIT_BYTES)


def _rope_tables(pos):
    half = ROT_DIM // 2
    inv_freq = jnp.power(jnp.float32(ROPE_THETA), -jnp.arange(half, dtype=_F32) / half)
    ang = pos.astype(_F32)[:, None] * inv_freq[None, :]
    cos, sin = jnp.cos(ang), jnp.sin(ang)
    n = pos.shape[0]
    pad = jnp.zeros((n, HEAD_DIM - ROT_DIM), _F32)
    zero = jnp.zeros((n, half), _F32)
    cos_h = jnp.concatenate([cos, cos, pad + 1.0], axis=1)
    lo_h = jnp.concatenate([-sin, zero, pad], axis=1)
    hi_h = jnp.concatenate([zero, sin, pad], axis=1)
    rep = LANES // HEAD_DIM
    return tuple(jnp.tile(a, (1, rep)) for a in (cos_h, lo_h, hi_h))


def _slot_major_perm():
    idx = np.arange(N_HEADS * HEAD_DIM).reshape(N_KV_HEADS, GROUP, HEAD_DIM)
    return idx.transpose(1, 0, 2).reshape(-1)


def kernel(x_prompt, x_sample, state_conv, cache_k_win, cache_v_win, p_prompt, p_sample,
           norm_mix_g, norm_ffn_g, norm_ple_g, kv_norm_g, final_norm_g,
           conv_w_in, conv_w, conv_w_out, w_k, w_v, w_q, sinks, w_o,
           ffn_w_gate, ffn_w_up, ffn_w_down, ple_w_proj, ple_w_gate):
    batch, seq, d = x_prompt.shape
    dec = x_sample.shape[0]
    w_buf = cache_k_win.shape[1]
    tm = PROMPT_TILE
    assert d == D_MODEL and seq % tm == 0 and tm % BLOCK == 0 and tm >= WINDOW
    assert x_sample.shape[1] == 1 and w_buf == WINDOW and dec % SAMPLE_CHUNK == 0

    bf = lambda a: a.astype(_BF16)
    row = lambda a: a.reshape(1, -1)
    l0 = dict(g_mix=row(norm_mix_g[0]), g_ffn=row(norm_ffn_g[0]), g_ple=row(norm_ple_g[0]),
              w_in=bf(conv_w_in[0]), conv_w=conv_w[0], w_out=bf(conv_w_out[0]),
              w_gate=bf(ffn_w_gate[0]), w_up=bf(ffn_w_up[0]), w_down=bf(ffn_w_down[0]),
              ple_gate=bf(ple_w_gate[0]), ple_proj=bf(ple_w_proj[0]))
    l1 = dict(g_kv=row(kv_norm_g), g_mix=row(norm_mix_g[1]), g_ffn=row(norm_ffn_g[1]),
              g_ple=row(norm_ple_g[1]), g_final=row(final_norm_g),
              w_kv=bf(jnp.concatenate([w_k, w_v], axis=1)), w_q=bf(w_q[0]), w_o=bf(w_o[0]),
              w_gate=bf(ffn_w_gate[1]), w_up=bf(ffn_w_up[1]), w_down=bf(ffn_w_down[1]),
              ple_gate=bf(ple_w_gate[1]), ple_proj=bf(ple_w_proj[1]))
    l0_names = ("g_mix", "g_ffn", "g_ple", "w_in", "conv_w", "w_out",
                "w_gate", "w_up", "w_down", "ple_gate", "ple_proj")
    l0_args = [l0[n] for n in l0_names]

    grid = (batch, seq // tm)
    tile = lambda width: pl.BlockSpec((None, tm, width), lambda b, t: (b, t, 0))
    ple_tile = lambda layer: pl.BlockSpec((None, None, tm, p_prompt.shape[-1]),
                                          lambda b, t: (layer, b, t, 0))
    h1, conv_state_prompt = pl.pallas_call(
        _layer0_prompt_kernel,
        grid=grid,
        in_specs=[tile(d), ple_tile(0)] + [_resident(a) for a in l0_args],
        out_specs=[tile(d), pl.BlockSpec((None, CONV_WIDTH - 1, d), lambda b, t: (b, 0, 0))],
        out_shape=[jax.ShapeDtypeStruct((batch, seq, d), _F32),
                   jax.ShapeDtypeStruct((batch, CONV_WIDTH - 1, d), _F32)],
        scratch_shapes=[pltpu.VMEM((tm + SUBLANES, d), _F32)],
        compiler_params=_params(2),
        name="layer0_prompt",
    )(x_prompt, p_prompt, *l0_args)

    cos_p, lo_p, hi_p = _rope_tables(jnp.arange(seq, dtype=jnp.int32))
    l1_names = ("g_kv", "g_mix", "g_ffn", "g_ple", "g_final", "w_kv", "w_q", "w_o",
                "w_gate", "w_up", "w_down", "ple_gate", "ple_proj")
    l1_args = [l1[n] for n in l1_names]
    nt = seq // tm
    n_tiles = batch * nt
    cur = lambda s: jnp.minimum(s, n_tiles - 1)
    prev = lambda s: jnp.maximum(s - 1, 0)
    y_prompt, k_win_prompt, v_win_prompt = pl.pallas_call(
        functools.partial(_layer1_prompt_kernel, tiles_per_seq=nt, n_tiles=n_tiles),
        grid=(n_tiles + 1,),
        in_specs=[pl.BlockSpec((None, tm, d), lambda s: (cur(s) // nt, cur(s) % nt, 0)),
                  pl.BlockSpec((None, None, tm, p_prompt.shape[-1]),
                               lambda s: (1, prev(s) // nt, prev(s) % nt, 0))]
                 + [pl.BlockSpec((tm, LANES), lambda s: (cur(s) % nt, 0))] * 3
                 + [pl.BlockSpec(memory_space=pltpu.SMEM)] + [_resident(a) for a in l1_args],
        out_specs=[pl.BlockSpec((None, tm, d), lambda s: (prev(s) // nt, prev(s) % nt, 0)),
                   pl.BlockSpec((None, WINDOW, KV_WIDTH), lambda s: (cur(s) // nt, 0, 0)),
                   pl.BlockSpec((None, WINDOW, KV_WIDTH), lambda s: (cur(s) // nt, 0, 0))],
        out_shape=[jax.ShapeDtypeStruct((batch, seq, d), _F32),
                   jax.ShapeDtypeStruct((batch, WINDOW, KV_WIDTH), _F32),
                   jax.ShapeDtypeStruct((batch, WINDOW, KV_WIDTH), _F32)],
        scratch_shapes=[pltpu.VMEM((N_KV_HEADS, BLOCK + tm, LANES), _BF16),
                        pltpu.VMEM((N_KV_HEADS, BLOCK + tm, LANES), _BF16),
                        pltpu.VMEM((tm, d), _F32),
                        pltpu.VMEM((tm, d), _BF16)],
        compiler_params=_params(1),
        name="layer1_prompt",
    )(h1, p_prompt, cos_p, lo_p, hi_p, sinks[0], *l1_args)

    xs = x_sample.reshape(dec, d)
    ps = p_sample.reshape(p_sample.shape[0], dec, p_sample.shape[-1])
    whole = lambda a: pl.BlockSpec(a.shape, lambda i: (0,) * a.ndim)
    s_in = [xs, ps[0], state_conv[0, :, 0, :], state_conv[0, :, 1, :]]
    h1s, u_s = pl.pallas_call(
        _layer0_sample_kernel,
        grid=(1,),
        in_specs=[whole(a) for a in s_in] + [_resident(a) for a in l0_args],
        out_specs=[pl.BlockSpec((dec, d), lambda i: (0, 0))] * 2,
        out_shape=[jax.ShapeDtypeStruct((dec, d), _F32)] * 2,
        compiler_params=_params(1),
        name="layer0_sample",
    )(*s_in, *l0_args)
    conv_state_sample = jnp.stack([state_conv[0, :, 1, :], u_s], axis=1)[None]

    perm = _slot_major_perm()
    cos_s, lo_s, hi_s = _rope_tables(jnp.full((dec,), PAST_LEN, dtype=jnp.int32))
    qkv_in = [h1s, cos_s, lo_s, hi_s, l1["g_kv"], l1["g_mix"], l1["w_kv"], bf(w_q[0][:, perm])]
    q_s, k_s, v_s = pl.pallas_call(
        _sample_qkv_kernel,
        grid=(1,),
        in_specs=[whole(a) for a in qkv_in[:6]] + [_resident(a) for a in qkv_in[6:]],
        out_specs=[pl.BlockSpec((dec, d), lambda i: (0, 0)),
                   pl.BlockSpec((dec, KV_WIDTH), lambda i: (0, 0)),
                   pl.BlockSpec((dec, KV_WIDTH), lambda i: (0, 0))],
        out_shape=[jax.ShapeDtypeStruct((dec, d), _F32),
                   jax.ShapeDtypeStruct((dec, KV_WIDTH), _F32),
                   jax.ShapeDtypeStruct((dec, KV_WIDTH), _F32)],
        compiler_params=_params(1),
        name="sample_qkv",
    )(*qkv_in)

    bc = SAMPLE_CHUNK
    sink_rows = sinks[0].reshape(N_KV_HEADS, GROUP).T.reshape(N_HEADS, 1)
    ck = cache_k_win.reshape(dec, w_buf, KV_WIDTH)
    cv = cache_v_win.reshape(dec, w_buf, KV_WIDTH)
    rows = lambda width: pl.BlockSpec((bc, width), lambda i: (i, 0))
    cache = pl.BlockSpec((bc, w_buf, KV_WIDTH), lambda i: (i, 0, 0))
    o_s, k_win_sample, v_win_sample = pl.pallas_call(
        _sample_attn_kernel,
        grid=(dec // bc,),
        in_specs=[rows(d), rows(KV_WIDTH), rows(KV_WIDTH), cache, cache,
                  pl.BlockSpec((N_HEADS, 1), lambda i: (0, 0))],
        out_specs=[rows(d), cache, cache],
        out_shape=[jax.ShapeDtypeStruct((dec, d), _F32),
                   jax.ShapeDtypeStruct((dec, w_buf, KV_WIDTH), _F32),
                   jax.ShapeDtypeStruct((dec, w_buf, KV_WIDTH), _F32)],
        compiler_params=_params(1),
        name="sample_attn",
    )(q_s, k_s, v_s, ck, cv, sink_rows)

    post_names = ("g_ffn", "g_ple", "g_final", "w_o", "w_gate", "w_up", "w_down", "ple_gate", "ple_proj")
    post_args = [bf(w_o[0][perm, :]) if n == "w_o" else l1[n] for n in post_names]
    post_in = [h1s, o_s, ps[1]]
    y_sample = pl.pallas_call(
        _sample_post_kernel,
        grid=(1,),
        in_specs=[whole(a) for a in post_in] + [_resident(a) for a in post_args],
        out_specs=pl.BlockSpec((dec, d), lambda i: (0, 0)),
        out_shape=jax.ShapeDtypeStruct((dec, d), _F32),
        compiler_params=_params(1),
        name="sample_post",
    )(*post_in, *post_args)

    kv_shape = (N_KV_HEADS, HEAD_DIM)
    return (y_prompt, y_sample.reshape(dec, 1, d),
            conv_state_prompt[None], conv_state_sample,
            k_win_prompt.reshape(batch, w_buf, *kv_shape), v_win_prompt.reshape(batch, w_buf, *kv_shape),
            k_win_sample.reshape(dec, w_buf, *kv_shape), v_win_sample.reshape(dec, w_buf, *kv_shape))
```

```python
import functools

import jax
import jax.numpy as jnp
from jax import lax
from jax.experimental import pallas as pl
from jax.experimental.pallas import tpu as pltpu

D_MODEL = 1024
HEAD_DIM = 64
N_HEADS = 16
N_KV_HEADS = 4
GROUP = N_HEADS // N_KV_HEADS
KV_WIDTH = N_KV_HEADS * HEAD_DIM
ROT_DIM = HEAD_DIM // 4
ROPE_THETA = 500000.0
WINDOW = 128
BLOCK = 128
CONV_WIDTH = 3
PAST_LEN = 16384
RMS_EPS = 1e-6
NEG_INF = -1e30

LANES = 128
SUBLANES = 8
MXU_COLS = 256
CHUNK_COLS = 2 * MXU_COLS
VMEM_LIMIT_BYTES = 60000 * 1024

PROMPT_TILE = 256
SAMPLE_CHUNK = 8
LAYER0_WINDOWS = ((0.10, 0.92), (0.0, 1.0))
LAYER1_WINDOWS = ((0.04, 0.92), (0.0, 1.0))

_BF16 = jnp.bfloat16
_F32 = jnp.float32


def _rms_scale(x):
    return lax.rsqrt(jnp.mean(x * x, axis=-1, keepdims=True) + RMS_EPS)


def _sigmoid(x):
    return 1.0 / (1.0 + jnp.exp(-x))


def _bdot(a, w_ref):
    return jnp.dot(a.astype(_BF16), w_ref[...], preferred_element_type=_F32)


def _rope(x, cos, sin_lo, sin_hi):
    half = ROT_DIM // 2
    out = []
    for c in range(x.shape[1] // LANES):
        slab = x[:, c * LANES:(c + 1) * LANES]
        out.append(slab * cos
                   + pltpu.roll(slab, LANES - half, 1) * sin_lo
                   + pltpu.roll(slab, half, 1) * sin_hi)
    return jnp.concatenate(out, axis=1)


def _run(steps):
    try:
        while True:
            next(steps)
    except StopIteration as done:
        return done.value


def _interleave(*lanes):
    done = [0.0] * len(lanes)
    live = list(range(len(lanes)))

    def position(j):
        _, total, (start, end) = lanes[j]
        return start + (end - start) * done[j] / total

    while live:
        i = min(live, key=position)
        try:
            done[i] += next(lanes[i][0])
        except StopIteration:
            live.remove(i)


def _chunks(n_cols):
    return [(c, min(CHUNK_COLS, n_cols - c)) for c in range(0, n_cols, CHUNK_COLS)]


def _col_dot(a, w_ref, c0, width):
    return jnp.dot(a, w_ref[:, c0:c0 + width], preferred_element_type=_F32)


def _dot_cost(w_ref, width):
    return -(-w_ref.shape[0] // MXU_COLS) * (width // MXU_COLS)


def _chunks_cost(w_ref):
    return _dot_cost(w_ref, w_ref.shape[1])


def _dot_cols(a, w_ref):
    a = a.astype(_BF16)
    cols = []
    for c, width in _chunks(w_ref.shape[1]):
        cols.append(_col_dot(a, w_ref, c, width))
        yield _dot_cost(w_ref, width)
    return jnp.concatenate(cols, axis=1)


def _ffn_norm(h, g_ffn):
    return ((h * _rms_scale(h)) * g_ffn[...]).astype(_BF16)


def _ffn_and_ple_steps(h, hf, p, g_ple, w_gate, w_up, w_down, ple_gate, ple_proj):
    acts = []
    for c, width in _chunks(w_gate.shape[1]):
        g = _col_dot(hf, w_gate, c, width)
        yield _dot_cost(w_gate, width)
        u = _col_dot(hf, w_up, c, width)
        yield _dot_cost(w_up, width)
        acts.append(((g * _sigmoid(g)) * u).astype(_BF16))
    h = h + (yield from _dot_cols(jnp.concatenate(acts, axis=1), w_down))
    hp = (h * _rms_scale(h)) * g_ple[...]
    gate = _sigmoid((yield from _dot_cols(hp, ple_gate)))
    return h + gate * (yield from _dot_cols(p, ple_proj))


def _ffn_and_ple_cost(w_gate, w_down, ple_gate, ple_proj):
    return 2 * _chunks_cost(w_gate) + _chunks_cost(w_down) + _chunks_cost(ple_gate) + _chunks_cost(ple_proj)


def _conv_mixer_steps(x, g_mix, w_in, conv_w, w_out, history):
    hn = ((x * _rms_scale(x)) * g_mix[...]).astype(_BF16)
    yield 0
    us, convs = [], []
    for c, width in _chunks(D_MODEL):
        cols = slice(c, c + width)
        c_gate = _col_dot(hn, w_in, D_MODEL + c, width)
        yield _dot_cost(w_in, width)
        xin = _col_dot(hn, w_in, 2 * D_MODEL + c, width)
        yield _dot_cost(w_in, width)
        u = c_gate * xin
        back2, back1 = history(cols, u)
        conv = conv_w[0:1, cols] * back2
        conv = conv + conv_w[1:2, cols] * back1
        conv = conv + conv_w[2:3, cols] * u
        us.append(u)
        convs.append(conv)
    gated = []
    for j, (c, width) in enumerate(_chunks(D_MODEL)):
        b_gate = _col_dot(hn, w_in, c, width)
        yield _dot_cost(w_in, width)
        gated.append((b_gate * convs[j]).astype(_BF16))
    y = yield from _dot_cols(jnp.concatenate(gated, axis=1), w_out)
    return x + y, jnp.concatenate(us, axis=1)


def _conv_mixer_cost(w_in, w_out):
    return _chunks_cost(w_in) + _chunks_cost(w_out)


def _pipeline_bodies(step, n_tiles, body):
    pl.when(step == 0)(lambda: body(True, False))
    pl.when((step > 0) & (step < n_tiles))(lambda: body(True, True))
    pl.when(step == n_tiles)(lambda: body(False, True))


def _qkv(h, cos, sin_lo, sin_hi, g_kv, g_mix, w_kv, w_q):
    hs = h * _rms_scale(h)
    kv = _bdot(hs * g_kv[...], w_kv)
    k = _rope(kv[:, :KV_WIDTH], cos, sin_lo, sin_hi)
    v = kv[:, KV_WIDTH:]
    q = _rope(_bdot(hs * g_mix[...], w_q), cos, sin_lo, sin_hi) * (HEAD_DIM ** -0.5)
    return q, k, v


def _layer0_prompt_kernel(x_ref, p_ref, g_mix, g_ffn, g_ple, w_in, conv_w, w_out,
                          w_gate, w_up, w_down, ple_gate, ple_proj,
                          h_out, state_out, ubuf, hbuf, hfbuf, *, tiles_per_seq, n_tiles):
    tm = x_ref.shape[0]
    step = pl.program_id(0)

    def history(cols, u):
        ubuf[SUBLANES:SUBLANES + tm, cols] = u
        return (ubuf[SUBLANES - 2:SUBLANES - 2 + tm, cols], ubuf[SUBLANES - 1:SUBLANES - 1 + tm, cols])

    def ffn_half():
        out = yield from _ffn_and_ple_steps(hbuf[...], hfbuf[...], p_ref[...], g_ple,
                                            w_gate, w_up, w_down, ple_gate, ple_proj)
        h_out[...] = out

    def conv_half():
        h, _ = yield from _conv_mixer_steps(x_ref[...], g_mix, w_in, conv_w, w_out, history)
        state_out[...] = ubuf[SUBLANES + tm - (CONV_WIDTH - 1):SUBLANES + tm, :]
        ubuf[0:SUBLANES, :] = ubuf[tm:tm + SUBLANES, :]
        hbuf[...] = h
        hfbuf[...] = _ffn_norm(h, g_ffn)

    def body(first_half, second_half):
        lanes = []
        if first_half:
            zeros = jnp.zeros((SUBLANES, D_MODEL), _F32)
            fresh = lax.rem(step, tiles_per_seq) == 0
            ubuf[0:SUBLANES, :] = jnp.where(fresh, zeros, ubuf[0:SUBLANES, :]) if second_half else zeros
        if second_half:
            ffn = ffn_half()
            next(ffn)
            lanes.append((ffn, _ffn_and_ple_cost(w_gate, w_down, ple_gate, ple_proj), LAYER0_WINDOWS[1]))
        if first_half:
            conv = conv_half()
            next(conv)
            lanes.append((conv, _conv_mixer_cost(w_in, w_out), LAYER0_WINDOWS[0]))
        _interleave(*lanes)

    _pipeline_bodies(step, n_tiles, body)


def _layer0_sample_kernel(x_ref, p_ref, s0_ref, s1_ref, g_mix, g_ffn, g_ple, w_in, conv_w, w_out,
                          w_gate, w_up, w_down, ple_gate, ple_proj,
                          h_out, u_out):
    history = lambda cols, u: (s0_ref[:, cols], s1_ref[:, cols])
    h, u = _run(_conv_mixer_steps(x_ref[...], g_mix, w_in, conv_w, w_out, history))
    h_out[...] = _run(_ffn_and_ple_steps(h, _ffn_norm(h, g_ffn), p_ref[...], g_ple,
                                         w_gate, w_up, w_down, ple_gate, ple_proj))
    u_out[...] = u


def _layer1_prompt_kernel(h_ref, p_ref, cos_ref, slo_ref, shi_ref, sinks_ref,
                          g_kv, g_mix, g_ffn, g_ple, g_final,
                          w_kv, w_q, w_o, w_gate, w_up, w_down, ple_gate, ple_proj,
                          y_out, kwin_out, vwin_out, k2buf, v2buf, hbuf, obuf, *, tiles_per_seq, n_tiles):
    tm = h_ref.shape[0]
    step = pl.program_id(0)
    attn_cost = GROUP * BLOCK / tm

    def dense_half():
        hp = hbuf[...] + (yield from _dot_cols(obuf[...], w_o))
        hp = yield from _ffn_and_ple_steps(hp, _ffn_norm(hp, g_ffn), p_ref[...], g_ple,
                                           w_gate, w_up, w_down, ple_gate, ple_proj)
        y_out[...] = (hp * _rms_scale(hp)) * g_final[...]

    def attention_half():
        h = h_ref[...]
        hs = h * _rms_scale(h)
        h_kv = (hs * g_kv[...]).astype(_BF16)
        h_q = (hs * g_mix[...]).astype(_BF16)
        yield 0
        kv = yield from _dot_cols(h_kv, w_kv)
        q = yield from _dot_cols(h_q, w_q)
        cos, sin_lo, sin_hi = cos_ref[...], slo_ref[...], shi_ref[...]
        k = _rope(kv[:, :KV_WIDTH], cos, sin_lo, sin_hi)
        v = kv[:, KV_WIDTH:]
        qb = (_rope(q, cos, sin_lo, sin_hi) * (HEAD_DIM ** -0.5)).astype(_BF16)
        kwin_out[...] = k[tm - WINDOW:, :]
        vwin_out[...] = v[tm - WINDOW:, :]

        lane = lax.broadcasted_iota(jnp.int32, (1, LANES), 1)
        low = lane < HEAD_DIM
        for arr, buf in ((k, k2buf), (v, v2buf)):
            for s in range(KV_WIDTH // LANES):
                slab = arr[:, s * LANES:(s + 1) * LANES]
                swapped = pltpu.roll(slab, HEAD_DIM, 1)
                buf[2 * s, BLOCK:BLOCK + tm, :] = jnp.where(low, slab, swapped).astype(_BF16)
                buf[2 * s + 1, BLOCK:BLOCK + tm, :] = jnp.where(low, swapped, slab).astype(_BF16)

        row = lax.broadcasted_iota(jnp.int32, (BLOCK, 2 * BLOCK), 0)
        col = lax.broadcasted_iota(jnp.int32, (BLOCK, 2 * BLOCK), 1)
        diff = row + BLOCK - col
        band = (diff >= 0) & (diff <= WINDOW)
        first_col = jnp.where(lax.rem(step, tiles_per_seq) == 0, BLOCK, 0)
        zero = jnp.zeros((), _BF16)
        ones = jnp.ones((2 * BLOCK, LANES), _BF16)
        units = [(jb, g) for jb in range(tm // BLOCK) for g in range(N_KV_HEADS)]

        def scores(jb, g):
            r0 = jb * BLOCK
            parts = []
            for s in (2 * g, 2 * g + 1):
                slab = qb[r0:r0 + BLOCK, s * LANES:(s + 1) * LANES]
                parts.append(jnp.where(low, slab, zero))
                parts.append(jnp.where(low, zero, slab))
            qs = jnp.concatenate(parts, axis=0)
            return lax.dot_general(qs, k2buf[g, r0:r0 + 2 * BLOCK, :],
                                   (((1,), (1,)), ((), ())), preferred_element_type=_F32)

        def weights(jb, g, sc):
            valid = band & (col >= first_col) if jb == 0 else band
            es, tail = [], []
            for i in range(GROUP):
                sink = sinks_ref[GROUP * g + i]
                sp = jnp.where(valid, sc[i * BLOCK:(i + 1) * BLOCK], NEG_INF)
                m = jnp.maximum(jnp.max(sp, axis=-1, keepdims=True), sink)
                es.append(jnp.exp(sp - m).astype(_BF16))
                tail.append(jnp.exp(sink - m))
            return jnp.concatenate(es, axis=0), tail

        def values(jb, g, e, tail):
            r0 = jb * BLOCK
            vv = jnp.concatenate([v2buf[g, r0:r0 + 2 * BLOCK, :], ones], axis=1)
            o3 = jnp.dot(e, vv, preferred_element_type=_F32)
            o2 = []
            for i in range(GROUP):
                piece = o3[i * BLOCK:(i + 1) * BLOCK]
                o2.append(piece[:, :LANES] * (1.0 / (piece[:, LANES:] + tail[i])))
            obuf[r0:r0 + BLOCK, (2 * g) * LANES:(2 * g + 1) * LANES] = (
                jnp.where(low, o2[0], o2[1]).astype(_BF16))
            obuf[r0:r0 + BLOCK, (2 * g + 1) * LANES:(2 * g + 2) * LANES] = (
                jnp.where(low, o2[2], o2[3]).astype(_BF16))

        sc, ew = {}, {}
        for n in range(len(units) + 2):
            if n < len(units):
                sc[n] = scores(*units[n])
                yield attn_cost
            if 0 <= n - 1 < len(units):
                ew[n - 1] = weights(*units[n - 1], sc.pop(n - 1))
                yield 0.0
            if 0 <= n - 2 < len(units):
                values(*units[n - 2], *ew.pop(n - 2))
                yield attn_cost
        hbuf[...] = h
        k2buf[:, 0:BLOCK, :] = k2buf[:, tm:tm + BLOCK, :]
        v2buf[:, 0:BLOCK, :] = v2buf[:, tm:tm + BLOCK, :]

    def body(first_half, second_half):
        lanes = []
        if first_half and not second_half:
            zeros = jnp.zeros((N_KV_HEADS, BLOCK, LANES), _BF16)
            k2buf[:, 0:BLOCK, :] = zeros
            v2buf[:, 0:BLOCK, :] = zeros
        if second_half:
            dense = dense_half()
            next(dense)
            lanes.append((dense, _chunks_cost(w_o) + _ffn_and_ple_cost(w_gate, w_down, ple_gate, ple_proj),
                          LAYER1_WINDOWS[1]))
        if first_half:
            n_units = (tm // BLOCK) * N_KV_HEADS
            attention = attention_half()
            next(attention)
            lanes.append((attention,
                          _chunks_cost(w_kv) + _chunks_cost(w_q) + 2 * n_units * attn_cost,
                          LAYER1_WINDOWS[0]))
        _interleave(*lanes)

    _pipeline_bodies(step, n_tiles, body)


def _sample_qkv_kernel(h_ref, cos_ref, slo_ref, shi_ref, g_kv, g_mix, w_kv, w_q,
                       q_out, k_out, v_out):
    q, k, v = _qkv(h_ref[...], cos_ref[...], slo_ref[...], shi_ref[...], g_kv, g_mix, w_kv, w_q)
    q_out[...] = q
    k_out[...] = k
    v_out[...] = v


def _sample_attn_kernel(q_ref, kn_ref, vn_ref, ck_ref, cv_ref, sink_ref,
                        o_out, kwin_out, vwin_out):
    lane = lax.broadcasted_iota(jnp.int32, (N_HEADS, KV_WIDTH), 1)
    r16 = lax.broadcasted_iota(jnp.int32, (N_HEADS, KV_WIDTH), 0)
    slot = r16 // N_KV_HEADS
    kvh = r16 - slot * N_KV_HEADS
    own = (lane >= kvh * HEAD_DIM) & (lane < (kvh + 1) * HEAD_DIM)
    sink = sink_ref[...]
    w = ck_ref.shape[1]
    for b in range(q_ref.shape[0]):
        kn = kn_ref[b:b + 1, :]
        vn = vn_ref[b:b + 1, :]
        kwin_out[b, 0:w - 1, :] = ck_ref[b, 1:w, :]
        kwin_out[b, w - 1:w, :] = kn
        vwin_out[b, 0:w - 1, :] = cv_ref[b, 1:w, :]
        vwin_out[b, w - 1:w, :] = vn
        lhs = jnp.zeros((N_HEADS, KV_WIDTH), _F32)
        for i in range(GROUP):
            q_i = jnp.broadcast_to(q_ref[b:b + 1, i * KV_WIDTH:(i + 1) * KV_WIDTH], (N_HEADS, KV_WIDTH))
            lhs = jnp.where(own & (slot == i), q_i, lhs)
        lhs = lhs.astype(_BF16)
        s_old = lax.dot_general(lhs, ck_ref[b].astype(_BF16), (((1,), (1,)), ((), ())),
                                preferred_element_type=_F32)
        s_new = jnp.sum(lhs.astype(_F32) * kn.astype(_BF16).astype(_F32), axis=-1, keepdims=True)
        m = jnp.maximum(jnp.maximum(jnp.max(s_old, axis=-1, keepdims=True), s_new), sink)
        e_old = jnp.exp(s_old - m)
        e_new = jnp.exp(s_new - m)
        inv = 1.0 / (jnp.sum(e_old, axis=-1, keepdims=True) + e_new + jnp.exp(sink - m))
        o16 = jnp.dot(e_old.astype(_BF16), cv_ref[b].astype(_BF16), preferred_element_type=_F32)
        o16 = (o16 + e_new.astype(_BF16).astype(_F32) * vn.astype(_BF16).astype(_F32)) * inv
        for i in range(GROUP):
            picked = jnp.where(own & (slot == i), o16, 0.0)
            o_out[b:b + 1, i * KV_WIDTH:(i + 1) * KV_WIDTH] = jnp.sum(picked, axis=0, keepdims=True)


def _sample_post_kernel(h_ref, o_ref, p_ref, g_ffn, g_ple, g_final,
                        w_o, w_gate, w_up, w_down, ple_gate, ple_proj, y_out):
    h = h_ref[...] + _bdot(o_ref[...], w_o)
    h = _run(_ffn_and_ple_steps(h, _ffn_norm(h, g_ffn), p_ref[...], g_ple,
                                w_gate, w_up, w_down, ple_gate, ple_proj))
    y_out[...] = (h * _rms_scale(h)) * g_final[...]


def _resident(arr):
    nd = arr.ndim
    return pl.BlockSpec(arr.shape, lambda *_: (0,) * nd, pipeline_mode=pl.Buffered(1))


def _params(n_grid_dims):
    return pltpu.CompilerParams(dimension_semantics=("arbitrary",) * n_grid_dims,
                                vmem_limit_bytes=VMEM_LIMIT_BYTES)


def _rope_tables(pos):
    half = ROT_DIM // 2
    inv_freq = jnp.power(jnp.float32(ROPE_THETA), -jnp.arange(half, dtype=_F32) / half)
    ang = pos.astype(_F32)[:, None] * inv_freq[None, :]
    cos, sin = jnp.cos(ang), jnp.sin(ang)
    n = pos.shape[0]
    pad = jnp.zeros((n, HEAD_DIM - ROT_DIM), _F32)
    zero = jnp.zeros((n, half), _F32)
    cos_h = jnp.concatenate([cos, cos, pad + 1.0], axis=1)
    lo_h = jnp.concatenate([-sin, zero, pad], axis=1)
    hi_h = jnp.concatenate([zero, sin, pad], axis=1)
    rep = LANES // HEAD_DIM
    return tuple(jnp.tile(a, (1, rep)) for a in (cos_h, lo_h, hi_h))


def _slot_major(w, axis):
    shape = w.shape
    split = shape[:axis] + (N_KV_HEADS, GROUP, HEAD_DIM) + shape[axis + 1:]
    return jnp.swapaxes(w.reshape(split), axis, axis + 1).reshape(shape)


def kernel(x_prompt, x_sample, state_conv, cache_k_win, cache_v_win, p_prompt, p_sample,
           norm_mix_g, norm_ffn_g, norm_ple_g, kv_norm_g, final_norm_g,
           conv_w_in, conv_w, conv_w_out, w_k, w_v, w_q, sinks, w_o,
           ffn_w_gate, ffn_w_up, ffn_w_down, ple_w_proj, ple_w_gate):
    batch, seq, d = x_prompt.shape
    dec = x_sample.shape[0]
    w_buf = cache_k_win.shape[1]
    tm = PROMPT_TILE
    assert d == D_MODEL and seq % tm == 0 and tm % BLOCK == 0 and tm >= WINDOW
    assert x_sample.shape[1] == 1 and w_buf == WINDOW and dec % SAMPLE_CHUNK == 0

    bf = lambda a: a.astype(_BF16)
    row = lambda a: a.reshape(1, -1)
    l0 = dict(g_mix=row(norm_mix_g[0]), g_ffn=row(norm_ffn_g[0]), g_ple=row(norm_ple_g[0]),
              w_in=bf(conv_w_in[0]), conv_w=conv_w[0], w_out=bf(conv_w_out[0]),
              w_gate=bf(ffn_w_gate[0]), w_up=bf(ffn_w_up[0]), w_down=bf(ffn_w_down[0]),
              ple_gate=bf(ple_w_gate[0]), ple_proj=bf(ple_w_proj[0]))
    l1 = dict(g_kv=row(kv_norm_g), g_mix=row(norm_mix_g[1]), g_ffn=row(norm_ffn_g[1]),
              g_ple=row(norm_ple_g[1]), g_final=row(final_norm_g),
              w_kv=bf(jnp.concatenate([w_k, w_v], axis=1)), w_q=bf(w_q[0]), w_o=bf(w_o[0]),
              w_gate=bf(ffn_w_gate[1]), w_up=bf(ffn_w_up[1]), w_down=bf(ffn_w_down[1]),
              ple_gate=bf(ple_w_gate[1]), ple_proj=bf(ple_w_proj[1]))
    l0_names = ("g_mix", "g_ffn", "g_ple", "w_in", "conv_w", "w_out",
                "w_gate", "w_up", "w_down", "ple_gate", "ple_proj")
    l0_args = [l0[n] for n in l0_names]

    nt = seq // tm
    n_tiles = batch * nt
    cur = lambda s: jnp.minimum(s, n_tiles - 1)
    prev = lambda s: jnp.maximum(s - 1, 0)
    cur_tile = pl.BlockSpec((None, tm, d), lambda s: (cur(s) // nt, cur(s) % nt, 0))
    prev_tile = pl.BlockSpec((None, tm, d), lambda s: (prev(s) // nt, prev(s) % nt, 0))
    prev_ple = lambda layer: pl.BlockSpec((None, None, tm, p_prompt.shape[-1]),
                                          lambda s: (layer, prev(s) // nt, prev(s) % nt, 0))
    h1, conv_state_prompt = pl.pallas_call(
        functools.partial(_layer0_prompt_kernel, tiles_per_seq=nt, n_tiles=n_tiles),
        grid=(n_tiles + 1,),
        in_specs=[cur_tile, prev_ple(0)] + [_resident(a) for a in l0_args],
        out_specs=[prev_tile, pl.BlockSpec((None, CONV_WIDTH - 1, d), lambda s: (cur(s) // nt, 0, 0))],
        out_shape=[jax.ShapeDtypeStruct((batch, seq, d), _F32),
                   jax.ShapeDtypeStruct((batch, CONV_WIDTH - 1, d), _F32)],
        scratch_shapes=[pltpu.VMEM((tm + SUBLANES, d), _F32),
                        pltpu.VMEM((tm, d), _F32),
                        pltpu.VMEM((tm, d), _BF16)],
        compiler_params=_params(1),
        name="layer0_prompt",
    )(x_prompt, p_prompt, *l0_args)

    cos_p, lo_p, hi_p = _rope_tables(jnp.arange(seq, dtype=jnp.int32))
    l1_names = ("g_kv", "g_mix", "g_ffn", "g_ple", "g_final", "w_kv", "w_q", "w_o",
                "w_gate", "w_up", "w_down", "ple_gate", "ple_proj")
    l1_args = [l1[n] for n in l1_names]
    y_prompt, k_win_prompt, v_win_prompt = pl.pallas_call(
        functools.partial(_layer1_prompt_kernel, tiles_per_seq=nt, n_tiles=n_tiles),
        grid=(n_tiles + 1,),
        in_specs=[cur_tile, prev_ple(1)]
                 + [pl.BlockSpec((tm, LANES), lambda s: (cur(s) % nt, 0))] * 3
                 + [pl.BlockSpec(memory_space=pltpu.SMEM)] + [_resident(a) for a in l1_args],
        out_specs=[prev_tile,
                   pl.BlockSpec((None, WINDOW, KV_WIDTH), lambda s: (cur(s) // nt, 0, 0)),
                   pl.BlockSpec((None, WINDOW, KV_WIDTH), lambda s: (cur(s) // nt, 0, 0))],
        out_shape=[jax.ShapeDtypeStruct((batch, seq, d), _F32),
                   jax.ShapeDtypeStruct((batch, WINDOW, KV_WIDTH), _F32),
                   jax.ShapeDtypeStruct((batch, WINDOW, KV_WIDTH), _F32)],
        scratch_shapes=[pltpu.VMEM((N_KV_HEADS, BLOCK + tm, LANES), _BF16),
                        pltpu.VMEM((N_KV_HEADS, BLOCK + tm, LANES), _BF16),
                        pltpu.VMEM((tm, d), _F32),
                        pltpu.VMEM((tm, d), _BF16)],
        compiler_params=_params(1),
        name="layer1_prompt",
    )(h1, p_prompt, cos_p, lo_p, hi_p, sinks[0], *l1_args)

    xs = x_sample.reshape(dec, d)
    ps = p_sample.reshape(p_sample.shape[0], dec, p_sample.shape[-1])
    whole = lambda a: pl.BlockSpec(a.shape, lambda i: (0,) * a.ndim)
    s_in = [xs, ps[0], state_conv[0, :, 0, :], state_conv[0, :, 1, :]]
    h1s, u_s = pl.pallas_call(
        _layer0_sample_kernel,
        grid=(1,),
        in_specs=[whole(a) for a in s_in] + [_resident(a) for a in l0_args],
        out_specs=[pl.BlockSpec((dec, d), lambda i: (0, 0))] * 2,
        out_shape=[jax.ShapeDtypeStruct((dec, d), _F32)] * 2,
        compiler_params=_params(1),
        name="layer0_sample",
    )(*s_in, *l0_args)
    conv_state_sample = jnp.stack([state_conv[0, :, 1, :], u_s], axis=1)[None]

    cos_s, lo_s, hi_s = _rope_tables(jnp.full((dec,), PAST_LEN, dtype=jnp.int32))
    qkv_in = [h1s, cos_s, lo_s, hi_s, l1["g_kv"], l1["g_mix"], l1["w_kv"], bf(_slot_major(w_q[0], 1))]
    q_s, k_s, v_s = pl.pallas_call(
        _sample_qkv_kernel,
        grid=(1,),
        in_specs=[whole(a) for a in qkv_in[:6]] + [_resident(a) for a in qkv_in[6:]],
        out_specs=[pl.BlockSpec((dec, d), lambda i: (0, 0)),
                   pl.BlockSpec((dec, KV_WIDTH), lambda i: (0, 0)),
                   pl.BlockSpec((dec, KV_WIDTH), lambda i: (0, 0))],
        out_shape=[jax.ShapeDtypeStruct((dec, d), _F32),
                   jax.ShapeDtypeStruct((dec, KV_WIDTH), _F32),
                   jax.ShapeDtypeStruct((dec, KV_WIDTH), _F32)],
        compiler_params=_params(1),
        name="sample_qkv",
    )(*qkv_in)

    bc = SAMPLE_CHUNK
    sink_rows = sinks[0].reshape(N_KV_HEADS, GROUP).T.reshape(N_HEADS, 1)
    ck = cache_k_win.reshape(dec, w_buf, KV_WIDTH)
    cv = cache_v_win.reshape(dec, w_buf, KV_WIDTH)
    rows = lambda width: pl.BlockSpec((bc, width), lambda i: (i, 0))
    cache = pl.BlockSpec((bc, w_buf, KV_WIDTH), lambda i: (i, 0, 0))
    o_s, k_win_sample, v_win_sample = pl.pallas_call(
        _sample_attn_kernel,
        grid=(dec // bc,),
        in_specs=[rows(d), rows(KV_WIDTH), rows(KV_WIDTH), cache, cache,
                  pl.BlockSpec((N_HEADS, 1), lambda i: (0, 0))],
        out_specs=[rows(d), cache, cache],
        out_shape=[jax.ShapeDtypeStruct((dec, d), _F32),
                   jax.ShapeDtypeStruct((dec, w_buf, KV_WIDTH), _F32),
                   jax.ShapeDtypeStruct((dec, w_buf, KV_WIDTH), _F32)],
        compiler_params=_params(1),
        name="sample_attn",
    )(q_s, k_s, v_s, ck, cv, sink_rows)

    post_names = ("g_ffn", "g_ple", "g_final", "w_o", "w_gate", "w_up", "w_down", "ple_gate", "ple_proj")
    post_args = [bf(_slot_major(w_o[0], 0)) if n == "w_o" else l1[n] for n in post_names]
    post_in = [h1s, o_s, ps[1]]
    y_sample = pl.pallas_call(
        _sample_post_kernel,
        grid=(1,),
        in_specs=[whole(a) for a in post_in] + [_resident(a) for a in post_args],
        out_specs=pl.BlockSpec((dec, d), lambda i: (0, 0)),
        out_shape=jax.ShapeDtypeStruct((dec, d), _F32),
        compiler_params=_params(1),
        name="sample_post",
    )(*post_in, *post_args)

    kv_shape = (N_KV_HEADS, HEAD_DIM)
    return (y_prompt, y_sample.reshape(dec, 1, d),
            conv_state_prompt[None], conv_state_sample,
            k_win_prompt.reshape(batch, w_buf, *kv_shape), v_win_prompt.reshape(batch, w_buf, *kv_shape),
            k_win_sample.reshape(dec, w_buf, *kv_shape), v_win_sample.reshape(dec, w_buf, *kv_shape))
```

```python
import functools

import jax
import jax.numpy as jnp
import numpy as np
from jax import lax
from jax.experimental import pallas as pl
from jax.experimental.pallas import tpu as pltpu

D_MODEL = 1024
HEAD_DIM = 64
N_HEADS = 16
N_KV_HEADS = 4
GROUP = N_HEADS // N_KV_HEADS
KV_WIDTH = N_KV_HEADS * HEAD_DIM
ROT_DIM = HEAD_DIM // 4
ROPE_THETA = 500000.0
WINDOW = 128
BLOCK = 128
CONV_WIDTH = 3
PAST_LEN = 16384
RMS_EPS = 1e-6
NEG_INF = -1e30

LANES = 128
SUBLANES = 8
MXU_COLS = 256
CHUNK_COLS = 2 * MXU_COLS
VMEM_LIMIT_BYTES = 60000 * 1024

PROMPT_TILE = 256
SAMPLE_CHUNK = 8
LAYER0_WINDOWS = ((0.10, 0.92), (0.0, 1.0))
LAYER1_WINDOWS = ((0.04, 0.92), (0.0, 1.0))

_BF16 = jnp.bfloat16
_F32 = jnp.float32


def _rms_scale(x):
    return lax.rsqrt(jnp.mean(x * x, axis=-1, keepdims=True) + RMS_EPS)


def _sigmoid(x):
    return 1.0 / (1.0 + jnp.exp(-x))


def _bdot(a, w_ref):
    return jnp.dot(a.astype(_BF16), w_ref[...], preferred_element_type=_F32)


def _rope(x, cos, sin_lo, sin_hi):
    half = ROT_DIM // 2
    out = []
    for c in range(x.shape[1] // LANES):
        slab = x[:, c * LANES:(c + 1) * LANES]
        out.append(slab * cos
                   + pltpu.roll(slab, LANES - half, 1) * sin_lo
                   + pltpu.roll(slab, half, 1) * sin_hi)
    return jnp.concatenate(out, axis=1)


def _run(steps):
    try:
        while True:
            next(steps)
    except StopIteration as done:
        return done.value


def _interleave(*lanes):
    done = [0.0] * len(lanes)
    live = list(range(len(lanes)))

    def position(j):
        _, total, (start, end) = lanes[j]
        return start + (end - start) * done[j] / total

    while live:
        i = min(live, key=position)
        try:
            done[i] += next(lanes[i][0])
        except StopIteration:
            live.remove(i)


def _chunks(n_cols):
    return [(c, min(CHUNK_COLS, n_cols - c)) for c in range(0, n_cols, CHUNK_COLS)]


def _col_dot(a, w_ref, c0, width):
    return jnp.dot(a, w_ref[:, c0:c0 + width], preferred_element_type=_F32)


def _dot_cost(w_ref, width):
    return -(-w_ref.shape[0] // MXU_COLS) * (width // MXU_COLS)


def _chunks_cost(w_ref):
    return _dot_cost(w_ref, w_ref.shape[1])


def _dot_cols(a, w_ref):
    a = a.astype(_BF16)
    cols = []
    for c, width in _chunks(w_ref.shape[1]):
        cols.append(_col_dot(a, w_ref, c, width))
        yield _dot_cost(w_ref, width)
    return jnp.concatenate(cols, axis=1)


def _ffn_norm(h, g_ffn):
    return ((h * _rms_scale(h)) * g_ffn[...]).astype(_BF16)


def _ffn_and_ple_steps(h, hf, p, g_ple, w_gate, w_up, w_down, ple_gate, ple_proj):
    acts = []
    for c, width in _chunks(w_gate.shape[1]):
        g = _col_dot(hf, w_gate, c, width)
        yield _dot_cost(w_gate, width)
        u = _col_dot(hf, w_up, c, width)
        yield _dot_cost(w_up, width)
        acts.append(((g * _sigmoid(g)) * u).astype(_BF16))
    h = h + (yield from _dot_cols(jnp.concatenate(acts, axis=1), w_down))
    hp = (h * _rms_scale(h)) * g_ple[...]
    gate = _sigmoid((yield from _dot_cols(hp, ple_gate)))
    return h + gate * (yield from _dot_cols(p, ple_proj))


def _ffn_and_ple_cost(w_gate, w_down, ple_gate, ple_proj):
    return 2 * _chunks_cost(w_gate) + _chunks_cost(w_down) + _chunks_cost(ple_gate) + _chunks_cost(ple_proj)


def _conv_mixer_steps(x, g_mix, w_in, conv_w, w_out, history):
    hn = ((x * _rms_scale(x)) * g_mix[...]).astype(_BF16)
    yield 0
    us, convs = [], []
    for c, width in _chunks(D_MODEL):
        cols = slice(c, c + width)
        c_gate = _col_dot(hn, w_in, D_MODEL + c, width)
        yield _dot_cost(w_in, width)
        xin = _col_dot(hn, w_in, 2 * D_MODEL + c, width)
        yield _dot_cost(w_in, width)
        u = c_gate * xin
        back2, back1 = history(cols, u)
        conv = conv_w[0:1, cols] * back2
        conv = conv + conv_w[1:2, cols] * back1
        conv = conv + conv_w[2:3, cols] * u
        us.append(u)
        convs.append(conv)
    gated = []
    for j, (c, width) in enumerate(_chunks(D_MODEL)):
        b_gate = _col_dot(hn, w_in, c, width)
        yield _dot_cost(w_in, width)
        gated.append((b_gate * convs[j]).astype(_BF16))
    y = yield from _dot_cols(jnp.concatenate(gated, axis=1), w_out)
    return x + y, jnp.concatenate(us, axis=1)


def _conv_mixer_cost(w_in, w_out):
    return _chunks_cost(w_in) + _chunks_cost(w_out)


def _pipeline_bodies(step, n_tiles, body):
    pl.when(step == 0)(lambda: body(True, False))
    pl.when((step > 0) & (step < n_tiles))(lambda: body(True, True))
    pl.when(step == n_tiles)(lambda: body(False, True))


def _qkv(h, cos, sin_lo, sin_hi, g_kv, g_mix, w_kv, w_q):
    hs = h * _rms_scale(h)
    kv = _bdot(hs * g_kv[...], w_kv)
    k = _rope(kv[:, :KV_WIDTH], cos, sin_lo, sin_hi)
    v = kv[:, KV_WIDTH:]
    q = _rope(_bdot(hs * g_mix[...], w_q), cos, sin_lo, sin_hi) * (HEAD_DIM ** -0.5)
    return q, k, v


def _layer0_prompt_kernel(x_ref, p_ref, g_mix, g_ffn, g_ple, w_in, conv_w, w_out,
                          w_gate, w_up, w_down, ple_gate, ple_proj,
                          h_out, state_out, ubuf, hbuf, hfbuf, *, tiles_per_seq, n_tiles):
    tm = x_ref.shape[0]
    step = pl.program_id(0)

    def history(cols, u):
        ubuf[SUBLANES:SUBLANES + tm, cols] = u
        return (ubuf[SUBLANES - 2:SUBLANES - 2 + tm, cols], ubuf[SUBLANES - 1:SUBLANES - 1 + tm, cols])

    def ffn_half():
        out = yield from _ffn_and_ple_steps(hbuf[...], hfbuf[...], p_ref[...], g_ple,
                                            w_gate, w_up, w_down, ple_gate, ple_proj)
        h_out[...] = out

    def conv_half():
        h, _ = yield from _conv_mixer_steps(x_ref[...], g_mix, w_in, conv_w, w_out, history)
        state_out[...] = ubuf[SUBLANES + tm - (CONV_WIDTH - 1):SUBLANES + tm, :]
        ubuf[0:SUBLANES, :] = ubuf[tm:tm + SUBLANES, :]
        hbuf[...] = h
        hfbuf[...] = _ffn_norm(h, g_ffn)

    def body(first_half, second_half):
        lanes = []
        if first_half:
            zeros = jnp.zeros((SUBLANES, D_MODEL), _F32)
            fresh = lax.rem(step, tiles_per_seq) == 0
            ubuf[0:SUBLANES, :] = jnp.where(fresh, zeros, ubuf[0:SUBLANES, :]) if second_half else zeros
        if second_half:
            ffn = ffn_half()
            next(ffn)
            lanes.append((ffn, _ffn_and_ple_cost(w_gate, w_down, ple_gate, ple_proj), LAYER0_WINDOWS[1]))
        if first_half:
            conv = conv_half()
            next(conv)
            lanes.append((conv, _conv_mixer_cost(w_in, w_out), LAYER0_WINDOWS[0]))
        _interleave(*lanes)

    _pipeline_bodies(step, n_tiles, body)


def _layer0_sample_kernel(x_ref, p_ref, s0_ref, s1_ref, g_mix, g_ffn, g_ple, w_in, conv_w, w_out,
                          w_gate, w_up, w_down, ple_gate, ple_proj,
                          h_out, u_out):
    history = lambda cols, u: (s0_ref[:, cols], s1_ref[:, cols])
    h, u = _run(_conv_mixer_steps(x_ref[...], g_mix, w_in, conv_w, w_out, history))
    h_out[...] = _run(_ffn_and_ple_steps(h, _ffn_norm(h, g_ffn), p_ref[...], g_ple,
                                         w_gate, w_up, w_down, ple_gate, ple_proj))
    u_out[...] = u


def _layer1_prompt_kernel(h_ref, p_ref, cos_ref, slo_ref, shi_ref, sinks_ref,
                          g_kv, g_mix, g_ffn, g_ple, g_final,
                          w_kv, w_q, w_o, w_gate, w_up, w_down, ple_gate, ple_proj,
                          y_out, kwin_out, vwin_out, k2buf, v2buf, hbuf, obuf, *, tiles_per_seq, n_tiles):
    tm = h_ref.shape[0]
    step = pl.program_id(0)
    attn_cost = GROUP * BLOCK / tm

    def dense_half():
        hp = hbuf[...] + (yield from _dot_cols(obuf[...], w_o))
        hp = yield from _ffn_and_ple_steps(hp, _ffn_norm(hp, g_ffn), p_ref[...], g_ple,
                                           w_gate, w_up, w_down, ple_gate, ple_proj)
        y_out[...] = (hp * _rms_scale(hp)) * g_final[...]

    def attention_half():
        h = h_ref[...]
        hs = h * _rms_scale(h)
        h_kv = (hs * g_kv[...]).astype(_BF16)
        h_q = (hs * g_mix[...]).astype(_BF16)
        yield 0
        kv = yield from _dot_cols(h_kv, w_kv)
        q = yield from _dot_cols(h_q, w_q)
        cos, sin_lo, sin_hi = cos_ref[...], slo_ref[...], shi_ref[...]
        k = _rope(kv[:, :KV_WIDTH], cos, sin_lo, sin_hi)
        v = kv[:, KV_WIDTH:]
        qb = (_rope(q, cos, sin_lo, sin_hi) * (HEAD_DIM ** -0.5)).astype(_BF16)
        kwin_out[...] = k[tm - WINDOW:, :]
        vwin_out[...] = v[tm - WINDOW:, :]

        lane = lax.broadcasted_iota(jnp.int32, (1, LANES), 1)
        low = lane < HEAD_DIM
        for arr, buf in ((k, k2buf), (v, v2buf)):
            for s in range(KV_WIDTH // LANES):
                slab = arr[:, s * LANES:(s + 1) * LANES]
                swapped = pltpu.roll(slab, HEAD_DIM, 1)
                buf[2 * s, BLOCK:BLOCK + tm, :] = jnp.where(low, slab, swapped).astype(_BF16)
                buf[2 * s + 1, BLOCK:BLOCK + tm, :] = jnp.where(low, swapped, slab).astype(_BF16)

        row = lax.broadcasted_iota(jnp.int32, (BLOCK, 2 * BLOCK), 0)
        col = lax.broadcasted_iota(jnp.int32, (BLOCK, 2 * BLOCK), 1)
        diff = row + BLOCK - col
        band = (diff >= 0) & (diff <= WINDOW)
        first_col = jnp.where(lax.rem(step, tiles_per_seq) == 0, BLOCK, 0)
        zero = jnp.zeros((), _BF16)
        ones = jnp.ones((2 * BLOCK, LANES), _BF16)
        units = [(jb, g) for jb in range(tm // BLOCK) for g in range(N_KV_HEADS)]

        def scores(jb, g):
            r0 = jb * BLOCK
            parts = []
            for s in (2 * g, 2 * g + 1):
                slab = qb[r0:r0 + BLOCK, s * LANES:(s + 1) * LANES]
                parts.append(jnp.where(low, slab, zero))
                parts.append(jnp.where(low, zero, slab))
            qs = jnp.concatenate(parts, axis=0)
            return lax.dot_general(qs, k2buf[g, r0:r0 + 2 * BLOCK, :],
                                   (((1,), (1,)), ((), ())), preferred_element_type=_F32)

        def weights(jb, g, sc):
            valid = band & (col >= first_col) if jb == 0 else band
            es, tail = [], []
            for i in range(GROUP):
                sink = sinks_ref[GROUP * g + i]
                sp = jnp.where(valid, sc[i * BLOCK:(i + 1) * BLOCK], NEG_INF)
                m = jnp.maximum(jnp.max(sp, axis=-1, keepdims=True), sink)
                es.append(jnp.exp(sp - m).astype(_BF16))
                tail.append(jnp.exp(sink - m))
            return jnp.concatenate(es, axis=0), tail

        def values(jb, g, e, tail):
            r0 = jb * BLOCK
            vv = jnp.concatenate([v2buf[g, r0:r0 + 2 * BLOCK, :], ones], axis=1)
            o3 = jnp.dot(e, vv, preferred_element_type=_F32)
            o2 = []
            for i in range(GROUP):
                piece = o3[i * BLOCK:(i + 1) * BLOCK]
                o2.append(piece[:, :LANES] * (1.0 / (piece[:, LANES:] + tail[i])))
            obuf[r0:r0 + BLOCK, (2 * g) * LANES:(2 * g + 1) * LANES] = (
                jnp.where(low, o2[0], o2[1]).astype(_BF16))
            obuf[r0:r0 + BLOCK, (2 * g + 1) * LANES:(2 * g + 2) * LANES] = (
                jnp.where(low, o2[2], o2[3]).astype(_BF16))

        sc, ew = {}, {}
        for n in range(len(units) + 2):
            if n < len(units):
                sc[n] = scores(*units[n])
                yield attn_cost
            if 0 <= n - 1 < len(units):
                ew[n - 1] = weights(*units[n - 1], sc.pop(n - 1))
                yield 0.0
            if 0 <= n - 2 < len(units):
                values(*units[n - 2], *ew.pop(n - 2))
                yield attn_cost
        hbuf[...] = h
        k2buf[:, 0:BLOCK, :] = k2buf[:, tm:tm + BLOCK, :]
        v2buf[:, 0:BLOCK, :] = v2buf[:, tm:tm + BLOCK, :]

    def body(first_half, second_half):
        lanes = []
        if first_half and not second_half:
            zeros = jnp.zeros((N_KV_HEADS, BLOCK, LANES), _BF16)
            k2buf[:, 0:BLOCK, :] = zeros
            v2buf[:, 0:BLOCK, :] = zeros
        if second_half:
            dense = dense_half()
            next(dense)
            lanes.append((dense, _chunks_cost(w_o) + _ffn_and_ple_cost(w_gate, w_down, ple_gate, ple_proj),
                          LAYER1_WINDOWS[1]))
        if first_half:
            n_units = (tm // BLOCK) * N_KV_HEADS
            attention = attention_half()
            next(attention)
            lanes.append((attention,
                          _chunks_cost(w_kv) + _chunks_cost(w_q) + 2 * n_units * attn_cost,
                          LAYER1_WINDOWS[0]))
        _interleave(*lanes)

    _pipeline_bodies(step, n_tiles, body)


def _sample_qkv_kernel(h_ref, cos_ref, slo_ref, shi_ref, g_kv, g_mix, w_kv, w_q,
                       q_out, k_out, v_out):
    q, k, v = _qkv(h_ref[...], cos_ref[...], slo_ref[...], shi_ref[...], g_kv, g_mix, w_kv, w_q)
    q_out[...] = q
    k_out[...] = k
    v_out[...] = v


def _sample_attn_kernel(q_ref, kn_ref, vn_ref, ck_ref, cv_ref, sink_ref,
                        o_out, kwin_out, vwin_out):
    lane = lax.broadcasted_iota(jnp.int32, (N_HEADS, KV_WIDTH), 1)
    r16 = lax.broadcasted_iota(jnp.int32, (N_HEADS, KV_WIDTH), 0)
    slot = r16 // N_KV_HEADS
    kvh = r16 - slot * N_KV_HEADS
    own = (lane >= kvh * HEAD_DIM) & (lane < (kvh + 1) * HEAD_DIM)
    sink = sink_ref[...]
    w = ck_ref.shape[1]
    for b in range(q_ref.shape[0]):
        kn = kn_ref[b:b + 1, :]
        vn = vn_ref[b:b + 1, :]
        kwin_out[b, 0:w - 1, :] = ck_ref[b, 1:w, :]
        kwin_out[b, w - 1:w, :] = kn
        vwin_out[b, 0:w - 1, :] = cv_ref[b, 1:w, :]
        vwin_out[b, w - 1:w, :] = vn
        lhs = jnp.zeros((N_HEADS, KV_WIDTH), _F32)
        for i in range(GROUP):
            q_i = jnp.broadcast_to(q_ref[b:b + 1, i * KV_WIDTH:(i + 1) * KV_WIDTH], (N_HEADS, KV_WIDTH))
            lhs = jnp.where(own & (slot == i), q_i, lhs)
        lhs = lhs.astype(_BF16)
        s_old = lax.dot_general(lhs, ck_ref[b].astype(_BF16), (((1,), (1,)), ((), ())),
                                preferred_element_type=_F32)
        s_new = jnp.sum(lhs.astype(_F32) * kn.astype(_BF16).astype(_F32), axis=-1, keepdims=True)
        m = jnp.maximum(jnp.maximum(jnp.max(s_old, axis=-1, keepdims=True), s_new), sink)
        e_old = jnp.exp(s_old - m)
        e_new = jnp.exp(s_new - m)
        inv = 1.0 / (jnp.sum(e_old, axis=-1, keepdims=True) + e_new + jnp.exp(sink - m))
        o16 = jnp.dot(e_old.astype(_BF16), cv_ref[b].astype(_BF16), preferred_element_type=_F32)
        o16 = (o16 + e_new.astype(_BF16).astype(_F32) * vn.astype(_BF16).astype(_F32)) * inv
        for i in range(GROUP):
            picked = jnp.where(own & (slot == i), o16, 0.0)
            o_out[b:b + 1, i * KV_WIDTH:(i + 1) * KV_WIDTH] = jnp.sum(picked, axis=0, keepdims=True)


def _sample_post_kernel(h_ref, o_ref, p_ref, g_ffn, g_ple, g_final,
                        w_o, w_gate, w_up, w_down, ple_gate, ple_proj, y_out):
    h = h_ref[...] + _bdot(o_ref[...], w_o)
    h = _run(_ffn_and_ple_steps(h, _ffn_norm(h, g_ffn), p_ref[...], g_ple,
                                w_gate, w_up, w_down, ple_gate, ple_proj))
    y_out[...] = (h * _rms_scale(h)) * g_final[...]


class _Layer:
    def __init__(self, stacked, layer):
        self.stacked, self.layer = stacked, layer


def _operand(x):
    return x.stacked if isinstance(x, _Layer) else x


def _resident(x):
    if isinstance(x, _Layer):
        rest = x.stacked.shape[1:]
        index = (x.layer,) + (0,) * len(rest)
        return pl.BlockSpec((None,) + rest, lambda *_: index, pipeline_mode=pl.Buffered(1))
    nd = x.ndim
    return pl.BlockSpec(x.shape, lambda *_: (0,) * nd, pipeline_mode=pl.Buffered(1))


def _params(n_grid_dims):
    return pltpu.CompilerParams(dimension_semantics=("arbitrary",) * n_grid_dims,
                                vmem_limit_bytes=VMEM_LIMIT_BYTES)


def _rope_tables(pos):
    half = ROT_DIM // 2
    inv_freq = np.power(np.float64(ROPE_THETA), -np.arange(half, dtype=np.float64) / half)
    ang = pos.astype(np.float64)[:, None] * inv_freq[None, :]
    cos, sin = np.cos(ang), np.sin(ang)
    n = pos.shape[0]
    pad = np.zeros((n, HEAD_DIM - ROT_DIM))
    zero = np.zeros((n, half))
    cos_h = np.concatenate([cos, cos, pad + 1.0], axis=1)
    lo_h = np.concatenate([-sin, zero, pad], axis=1)
    hi_h = np.concatenate([zero, sin, pad], axis=1)
    rep = LANES // HEAD_DIM
    return tuple(jnp.asarray(np.tile(a, (1, rep)), dtype=_F32) for a in (cos_h, lo_h, hi_h))


def _slot_major(w, axis):
    shape = w.shape
    split = shape[:axis] + (N_KV_HEADS, GROUP, HEAD_DIM) + shape[axis + 1:]
    return jnp.swapaxes(w.reshape(split), axis, axis + 1).reshape(shape)


def kernel(x_prompt, x_sample, state_conv, cache_k_win, cache_v_win, p_prompt, p_sample,
           norm_mix_g, norm_ffn_g, norm_ple_g, kv_norm_g, final_norm_g,
           conv_w_in, conv_w, conv_w_out, w_k, w_v, w_q, sinks, w_o,
           ffn_w_gate, ffn_w_up, ffn_w_down, ple_w_proj, ple_w_gate):
    batch, seq, d = x_prompt.shape
    dec = x_sample.shape[0]
    w_buf = cache_k_win.shape[1]
    tm = PROMPT_TILE
    assert d == D_MODEL and seq % tm == 0 and tm % BLOCK == 0 and tm >= WINDOW
    assert x_sample.shape[1] == 1 and w_buf == WINDOW and dec % SAMPLE_CHUNK == 0

    bf = lambda a: a.astype(_BF16)
    row = lambda a: a.reshape(1, -1)
    rows = lambda a: a.reshape(a.shape[0], 1, a.shape[1])
    stacked = dict(g_mix=rows(norm_mix_g), g_ffn=rows(norm_ffn_g), g_ple=rows(norm_ple_g),
                   w_gate=bf(ffn_w_gate), w_up=bf(ffn_w_up), w_down=bf(ffn_w_down),
                   ple_gate=bf(ple_w_gate), ple_proj=bf(ple_w_proj))
    l0 = {n: _Layer(a, 0) for n, a in stacked.items()}
    l0.update(w_in=_Layer(bf(conv_w_in), 0), conv_w=_Layer(conv_w, 0), w_out=_Layer(bf(conv_w_out), 0))
    l1 = {n: _Layer(a, 1) for n, a in stacked.items()}
    l1.update(g_kv=row(kv_norm_g), g_final=row(final_norm_g),
              w_kv=bf(jnp.concatenate([w_k, w_v], axis=1)), w_q=_Layer(bf(w_q), 0), w_o=_Layer(bf(w_o), 0))
    l0_names = ("g_mix", "g_ffn", "g_ple", "w_in", "conv_w", "w_out",
                "w_gate", "w_up", "w_down", "ple_gate", "ple_proj")
    l0_args = [l0[n] for n in l0_names]

    nt = seq // tm
    n_tiles = batch * nt
    cur = lambda s: jnp.minimum(s, n_tiles - 1)
    prev = lambda s: jnp.maximum(s - 1, 0)
    cur_tile = pl.BlockSpec((None, tm, d), lambda s: (cur(s) // nt, cur(s) % nt, 0))
    prev_tile = pl.BlockSpec((None, tm, d), lambda s: (prev(s) // nt, prev(s) % nt, 0))
    prev_ple = lambda layer: pl.BlockSpec((None, None, tm, p_prompt.shape[-1]),
                                          lambda s: (layer, prev(s) // nt, prev(s) % nt, 0))
    h1, conv_state_prompt = pl.pallas_call(
        functools.partial(_layer0_prompt_kernel, tiles_per_seq=nt, n_tiles=n_tiles),
        grid=(n_tiles + 1,),
        in_specs=[cur_tile, prev_ple(0)] + [_resident(a) for a in l0_args],
        out_specs=[prev_tile, pl.BlockSpec((None, CONV_WIDTH - 1, d), lambda s: (cur(s) // nt, 0, 0))],
        out_shape=[jax.ShapeDtypeStruct((batch, seq, d), _F32),
                   jax.ShapeDtypeStruct((batch, CONV_WIDTH - 1, d), _F32)],
        scratch_shapes=[pltpu.VMEM((tm + SUBLANES, d), _F32),
                        pltpu.VMEM((tm, d), _F32),
                        pltpu.VMEM((tm, d), _BF16)],
        compiler_params=_params(1),
        name="layer0_prompt",
    )(x_prompt, p_prompt, *map(_operand, l0_args))

    cos_p, lo_p, hi_p = _rope_tables(np.arange(seq))
    l1_names = ("g_kv", "g_mix", "g_ffn", "g_ple", "g_final", "w_kv", "w_q", "w_o",
                "w_gate", "w_up", "w_down", "ple_gate", "ple_proj")
    l1_args = [l1[n] for n in l1_names]
    y_prompt, k_win_prompt, v_win_prompt = pl.pallas_call(
        functools.partial(_layer1_prompt_kernel, tiles_per_seq=nt, n_tiles=n_tiles),
        grid=(n_tiles + 1,),
        in_specs=[cur_tile, prev_ple(1)]
                 + [pl.BlockSpec((tm, LANES), lambda s: (cur(s) % nt, 0))] * 3
                 + [pl.BlockSpec(memory_space=pltpu.SMEM)] + [_resident(a) for a in l1_args],
        out_specs=[prev_tile,
                   pl.BlockSpec((None, WINDOW, KV_WIDTH), lambda s: (cur(s) // nt, 0, 0)),
                   pl.BlockSpec((None, WINDOW, KV_WIDTH), lambda s: (cur(s) // nt, 0, 0))],
        out_shape=[jax.ShapeDtypeStruct((batch, seq, d), _F32),
                   jax.ShapeDtypeStruct((batch, WINDOW, KV_WIDTH), _F32),
                   jax.ShapeDtypeStruct((batch, WINDOW, KV_WIDTH), _F32)],
        scratch_shapes=[pltpu.VMEM((N_KV_HEADS, BLOCK + tm, LANES), _BF16),
                        pltpu.VMEM((N_KV_HEADS, BLOCK + tm, LANES), _BF16),
                        pltpu.VMEM((tm, d), _F32),
                        pltpu.VMEM((tm, d), _BF16)],
        compiler_params=_params(1),
        name="layer1_prompt",
    )(h1, p_prompt, cos_p, lo_p, hi_p, sinks[0], *map(_operand, l1_args))

    xs = x_sample.reshape(dec, d)
    ps = p_sample.reshape(p_sample.shape[0], dec, p_sample.shape[-1])
    whole = lambda a: pl.BlockSpec(a.shape, lambda i: (0,) * a.ndim)
    s_in = [xs, ps[0], state_conv[0, :, 0, :], state_conv[0, :, 1, :]]
    h1s, u_s = pl.pallas_call(
        _layer0_sample_kernel,
        grid=(1,),
        in_specs=[whole(a) for a in s_in] + [_resident(a) for a in l0_args],
        out_specs=[pl.BlockSpec((dec, d), lambda i: (0, 0))] * 2,
        out_shape=[jax.ShapeDtypeStruct((dec, d), _F32)] * 2,
        compiler_params=_params(1),
        name="layer0_sample",
    )(*s_in, *map(_operand, l0_args))
    conv_state_sample = jnp.stack([state_conv[0, :, 1, :], u_s], axis=1)[None]

    cos_s, lo_s, hi_s = _rope_tables(np.full((dec,), PAST_LEN))
    qkv_small = [h1s, cos_s, lo_s, hi_s, l1["g_kv"]]
    qkv_weights = [l1["g_mix"], l1["w_kv"], bf(_slot_major(w_q[0], 1))]
    q_s, k_s, v_s = pl.pallas_call(
        _sample_qkv_kernel,
        grid=(1,),
        in_specs=[whole(a) for a in qkv_small] + [_resident(a) for a in qkv_weights],
        out_specs=[pl.BlockSpec((dec, d), lambda i: (0, 0)),
                   pl.BlockSpec((dec, KV_WIDTH), lambda i: (0, 0)),
                   pl.BlockSpec((dec, KV_WIDTH), lambda i: (0, 0))],
        out_shape=[jax.ShapeDtypeStruct((dec, d), _F32),
                   jax.ShapeDtypeStruct((dec, KV_WIDTH), _F32),
                   jax.ShapeDtypeStruct((dec, KV_WIDTH), _F32)],
        compiler_params=_params(1),
        name="sample_qkv",
    )(*qkv_small, *map(_operand, qkv_weights))

    bc = SAMPLE_CHUNK
    sink_rows = sinks[0].reshape(N_KV_HEADS, GROUP).T.reshape(N_HEADS, 1)
    ck = cache_k_win.reshape(dec, w_buf, KV_WIDTH)
    cv = cache_v_win.reshape(dec, w_buf, KV_WIDTH)
    rows = lambda width: pl.BlockSpec((bc, width), lambda i: (i, 0))
    cache = pl.BlockSpec((bc, w_buf, KV_WIDTH), lambda i: (i, 0, 0))
    o_s, k_win_sample, v_win_sample = pl.pallas_call(
        _sample_attn_kernel,
        grid=(dec // bc,),
        in_specs=[rows(d), rows(KV_WIDTH), rows(KV_WIDTH), cache, cache,
                  pl.BlockSpec((N_HEADS, 1), lambda i: (0, 0))],
        out_specs=[rows(d), cache, cache],
        out_shape=[jax.ShapeDtypeStruct((dec, d), _F32),
                   jax.ShapeDtypeStruct((dec, w_buf, KV_WIDTH), _F32),
                   jax.ShapeDtypeStruct((dec, w_buf, KV_WIDTH), _F32)],
        compiler_params=_params(1),
        name="sample_attn",
    )(q_s, k_s, v_s, ck, cv, sink_rows)

    post_names = ("g_ffn", "g_ple", "g_final", "w_o", "w_gate", "w_up", "w_down", "ple_gate", "ple_proj")
    post_args = [bf(_slot_major(w_o[0], 0)) if n == "w_o" else l1[n] for n in post_names]
    post_in = [h1s, o_s, ps[1]]
    y_sample = pl.pallas_call(
        _sample_post_kernel,
        grid=(1,),
        in_specs=[whole(a) for a in post_in] + [_resident(a) for a in post_args],
        out_specs=pl.BlockSpec((dec, d), lambda i: (0, 0)),
        out_shape=jax.ShapeDtypeStruct((dec, d), _F32),
        compiler_params=_params(1),
        name="sample_post",
    )(*post_in, *map(_operand, post_args))

    kv_shape = (N_KV_HEADS, HEAD_DIM)
    return (y_prompt, y_sample.reshape(dec, 1, d),
            conv_state_prompt[None], conv_state_sample,
            k_win_prompt.reshape(batch, w_buf, *kv_shape), v_win_prompt.reshape(batch, w_buf, *kv_shape),
            k_win_sample.reshape(dec, w_buf, *kv_shape), v_win_sample.reshape(dec, w_buf, *kv_shape))
```

```python
import functools

import jax
import jax.numpy as jnp
import numpy as np
from jax import lax
from jax.experimental import pallas as pl
from jax.experimental.pallas import tpu as pltpu

D_MODEL = 1024
HEAD_DIM = 64
N_HEADS = 16
N_KV_HEADS = 4
GROUP = N_HEADS // N_KV_HEADS
KV_WIDTH = N_KV_HEADS * HEAD_DIM
ROT_DIM = HEAD_DIM // 4
ROPE_THETA = 500000.0
WINDOW = 128
BLOCK = 128
CONV_WIDTH = 3
PAST_LEN = 16384
RMS_EPS = 1e-6
NEG_INF = -1e30

LANES = 128
SUBLANES = 8
MXU_COLS = 256
CHUNK_COLS = 2 * MXU_COLS
VMEM_LIMIT_BYTES = 60000 * 1024

PROMPT_TILE = 256
SAMPLE_DECODE_WINDOW = (0.05, 0.85)
LAYER0_WINDOWS = ((0.10, 0.92), (0.0, 1.0))
LAYER1_WINDOWS = ((0.04, 0.92), (0.0, 1.0))

_BF16 = jnp.bfloat16
_F32 = jnp.float32


def _rms_scale(x):
    return lax.rsqrt(jnp.mean(x * x, axis=-1, keepdims=True) + RMS_EPS)


def _sigmoid(x):
    return 1.0 / (1.0 + jnp.exp(-x))


def _rope(x, cos, sin_lo, sin_hi):
    half = ROT_DIM // 2
    out = []
    for c in range(x.shape[1] // LANES):
        slab = x[:, c * LANES:(c + 1) * LANES]
        out.append(slab * cos
                   + pltpu.roll(slab, LANES - half, 1) * sin_lo
                   + pltpu.roll(slab, half, 1) * sin_hi)
    return jnp.concatenate(out, axis=1)


def _interleave(*lanes):
    done = [0.0] * len(lanes)
    live = list(range(len(lanes)))

    def position(j):
        _, total, (start, end) = lanes[j]
        return start + (end - start) * done[j] / total

    while live:
        i = min(live, key=position)
        try:
            done[i] += next(lanes[i][0])
        except StopIteration:
            live.remove(i)


def _scaled(steps, factor):
    try:
        while True:
            yield next(steps) * factor
    except StopIteration as done:
        return done.value


def _chunks(n_cols):
    return [(c, min(CHUNK_COLS, n_cols - c)) for c in range(0, n_cols, CHUNK_COLS)]


def _col_dot(a, w_ref, c0, width):
    return jnp.dot(a, w_ref[:, c0:c0 + width], preferred_element_type=_F32)


def _dot_cost(w_ref, width):
    return -(-w_ref.shape[0] // MXU_COLS) * (width // MXU_COLS)


def _chunks_cost(w_ref):
    return _dot_cost(w_ref, w_ref.shape[1])


def _dot_cols(a, w_ref):
    a = a.astype(_BF16)
    cols = []
    for c, width in _chunks(w_ref.shape[1]):
        cols.append(_col_dot(a, w_ref, c, width))
        yield _dot_cost(w_ref, width)
    return jnp.concatenate(cols, axis=1)


def _ffn_norm(h, g_ffn):
    return ((h * _rms_scale(h)) * g_ffn[...]).astype(_BF16)


def _ffn_and_ple_steps(h, hf, p, g_ple, w_gate, w_up, w_down, ple_gate, ple_proj):
    acts = []
    for c, width in _chunks(w_gate.shape[1]):
        g = _col_dot(hf, w_gate, c, width)
        yield _dot_cost(w_gate, width)
        u = _col_dot(hf, w_up, c, width)
        yield _dot_cost(w_up, width)
        acts.append(((g * _sigmoid(g)) * u).astype(_BF16))
    h = h + (yield from _dot_cols(jnp.concatenate(acts, axis=1), w_down))
    hp = (h * _rms_scale(h)) * g_ple[...]
    gate = _sigmoid((yield from _dot_cols(hp, ple_gate)))
    return h + gate * (yield from _dot_cols(p, ple_proj))


def _ffn_and_ple_cost(w_gate, w_down, ple_gate, ple_proj):
    return 2 * _chunks_cost(w_gate) + _chunks_cost(w_down) + _chunks_cost(ple_gate) + _chunks_cost(ple_proj)


def _conv_mixer_steps(x, g_mix, w_in, conv_w, w_out, history):
    hn = ((x * _rms_scale(x)) * g_mix[...]).astype(_BF16)
    yield 0
    us, convs = [], []
    for c, width in _chunks(D_MODEL):
        cols = slice(c, c + width)
        c_gate = _col_dot(hn, w_in, D_MODEL + c, width)
        yield _dot_cost(w_in, width)
        xin = _col_dot(hn, w_in, 2 * D_MODEL + c, width)
        yield _dot_cost(w_in, width)
        u = c_gate * xin
        back2, back1 = history(cols, u)
        conv = conv_w[0:1, cols] * back2
        conv = conv + conv_w[1:2, cols] * back1
        conv = conv + conv_w[2:3, cols] * u
        us.append(u)
        convs.append(conv)
    gated = []
    for j, (c, width) in enumerate(_chunks(D_MODEL)):
        b_gate = _col_dot(hn, w_in, c, width)
        yield _dot_cost(w_in, width)
        gated.append((b_gate * convs[j]).astype(_BF16))
    y = yield from _dot_cols(jnp.concatenate(gated, axis=1), w_out)
    return x + y, jnp.concatenate(us, axis=1)


def _conv_mixer_cost(w_in, w_out):
    return _chunks_cost(w_in) + _chunks_cost(w_out)


def _pipeline_bodies(step, n_tiles, body):
    pl.when(step == 0)(lambda: body(True, False))
    pl.when((step > 0) & (step < n_tiles))(lambda: body(True, True))
    pl.when(step == n_tiles)(lambda: body(False, True))


def _layer0_kernel(x_ref, p_ref, xs_ref, ps_ref, s0_ref, s1_ref,
                   g_mix, g_ffn, g_ple, w_in, conv_w, w_out,
                   w_gate, w_up, w_down, ple_gate, ple_proj,
                   h_out, state_out, hs_out, us_out, ubuf, hbuf, hfbuf, *, tiles_per_seq, n_tiles):
    tm = x_ref.shape[0]
    step = pl.program_id(0)
    ffn_cost = _ffn_and_ple_cost(w_gate, w_down, ple_gate, ple_proj)
    conv_cost = _conv_mixer_cost(w_in, w_out)

    def history(cols, u):
        ubuf[SUBLANES:SUBLANES + tm, cols] = u
        return (ubuf[SUBLANES - 2:SUBLANES - 2 + tm, cols], ubuf[SUBLANES - 1:SUBLANES - 1 + tm, cols])

    def ffn_half():
        out = yield from _ffn_and_ple_steps(hbuf[...], hfbuf[...], p_ref[...], g_ple,
                                            w_gate, w_up, w_down, ple_gate, ple_proj)
        h_out[...] = out

    def conv_half():
        h, _ = yield from _conv_mixer_steps(x_ref[...], g_mix, w_in, conv_w, w_out, history)
        state_out[...] = ubuf[SUBLANES + tm - (CONV_WIDTH - 1):SUBLANES + tm, :]
        ubuf[0:SUBLANES, :] = ubuf[tm:tm + SUBLANES, :]
        hbuf[...] = h
        hfbuf[...] = _ffn_norm(h, g_ffn)

    def sample_layer():
        history = lambda cols, u: (s0_ref[:, cols], s1_ref[:, cols])
        h, u = yield from _conv_mixer_steps(xs_ref[...], g_mix, w_in, conv_w, w_out, history)
        out = yield from _ffn_and_ple_steps(h, _ffn_norm(h, g_ffn), ps_ref[...], g_ple,
                                            w_gate, w_up, w_down, ple_gate, ple_proj)
        hs_out[...] = out
        us_out[...] = u

    def body(first_half, second_half):
        lanes = []
        if first_half:
            zeros = jnp.zeros((SUBLANES, D_MODEL), _F32)
            fresh = lax.rem(step, tiles_per_seq) == 0
            ubuf[0:SUBLANES, :] = jnp.where(fresh, zeros, ubuf[0:SUBLANES, :]) if second_half else zeros
        if second_half:
            ffn = ffn_half()
            next(ffn)
            lanes.append((ffn, ffn_cost, LAYER0_WINDOWS[1]))
        if first_half:
            conv = conv_half()
            next(conv)
            lanes.append((conv, conv_cost, LAYER0_WINDOWS[0]))
        else:
            rows = xs_ref.shape[0] / tm
            lanes.append((_scaled(sample_layer(), rows), (conv_cost + ffn_cost) * rows, (0.0, 1.0)))
        _interleave(*lanes)

    _pipeline_bodies(step, n_tiles, body)


def _qkv_steps(h, cos, sin_lo, sin_hi, g_kv, g_mix, w_kv, w_q):
    hs = h * _rms_scale(h)
    h_kv = (hs * g_kv[...]).astype(_BF16)
    h_q = (hs * g_mix[...]).astype(_BF16)
    yield 0
    kv = yield from _dot_cols(h_kv, w_kv)
    q = yield from _dot_cols(h_q, w_q)
    k = _rope(kv[:, :KV_WIDTH], cos, sin_lo, sin_hi)
    v = kv[:, KV_WIDTH:]
    q = _rope(q, cos, sin_lo, sin_hi) * (HEAD_DIM ** -0.5)
    return q, k, v


def _regroup_heads(x, order):
    low = lax.broadcasted_iota(jnp.int32, (1, LANES), 1) < HEAD_DIM
    out = []
    for s in range(x.shape[1] // LANES):
        halves = []
        for half in range(2):
            src_head = order[2 * s + half]
            slab = x[:, (src_head // 2) * LANES:(src_head // 2 + 1) * LANES]
            halves.append(slab if src_head % 2 == half else pltpu.roll(slab, HEAD_DIM, 1))
        out.append(jnp.where(low, halves[0], halves[1]))
    return jnp.concatenate(out, axis=1)


_SLOT_SWAP = [GROUP * (p % N_KV_HEADS) + p // N_KV_HEADS for p in range(N_HEADS)]


def _layer1_kernel(h_ref, p_ref, cos_ref, slo_ref, shi_ref, sinks_ref,
                   hs_ref, pss_ref, cos_s, slo_s, shi_s, sink_rows, ck_ref, cv_ref,
                   g_kv, g_mix, g_ffn, g_ple, g_final,
                   w_kv, w_q, w_o, w_gate, w_up, w_down, ple_gate, ple_proj,
                   y_out, kwin_out, vwin_out, ys_out, kwins_out, vwins_out,
                   k2buf, v2buf, hbuf, obuf, qs_buf, ks_buf, vs_buf, os_buf, *, tiles_per_seq, n_tiles):
    tm = h_ref.shape[0]
    step = pl.program_id(0)
    attn_cost = GROUP * BLOCK / tm
    dense_cost = _chunks_cost(w_o) + _ffn_and_ple_cost(w_gate, w_down, ple_gate, ple_proj)
    per_step = ck_ref.shape[0]
    sample_rows = hs_ref.shape[0] / tm

    def dense_half():
        hp = hbuf[...] + (yield from _dot_cols(obuf[...], w_o))
        hp = yield from _ffn_and_ple_steps(hp, _ffn_norm(hp, g_ffn), p_ref[...], g_ple,
                                           w_gate, w_up, w_down, ple_gate, ple_proj)
        y_out[...] = (hp * _rms_scale(hp)) * g_final[...]

    def sample_qkv():
        q, k, v = yield from _qkv_steps(hs_ref[...], cos_s[...], slo_s[...], shi_s[...],
                                        g_kv, g_mix, w_kv, w_q)
        qs_buf[...] = _regroup_heads(q, _SLOT_SWAP)
        ks_buf[...] = k
        vs_buf[...] = v

    def sample_decode(j):
        r = (step - 1) * per_step + j
        w = ck_ref.shape[1]
        lane = lax.broadcasted_iota(jnp.int32, (N_HEADS, KV_WIDTH), 1)
        r16 = lax.broadcasted_iota(jnp.int32, (N_HEADS, KV_WIDTH), 0)
        slot = r16 // N_KV_HEADS
        kvh = r16 - slot * N_KV_HEADS
        own = (lane >= kvh * HEAD_DIM) & (lane < (kvh + 1) * HEAD_DIM)
        sink = sink_rows[...]
        kn = ks_buf[pl.ds(r, 1), :]
        vn = vs_buf[pl.ds(r, 1), :]
        kwins_out[j, 0:w - 1, :] = ck_ref[j, 1:w, :]
        kwins_out[j, w - 1:w, :] = kn
        vwins_out[j, 0:w - 1, :] = cv_ref[j, 1:w, :]
        vwins_out[j, w - 1:w, :] = vn
        lhs = jnp.zeros((N_HEADS, KV_WIDTH), _F32)
        for i in range(GROUP):
            q_i = jnp.broadcast_to(qs_buf[pl.ds(r, 1), i * KV_WIDTH:(i + 1) * KV_WIDTH], (N_HEADS, KV_WIDTH))
            lhs = jnp.where(own & (slot == i), q_i, lhs)
        lhs = lhs.astype(_BF16)
        s_old = lax.dot_general(lhs, ck_ref[j].astype(_BF16), (((1,), (1,)), ((), ())),
                                preferred_element_type=_F32)
        yield 1
        s_new = jnp.sum(lhs.astype(_F32) * kn.astype(_BF16).astype(_F32), axis=-1, keepdims=True)
        m = jnp.maximum(jnp.maximum(jnp.max(s_old, axis=-1, keepdims=True), s_new), sink)
        e_old = jnp.exp(s_old - m)
        e_new = jnp.exp(s_new - m)
        inv = 1.0 / (jnp.sum(e_old, axis=-1, keepdims=True) + e_new + jnp.exp(sink - m))
        yield 1
        o16 = jnp.dot(e_old.astype(_BF16), cv_ref[j].astype(_BF16), preferred_element_type=_F32)
        o16 = (o16 + e_new.astype(_BF16).astype(_F32) * vn.astype(_BF16).astype(_F32)) * inv
        for i in range(GROUP):
            picked = jnp.where(own & (slot == i), o16, 0.0)
            os_buf[pl.ds(r, 1), i * KV_WIDTH:(i + 1) * KV_WIDTH] = jnp.sum(picked, axis=0, keepdims=True)
        yield 1

    def sample_decodes():
        for j in range(per_step):
            yield from sample_decode(j)

    def sample_tail():
        yield from sample_decodes()
        o = _regroup_heads(os_buf[...], _SLOT_SWAP)
        h = hs_ref[...] + (yield from _scaled(_dot_cols(o, w_o), sample_rows))
        h = yield from _scaled(_ffn_and_ple_steps(h, _ffn_norm(h, g_ffn), pss_ref[...], g_ple,
                                                  w_gate, w_up, w_down, ple_gate, ple_proj), sample_rows)
        ys_out[...] = (h * _rms_scale(h)) * g_final[...]

    def attention_half():
        h = h_ref[...]
        q, k, v = yield from _qkv_steps(h, cos_ref[...], slo_ref[...], shi_ref[...], g_kv, g_mix, w_kv, w_q)
        qb = q.astype(_BF16)
        kwin_out[...] = k[tm - WINDOW:, :]
        vwin_out[...] = v[tm - WINDOW:, :]

        lane = lax.broadcasted_iota(jnp.int32, (1, LANES), 1)
        low = lane < HEAD_DIM
        for arr, buf in ((k, k2buf), (v, v2buf)):
            for s in range(KV_WIDTH // LANES):
                slab = arr[:, s * LANES:(s + 1) * LANES]
                swapped = pltpu.roll(slab, HEAD_DIM, 1)
                buf[2 * s, BLOCK:BLOCK + tm, :] = jnp.where(low, slab, swapped).astype(_BF16)
                buf[2 * s + 1, BLOCK:BLOCK + tm, :] = jnp.where(low, swapped, slab).astype(_BF16)

        row = lax.broadcasted_iota(jnp.int32, (BLOCK, 2 * BLOCK), 0)
        col = lax.broadcasted_iota(jnp.int32, (BLOCK, 2 * BLOCK), 1)
        diff = row + BLOCK - col
        band = (diff >= 0) & (diff <= WINDOW)
        first_col = jnp.where(lax.rem(step, tiles_per_seq) == 0, BLOCK, 0)
        zero = jnp.zeros((), _BF16)
        ones = jnp.ones((2 * BLOCK, LANES), _BF16)
        units = [(jb, g) for jb in range(tm // BLOCK) for g in range(N_KV_HEADS)]

        def scores(jb, g):
            r0 = jb * BLOCK
            parts = []
            for s in (2 * g, 2 * g + 1):
                slab = qb[r0:r0 + BLOCK, s * LANES:(s + 1) * LANES]
                parts.append(jnp.where(low, slab, zero))
                parts.append(jnp.where(low, zero, slab))
            qs = jnp.concatenate(parts, axis=0)
            return lax.dot_general(qs, k2buf[g, r0:r0 + 2 * BLOCK, :],
                                   (((1,), (1,)), ((), ())), preferred_element_type=_F32)

        def weights(jb, g, sc):
            valid = band & (col >= first_col) if jb == 0 else band
            es, tail = [], []
            for i in range(GROUP):
                sink = sinks_ref[GROUP * g + i]
                sp = jnp.where(valid, sc[i * BLOCK:(i + 1) * BLOCK], NEG_INF)
                m = jnp.maximum(jnp.max(sp, axis=-1, keepdims=True), sink)
                es.append(jnp.exp(sp - m).astype(_BF16))
                tail.append(jnp.exp(sink - m))
            return jnp.concatenate(es, axis=0), tail

        def values(jb, g, e, tail):
            r0 = jb * BLOCK
            vv = jnp.concatenate([v2buf[g, r0:r0 + 2 * BLOCK, :], ones], axis=1)
            o3 = jnp.dot(e, vv, preferred_element_type=_F32)
            o2 = []
            for i in range(GROUP):
                piece = o3[i * BLOCK:(i + 1) * BLOCK]
                o2.append(piece[:, :LANES] * (1.0 / (piece[:, LANES:] + tail[i])))
            obuf[r0:r0 + BLOCK, (2 * g) * LANES:(2 * g + 1) * LANES] = (
                jnp.where(low, o2[0], o2[1]).astype(_BF16))
            obuf[r0:r0 + BLOCK, (2 * g + 1) * LANES:(2 * g + 2) * LANES] = (
                jnp.where(low, o2[2], o2[3]).astype(_BF16))

        sc, ew = {}, {}
        for n in range(len(units) + 2):
            if n < len(units):
                sc[n] = scores(*units[n])
                yield attn_cost
            if 0 <= n - 1 < len(units):
                ew[n - 1] = weights(*units[n - 1], sc.pop(n - 1))
                yield 0.0
            if 0 <= n - 2 < len(units):
                values(*units[n - 2], *ew.pop(n - 2))
                yield attn_cost
        hbuf[...] = h
        k2buf[:, 0:BLOCK, :] = k2buf[:, tm:tm + BLOCK, :]
        v2buf[:, 0:BLOCK, :] = v2buf[:, tm:tm + BLOCK, :]

    def body(first_half, second_half):
        lanes = []
        qkv_cost = _chunks_cost(w_kv) + _chunks_cost(w_q)
        if first_half and not second_half:
            zeros = jnp.zeros((N_KV_HEADS, BLOCK, LANES), _BF16)
            k2buf[:, 0:BLOCK, :] = zeros
            v2buf[:, 0:BLOCK, :] = zeros
        if second_half:
            dense = dense_half()
            next(dense)
            lanes.append((dense, dense_cost, LAYER1_WINDOWS[1]))
        if first_half:
            n_units = (tm // BLOCK) * N_KV_HEADS
            attention = attention_half()
            next(attention)
            lanes.append((attention, qkv_cost + 2 * n_units * attn_cost, LAYER1_WINDOWS[0]))
        if first_half and not second_half:
            lanes.append((_scaled(sample_qkv(), sample_rows), qkv_cost * sample_rows, (0.0, 1.0)))
        elif first_half:
            lanes.append((sample_decodes(), 3 * per_step, SAMPLE_DECODE_WINDOW))
        else:
            lanes.append((sample_tail(), 3 * per_step + dense_cost * sample_rows, (0.0, 1.0)))
        _interleave(*lanes)

    _pipeline_bodies(step, n_tiles, body)


class _Layer:
    def __init__(self, stacked, layer):
        self.stacked, self.layer = stacked, layer


def _operand(x):
    return x.stacked if isinstance(x, _Layer) else x


def _resident(x):
    if isinstance(x, _Layer):
        rest = x.stacked.shape[1:]
        index = (x.layer,) + (0,) * len(rest)
        return pl.BlockSpec((None,) + rest, lambda *_: index, pipeline_mode=pl.Buffered(1))
    nd = x.ndim
    return pl.BlockSpec(x.shape, lambda *_: (0,) * nd, pipeline_mode=pl.Buffered(1))


def _params(n_grid_dims):
    return pltpu.CompilerParams(dimension_semantics=("arbitrary",) * n_grid_dims,
                                vmem_limit_bytes=VMEM_LIMIT_BYTES)


def _rope_tables(pos):
    half = ROT_DIM // 2
    inv_freq = np.power(np.float64(ROPE_THETA), -np.arange(half, dtype=np.float64) / half)
    ang = pos.astype(np.float64)[:, None] * inv_freq[None, :]
    cos, sin = np.cos(ang), np.sin(ang)
    n = pos.shape[0]
    pad = np.zeros((n, HEAD_DIM - ROT_DIM))
    zero = np.zeros((n, half))
    cos_h = np.concatenate([cos, cos, pad + 1.0], axis=1)
    lo_h = np.concatenate([-sin, zero, pad], axis=1)
    hi_h = np.concatenate([zero, sin, pad], axis=1)
    rep = LANES // HEAD_DIM
    return tuple(jnp.asarray(np.tile(a, (1, rep)), dtype=_F32) for a in (cos_h, lo_h, hi_h))


def kernel(x_prompt, x_sample, state_conv, cache_k_win, cache_v_win, p_prompt, p_sample,
           norm_mix_g, norm_ffn_g, norm_ple_g, kv_norm_g, final_norm_g,
           conv_w_in, conv_w, conv_w_out, w_k, w_v, w_q, sinks, w_o,
           ffn_w_gate, ffn_w_up, ffn_w_down, ple_w_proj, ple_w_gate):
    batch, seq, d = x_prompt.shape
    dec = x_sample.shape[0]
    w_buf = cache_k_win.shape[1]
    tm = PROMPT_TILE
    assert d == D_MODEL and seq % tm == 0 and tm % BLOCK == 0 and tm >= WINDOW
    assert x_sample.shape[1] == 1 and w_buf == WINDOW and dec % (batch * (seq // tm)) == 0

    bf = lambda a: a.astype(_BF16)
    row = lambda a: a.reshape(1, -1)
    rows = lambda a: a.reshape(a.shape[0], 1, a.shape[1])
    stacked = dict(g_mix=rows(norm_mix_g), g_ffn=rows(norm_ffn_g), g_ple=rows(norm_ple_g),
                   w_gate=bf(ffn_w_gate), w_up=bf(ffn_w_up), w_down=bf(ffn_w_down),
                   ple_gate=bf(ple_w_gate), ple_proj=bf(ple_w_proj))
    l0 = {n: _Layer(a, 0) for n, a in stacked.items()}
    l0.update(w_in=_Layer(bf(conv_w_in), 0), conv_w=_Layer(conv_w, 0), w_out=_Layer(bf(conv_w_out), 0))
    l1 = {n: _Layer(a, 1) for n, a in stacked.items()}
    l1.update(g_kv=row(kv_norm_g), g_final=row(final_norm_g),
              w_kv=bf(jnp.concatenate([w_k, w_v], axis=1)), w_q=_Layer(bf(w_q), 0), w_o=_Layer(bf(w_o), 0))
    l0_names = ("g_mix", "g_ffn", "g_ple", "w_in", "conv_w", "w_out",
                "w_gate", "w_up", "w_down", "ple_gate", "ple_proj")
    l0_args = [l0[n] for n in l0_names]

    nt = seq // tm
    n_tiles = batch * nt
    cur = lambda s: jnp.minimum(s, n_tiles - 1)
    prev = lambda s: jnp.maximum(s - 1, 0)
    cur_tile = pl.BlockSpec((None, tm, d), lambda s: (cur(s) // nt, cur(s) % nt, 0))
    prev_tile = pl.BlockSpec((None, tm, d), lambda s: (prev(s) // nt, prev(s) % nt, 0))
    prev_ple = lambda layer: pl.BlockSpec((None, None, tm, p_prompt.shape[-1]),
                                          lambda s: (layer, prev(s) // nt, prev(s) % nt, 0))
    xs = x_sample.reshape(dec, d)
    ps = p_sample.reshape(p_sample.shape[0], dec, p_sample.shape[-1])
    sample_in = [xs, _Layer(ps, 0), state_conv[0, :, 0, :], state_conv[0, :, 1, :]]
    sample_rows = pl.BlockSpec((dec, d), lambda s: (0, 0))
    h1, conv_state_prompt, h1s, u_s = pl.pallas_call(
        functools.partial(_layer0_kernel, tiles_per_seq=nt, n_tiles=n_tiles),
        grid=(n_tiles + 1,),
        in_specs=[cur_tile, prev_ple(0)] + [_resident(a) for a in sample_in + l0_args],
        out_specs=[prev_tile, pl.BlockSpec((None, CONV_WIDTH - 1, d), lambda s: (cur(s) // nt, 0, 0)),
                   sample_rows, sample_rows],
        out_shape=[jax.ShapeDtypeStruct((batch, seq, d), _F32),
                   jax.ShapeDtypeStruct((batch, CONV_WIDTH - 1, d), _F32),
                   jax.ShapeDtypeStruct((dec, d), _F32),
                   jax.ShapeDtypeStruct((dec, d), _F32)],
        scratch_shapes=[pltpu.VMEM((tm + SUBLANES, d), _F32),
                        pltpu.VMEM((tm, d), _F32),
                        pltpu.VMEM((tm, d), _BF16)],
        compiler_params=_params(1),
        name="layer0",
    )(x_prompt, p_prompt, *map(_operand, sample_in + l0_args))
    conv_state_sample = jnp.stack([state_conv[0, :, 1, :], u_s], axis=1)[None]

    cos_p, lo_p, hi_p = _rope_tables(np.arange(seq))
    cos_s, lo_s, hi_s = _rope_tables(np.full((dec,), PAST_LEN))
    sink_rows = sinks[0].reshape(N_KV_HEADS, GROUP).T.reshape(N_HEADS, 1)
    ck = cache_k_win.reshape(dec, w_buf, KV_WIDTH)
    cv = cache_v_win.reshape(dec, w_buf, KV_WIDTH)
    per_step = dec // n_tiles
    cache = pl.BlockSpec((per_step, w_buf, KV_WIDTH), lambda s: (prev(s), 0, 0))
    sample_l1 = [h1s, _Layer(ps, 1), cos_s, lo_s, hi_s, sink_rows]
    l1_names = ("g_kv", "g_mix", "g_ffn", "g_ple", "g_final", "w_kv", "w_q", "w_o",
                "w_gate", "w_up", "w_down", "ple_gate", "ple_proj")
    l1_args = [l1[n] for n in l1_names]
    y_prompt, k_win_prompt, v_win_prompt, y_sample, k_win_sample, v_win_sample = pl.pallas_call(
        functools.partial(_layer1_kernel, tiles_per_seq=nt, n_tiles=n_tiles),
        grid=(n_tiles + 1,),
        in_specs=[cur_tile, prev_ple(1)]
                 + [pl.BlockSpec((tm, LANES), lambda s: (cur(s) % nt, 0))] * 3
                 + [pl.BlockSpec(memory_space=pltpu.SMEM)]
                 + [_resident(a) for a in sample_l1] + [cache, cache]
                 + [_resident(a) for a in l1_args],
        out_specs=[prev_tile,
                   pl.BlockSpec((None, WINDOW, KV_WIDTH), lambda s: (cur(s) // nt, 0, 0)),
                   pl.BlockSpec((None, WINDOW, KV_WIDTH), lambda s: (cur(s) // nt, 0, 0)),
                   sample_rows, cache, cache],
        out_shape=[jax.ShapeDtypeStruct((batch, seq, d), _F32),
                   jax.ShapeDtypeStruct((batch, WINDOW, KV_WIDTH), _F32),
                   jax.ShapeDtypeStruct((batch, WINDOW, KV_WIDTH), _F32),
                   jax.ShapeDtypeStruct((dec, d), _F32),
                   jax.ShapeDtypeStruct((dec, w_buf, KV_WIDTH), _F32),
                   jax.ShapeDtypeStruct((dec, w_buf, KV_WIDTH), _F32)],
        scratch_shapes=[pltpu.VMEM((N_KV_HEADS, BLOCK + tm, LANES), _BF16),
                        pltpu.VMEM((N_KV_HEADS, BLOCK + tm, LANES), _BF16),
                        pltpu.VMEM((tm, d), _F32),
                        pltpu.VMEM((tm, d), _BF16),
                        pltpu.VMEM((dec, d), _F32),
                        pltpu.VMEM((dec, KV_WIDTH), _F32),
                        pltpu.VMEM((dec, KV_WIDTH), _F32),
                        pltpu.VMEM((dec, d), _F32)],
        compiler_params=_params(1),
        name="layer1",
    )(h1, p_prompt, cos_p, lo_p, hi_p, sinks[0], *map(_operand, sample_l1), ck, cv,
      *map(_operand, l1_args))

    kv_shape = (N_KV_HEADS, HEAD_DIM)
    return (y_prompt, y_sample.reshape(dec, 1, d),
            conv_state_prompt[None], conv_state_sample,
            k_win_prompt.reshape(batch, w_buf, *kv_shape), v_win_prompt.reshape(batch, w_buf, *kv_shape),
            k_win_sample.reshape(dec, w_buf, *kv_shape), v_win_sample.reshape(dec, w_buf, *kv_shape))
```

```python
import functools

import jax
import jax.numpy as jnp
import numpy as np
from jax import lax
from jax.experimental import pallas as pl
from jax.experimental.pallas import tpu as pltpu

D_MODEL = 1024
HEAD_DIM = 64
N_HEADS = 16
N_KV_HEADS = 4
GROUP = N_HEADS // N_KV_HEADS
KV_WIDTH = N_KV_HEADS * HEAD_DIM
ROT_DIM = HEAD_DIM // 4
ROPE_THETA = 500000.0
WINDOW = 128
BLOCK = 128
CONV_WIDTH = 3
PAST_LEN = 16384
RMS_EPS = 1e-6
NEG_INF = -1e30

LANES = 128
SUBLANES = 8
MXU_COLS = 256
CHUNK_COLS = 2 * MXU_COLS
VMEM_LIMIT_BYTES = 60000 * 1024

PROMPT_TILE = 256
SAMPLE_DECODE_WINDOW = (0.05, 0.85)
LAYER0_WINDOWS = ((0.10, 0.92), (0.0, 1.0))
LAYER1_WINDOWS = ((0.04, 0.92), (0.0, 1.0))

_BF16 = jnp.bfloat16
_F32 = jnp.float32


def _rms_scale(x):
    return lax.rsqrt(jnp.mean(x * x, axis=-1, keepdims=True) + RMS_EPS)


def _sigmoid(x):
    return 1.0 / (1.0 + jnp.exp(-x))


def _rope(x, cos, sin_lo, sin_hi):
    half = ROT_DIM // 2
    out = []
    for c in range(x.shape[1] // LANES):
        slab = x[:, c * LANES:(c + 1) * LANES]
        out.append(slab * cos
                   + pltpu.roll(slab, LANES - half, 1) * sin_lo
                   + pltpu.roll(slab, half, 1) * sin_hi)
    return jnp.concatenate(out, axis=1)


def _interleave(*lanes):
    done = [0.0] * len(lanes)
    live = list(range(len(lanes)))

    def position(j):
        _, total, (start, end) = lanes[j]
        return start + (end - start) * done[j] / total

    while live:
        i = min(live, key=position)
        try:
            done[i] += next(lanes[i][0])
        except StopIteration:
            live.remove(i)


def _scaled(steps, factor):
    try:
        while True:
            yield next(steps) * factor
    except StopIteration as done:
        return done.value


def _chunks(n_cols):
    return [(c, min(CHUNK_COLS, n_cols - c)) for c in range(0, n_cols, CHUNK_COLS)]


def _col_dot(a, w_ref, c0, width):
    return jnp.dot(a, w_ref[:, c0:c0 + width], preferred_element_type=_F32)


def _dot_cost(w_ref, width):
    return -(-w_ref.shape[0] // MXU_COLS) * (width // MXU_COLS)


def _chunks_cost(w_ref):
    return _dot_cost(w_ref, w_ref.shape[1])


def _dot_cols(a, w_ref):
    a = a.astype(_BF16)
    cols = []
    for c, width in _chunks(w_ref.shape[1]):
        cols.append(_col_dot(a, w_ref, c, width))
        yield _dot_cost(w_ref, width)
    return jnp.concatenate(cols, axis=1)


def _ffn_norm(h, g_ffn):
    return ((h * _rms_scale(h)) * g_ffn[...]).astype(_BF16)


def _ffn_and_ple_steps(h, hf, p, g_ple, w_gate, w_up, w_down, ple_gate, ple_proj):
    acts = []
    for c, width in _chunks(w_gate.shape[1]):
        g = _col_dot(hf, w_gate, c, width)
        yield _dot_cost(w_gate, width)
        u = _col_dot(hf, w_up, c, width)
        yield _dot_cost(w_up, width)
        acts.append(((g * _sigmoid(g)) * u).astype(_BF16))
    h = h + (yield from _dot_cols(jnp.concatenate(acts, axis=1), w_down))
    hp = (h * _rms_scale(h)) * g_ple[...]
    gate = _sigmoid((yield from _dot_cols(hp, ple_gate)))
    return h + gate * (yield from _dot_cols(p, ple_proj))


def _ffn_and_ple_cost(w_gate, w_down, ple_gate, ple_proj):
    return 2 * _chunks_cost(w_gate) + _chunks_cost(w_down) + _chunks_cost(ple_gate) + _chunks_cost(ple_proj)


def _conv_mixer_steps(x, g_mix, w_in, conv_w, w_out, history):
    hn = ((x * _rms_scale(x)) * g_mix[...]).astype(_BF16)
    yield 0
    us, convs = [], []
    for c, width in _chunks(D_MODEL):
        cols = slice(c, c + width)
        c_gate = _col_dot(hn, w_in, D_MODEL + c, width)
        yield _dot_cost(w_in, width)
        xin = _col_dot(hn, w_in, 2 * D_MODEL + c, width)
        yield _dot_cost(w_in, width)
        u = c_gate * xin
        back2, back1 = history(cols, u)
        conv = conv_w[0:1, cols] * back2
        conv = conv + conv_w[1:2, cols] * back1
        conv = conv + conv_w[2:3, cols] * u
        us.append(u)
        convs.append(conv)
    gated = []
    for j, (c, width) in enumerate(_chunks(D_MODEL)):
        b_gate = _col_dot(hn, w_in, c, width)
        yield _dot_cost(w_in, width)
        gated.append((b_gate * convs[j]).astype(_BF16))
    y = yield from _dot_cols(jnp.concatenate(gated, axis=1), w_out)
    return x + y, jnp.concatenate(us, axis=1)


def _conv_mixer_cost(w_in, w_out):
    return _chunks_cost(w_in) + _chunks_cost(w_out)


def _pipeline_bodies(step, n_tiles, body):
    pl.when(step == 0)(lambda: body(True, False))
    pl.when((step > 0) & (step < n_tiles))(lambda: body(True, True))
    pl.when(step == n_tiles)(lambda: body(False, True))


def _layer0_kernel(x_ref, p_ref, xs_ref, ps_ref, s0_ref, s1_ref,
                   g_mix, g_ffn, g_ple, w_in, conv_w, w_out,
                   w_gate, w_up, w_down, ple_gate, ple_proj,
                   h_out, state_out, hs_out, us_out, ubuf, hbuf, hfbuf, *, tiles_per_seq, n_tiles):
    tm = x_ref.shape[0]
    step = pl.program_id(0)
    ffn_cost = _ffn_and_ple_cost(w_gate, w_down, ple_gate, ple_proj)
    conv_cost = _conv_mixer_cost(w_in, w_out)

    def history(cols, u):
        ubuf[SUBLANES:SUBLANES + tm, cols] = u
        return (ubuf[SUBLANES - 2:SUBLANES - 2 + tm, cols], ubuf[SUBLANES - 1:SUBLANES - 1 + tm, cols])

    def ffn_half():
        out = yield from _ffn_and_ple_steps(hbuf[...], hfbuf[...], p_ref[...], g_ple,
                                            w_gate, w_up, w_down, ple_gate, ple_proj)
        h_out[...] = out

    def conv_half():
        h, _ = yield from _conv_mixer_steps(x_ref[...], g_mix, w_in, conv_w, w_out, history)
        state_out[...] = ubuf[SUBLANES + tm - (CONV_WIDTH - 1):SUBLANES + tm, :]
        ubuf[0:SUBLANES, :] = ubuf[tm:tm + SUBLANES, :]
        hbuf[...] = h
        hfbuf[...] = _ffn_norm(h, g_ffn)

    def sample_layer():
        history = lambda cols, u: (s0_ref[:, cols], s1_ref[:, cols])
        h, u = yield from _conv_mixer_steps(xs_ref[...], g_mix, w_in, conv_w, w_out, history)
        out = yield from _ffn_and_ple_steps(h, _ffn_norm(h, g_ffn), ps_ref[...], g_ple,
                                            w_gate, w_up, w_down, ple_gate, ple_proj)
        hs_out[...] = out
        us_out[...] = u

    def body(first_half, second_half):
        lanes = []
        if first_half:
            zeros = jnp.zeros((SUBLANES, D_MODEL), _F32)
            fresh = lax.rem(step, tiles_per_seq) == 0
            ubuf[0:SUBLANES, :] = jnp.where(fresh, zeros, ubuf[0:SUBLANES, :]) if second_half else zeros
        if second_half:
            ffn = ffn_half()
            next(ffn)
            lanes.append((ffn, ffn_cost, LAYER0_WINDOWS[1]))
        if first_half:
            conv = conv_half()
            next(conv)
            lanes.append((conv, conv_cost, LAYER0_WINDOWS[0]))
        else:
            rows = xs_ref.shape[0] / tm
            lanes.append((_scaled(sample_layer(), rows), (conv_cost + ffn_cost) * rows, (0.0, 1.0)))
        _interleave(*lanes)

    _pipeline_bodies(step, n_tiles, body)


def _qkv_steps(h, cos, sin_lo, sin_hi, g_kv, g_mix, w_kv, w_q):
    hs = h * _rms_scale(h)
    h_kv = (hs * g_kv[...]).astype(_BF16)
    h_q = (hs * g_mix[...]).astype(_BF16)
    yield 0
    kv = yield from _dot_cols(h_kv, w_kv)
    q = yield from _dot_cols(h_q, w_q)
    k = _rope(kv[:, :KV_WIDTH], cos, sin_lo, sin_hi)
    v = kv[:, KV_WIDTH:]
    q = _rope(q, cos, sin_lo, sin_hi) * (HEAD_DIM ** -0.5)
    return q, k, v


def _regroup_heads(x, order):
    low = lax.broadcasted_iota(jnp.int32, (1, LANES), 1) < HEAD_DIM
    out = []
    for s in range(x.shape[1] // LANES):
        halves = []
        for half in range(2):
            src_head = order[2 * s + half]
            slab = x[:, (src_head // 2) * LANES:(src_head // 2 + 1) * LANES]
            halves.append(slab if src_head % 2 == half else pltpu.roll(slab, HEAD_DIM, 1))
        out.append(jnp.where(low, halves[0], halves[1]))
    return jnp.concatenate(out, axis=1)


_SLOT_SWAP = [GROUP * (p % N_KV_HEADS) + p // N_KV_HEADS for p in range(N_HEADS)]


def _layer1_kernel(h_ref, p_ref, cos_ref, slo_ref, shi_ref, sinks_ref,
                   hs_ref, pss_ref, cos_s, slo_s, shi_s, sink_rows, ck_ref, cv_ref,
                   g_kv, g_mix, g_ffn, g_ple, g_final,
                   w_kv, w_q, w_o, w_gate, w_up, w_down, ple_gate, ple_proj,
                   y_out, kwin_out, vwin_out, ys_out, kwins_out, vwins_out,
                   k2buf, v2buf, hbuf, obuf, qs_buf, ks_buf, vs_buf, kst_buf, vst_buf, os_buf,
                   *, tiles_per_seq, n_tiles):
    tm = h_ref.shape[0]
    step = pl.program_id(0)
    attn_cost = GROUP * BLOCK / tm
    dense_cost = _chunks_cost(w_o) + _ffn_and_ple_cost(w_gate, w_down, ple_gate, ple_proj)
    per_step = ck_ref.shape[0]
    sample_rows = hs_ref.shape[0] / tm

    def dense_half():
        hp = hbuf[...] + (yield from _dot_cols(obuf[...], w_o))
        hp = yield from _ffn_and_ple_steps(hp, _ffn_norm(hp, g_ffn), p_ref[...], g_ple,
                                           w_gate, w_up, w_down, ple_gate, ple_proj)
        y_out[...] = (hp * _rms_scale(hp)) * g_final[...]

    def sample_qkv():
        q, k, v = yield from _qkv_steps(hs_ref[...], cos_s[...], slo_s[...], shi_s[...],
                                        g_kv, g_mix, w_kv, w_q)
        qs_buf[...] = _regroup_heads(q, _SLOT_SWAP)
        ks_buf[...] = k
        vs_buf[...] = v
        kst_buf[...] = k.T
        vst_buf[...] = v.T

    def sample_decode(j):
        r = (step - 1) * per_step + j
        w = ck_ref.shape[2]
        key = lax.broadcasted_iota(jnp.int32, (KV_WIDTH, w), 1)
        seq_lane = lax.broadcasted_iota(jnp.int32, kst_buf.shape, 1)
        lane = lax.broadcasted_iota(jnp.int32, (N_HEADS, KV_WIDTH), 1)
        r16 = lax.broadcasted_iota(jnp.int32, (N_HEADS, KV_WIDTH), 0)
        slot = r16 // N_KV_HEADS
        kvh = r16 - slot * N_KV_HEADS
        own = (lane >= kvh * HEAD_DIM) & (lane < (kvh + 1) * HEAD_DIM)
        sink = sink_rows[...]
        kn = ks_buf[pl.ds(r, 1), :]
        vn = vs_buf[pl.ds(r, 1), :]
        for win_ref, new_t, out_ref in ((ck_ref, kst_buf, kwins_out), (cv_ref, vst_buf, vwins_out)):
            column = jnp.sum(jnp.where(seq_lane == r, new_t[...], 0.0), axis=1, keepdims=True)
            out_ref[j] = jnp.where(key == w - 1, column, pltpu.roll(win_ref[j], w - 1, 1))
        lhs = jnp.zeros((N_HEADS, KV_WIDTH), _F32)
        for i in range(GROUP):
            q_i = jnp.broadcast_to(qs_buf[pl.ds(r, 1), i * KV_WIDTH:(i + 1) * KV_WIDTH], (N_HEADS, KV_WIDTH))
            lhs = jnp.where(own & (slot == i), q_i, lhs)
        lhs = lhs.astype(_BF16)
        s_old = jnp.dot(lhs, ck_ref[j].astype(_BF16), preferred_element_type=_F32)
        yield 1
        s_new = jnp.sum(lhs.astype(_F32) * kn.astype(_BF16).astype(_F32), axis=-1, keepdims=True)
        m = jnp.maximum(jnp.maximum(jnp.max(s_old, axis=-1, keepdims=True), s_new), sink)
        e_old = jnp.exp(s_old - m)
        e_new = jnp.exp(s_new - m)
        inv = 1.0 / (jnp.sum(e_old, axis=-1, keepdims=True) + e_new + jnp.exp(sink - m))
        yield 1
        o16 = lax.dot_general(e_old.astype(_BF16), cv_ref[j].astype(_BF16), (((1,), (1,)), ((), ())),
                              preferred_element_type=_F32)
        o16 = (o16 + e_new.astype(_BF16).astype(_F32) * vn.astype(_BF16).astype(_F32)) * inv
        for i in range(GROUP):
            picked = jnp.where(own & (slot == i), o16, 0.0)
            os_buf[pl.ds(r, 1), i * KV_WIDTH:(i + 1) * KV_WIDTH] = jnp.sum(picked, axis=0, keepdims=True)
        yield 1

    def sample_decodes():
        for j in range(per_step):
            yield from sample_decode(j)

    def sample_tail():
        yield from sample_decodes()
        o = _regroup_heads(os_buf[...], _SLOT_SWAP)
        h = hs_ref[...] + (yield from _scaled(_dot_cols(o, w_o), sample_rows))
        h = yield from _scaled(_ffn_and_ple_steps(h, _ffn_norm(h, g_ffn), pss_ref[...], g_ple,
                                                  w_gate, w_up, w_down, ple_gate, ple_proj), sample_rows)
        ys_out[...] = (h * _rms_scale(h)) * g_final[...]

    def attention_half():
        h = h_ref[...]
        q, k, v = yield from _qkv_steps(h, cos_ref[...], slo_ref[...], shi_ref[...], g_kv, g_mix, w_kv, w_q)
        qb = q.astype(_BF16)
        kwin_out[...] = k[tm - WINDOW:, :]
        vwin_out[...] = v[tm - WINDOW:, :]

        lane = lax.broadcasted_iota(jnp.int32, (1, LANES), 1)
        low = lane < HEAD_DIM
        for arr, buf in ((k, k2buf), (v, v2buf)):
            for s in range(KV_WIDTH // LANES):
                slab = arr[:, s * LANES:(s + 1) * LANES]
                swapped = pltpu.roll(slab, HEAD_DIM, 1)
                buf[2 * s, BLOCK:BLOCK + tm, :] = jnp.where(low, slab, swapped).astype(_BF16)
                buf[2 * s + 1, BLOCK:BLOCK + tm, :] = jnp.where(low, swapped, slab).astype(_BF16)

        row = lax.broadcasted_iota(jnp.int32, (BLOCK, 2 * BLOCK), 0)
        col = lax.broadcasted_iota(jnp.int32, (BLOCK, 2 * BLOCK), 1)
        diff = row + BLOCK - col
        band = (diff >= 0) & (diff <= WINDOW)
        first_col = jnp.where(lax.rem(step, tiles_per_seq) == 0, BLOCK, 0)
        zero = jnp.zeros((), _BF16)
        ones = jnp.ones((2 * BLOCK, LANES), _BF16)
        units = [(jb, g) for jb in range(tm // BLOCK) for g in range(N_KV_HEADS)]

        def scores(jb, g):
            r0 = jb * BLOCK
            parts = []
            for s in (2 * g, 2 * g + 1):
                slab = qb[r0:r0 + BLOCK, s * LANES:(s + 1) * LANES]
                parts.append(jnp.where(low, slab, zero))
                parts.append(jnp.where(low, zero, slab))
            qs = jnp.concatenate(parts, axis=0)
            return lax.dot_general(qs, k2buf[g, r0:r0 + 2 * BLOCK, :],
                                   (((1,), (1,)), ((), ())), preferred_element_type=_F32)

        def weights(jb, g, sc):
            valid = band & (col >= first_col) if jb == 0 else band
            es, tail = [], []
            for i in range(GROUP):
                sink = sinks_ref[GROUP * g + i]
                sp = jnp.where(valid, sc[i * BLOCK:(i + 1) * BLOCK], NEG_INF)
                m = jnp.maximum(jnp.max(sp, axis=-1, keepdims=True), sink)
                es.append(jnp.exp(sp - m).astype(_BF16))
                tail.append(jnp.exp(sink - m))
            return jnp.concatenate(es, axis=0), tail

        def values(jb, g, e, tail):
            r0 = jb * BLOCK
            vv = jnp.concatenate([v2buf[g, r0:r0 + 2 * BLOCK, :], ones], axis=1)
            o3 = jnp.dot(e, vv, preferred_element_type=_F32)
            o2 = []
            for i in range(GROUP):
                piece = o3[i * BLOCK:(i + 1) * BLOCK]
                o2.append(piece[:, :LANES] * (1.0 / (piece[:, LANES:] + tail[i])))
            obuf[r0:r0 + BLOCK, (2 * g) * LANES:(2 * g + 1) * LANES] = (
                jnp.where(low, o2[0], o2[1]).astype(_BF16))
            obuf[r0:r0 + BLOCK, (2 * g + 1) * LANES:(2 * g + 2) * LANES] = (
                jnp.where(low, o2[2], o2[3]).astype(_BF16))

        sc, ew = {}, {}
        for n in range(len(units) + 2):
            if n < len(units):
                sc[n] = scores(*units[n])
                yield attn_cost
            if 0 <= n - 1 < len(units):
                ew[n - 1] = weights(*units[n - 1], sc.pop(n - 1))
                yield 0.0
            if 0 <= n - 2 < len(units):
                values(*units[n - 2], *ew.pop(n - 2))
                yield attn_cost
        hbuf[...] = h
        k2buf[:, 0:BLOCK, :] = k2buf[:, tm:tm + BLOCK, :]
        v2buf[:, 0:BLOCK, :] = v2buf[:, tm:tm + BLOCK, :]

    def body(first_half, second_half):
        lanes = []
        qkv_cost = _chunks_cost(w_kv) + _chunks_cost(w_q)
        if first_half and not second_half:
            zeros = jnp.zeros((N_KV_HEADS, BLOCK, LANES), _BF16)
            k2buf[:, 0:BLOCK, :] = zeros
            v2buf[:, 0:BLOCK, :] = zeros
        if second_half:
            dense = dense_half()
            next(dense)
            lanes.append((dense, dense_cost, LAYER1_WINDOWS[1]))
        if first_half:
            n_units = (tm // BLOCK) * N_KV_HEADS
            attention = attention_half()
            next(attention)
            lanes.append((attention, qkv_cost + 2 * n_units * attn_cost, LAYER1_WINDOWS[0]))
        if first_half and not second_half:
            lanes.append((_scaled(sample_qkv(), sample_rows), qkv_cost * sample_rows, (0.0, 1.0)))
        elif first_half:
            lanes.append((sample_decodes(), 3 * per_step, SAMPLE_DECODE_WINDOW))
        else:
            lanes.append((sample_tail(), 3 * per_step + dense_cost * sample_rows, (0.0, 1.0)))
        _interleave(*lanes)

    _pipeline_bodies(step, n_tiles, body)


class _Layer:
    def __init__(self, stacked, layer):
        self.stacked, self.layer = stacked, layer


def _operand(x):
    return x.stacked if isinstance(x, _Layer) else x


def _resident(x):
    if isinstance(x, _Layer):
        rest = x.stacked.shape[1:]
        index = (x.layer,) + (0,) * len(rest)
        return pl.BlockSpec((None,) + rest, lambda *_: index, pipeline_mode=pl.Buffered(1))
    nd = x.ndim
    return pl.BlockSpec(x.shape, lambda *_: (0,) * nd, pipeline_mode=pl.Buffered(1))


def _params(n_grid_dims):
    return pltpu.CompilerParams(dimension_semantics=("arbitrary",) * n_grid_dims,
                                vmem_limit_bytes=VMEM_LIMIT_BYTES)


def _rope_tables(pos):
    half = ROT_DIM // 2
    inv_freq = np.power(np.float64(ROPE_THETA), -np.arange(half, dtype=np.float64) / half)
    ang = pos.astype(np.float64)[:, None] * inv_freq[None, :]
    cos, sin = np.cos(ang), np.sin(ang)
    n = pos.shape[0]
    pad = np.zeros((n, HEAD_DIM - ROT_DIM))
    zero = np.zeros((n, half))
    cos_h = np.concatenate([cos, cos, pad + 1.0], axis=1)
    lo_h = np.concatenate([-sin, zero, pad], axis=1)
    hi_h = np.concatenate([zero, sin, pad], axis=1)
    rep = LANES // HEAD_DIM
    return tuple(jnp.asarray(np.tile(a, (1, rep)), dtype=_F32) for a in (cos_h, lo_h, hi_h))


def kernel(x_prompt, x_sample, state_conv, cache_k_win, cache_v_win, p_prompt, p_sample,
           norm_mix_g, norm_ffn_g, norm_ple_g, kv_norm_g, final_norm_g,
           conv_w_in, conv_w, conv_w_out, w_k, w_v, w_q, sinks, w_o,
           ffn_w_gate, ffn_w_up, ffn_w_down, ple_w_proj, ple_w_gate):
    batch, seq, d = x_prompt.shape
    dec = x_sample.shape[0]
    w_buf = cache_k_win.shape[1]
    tm = PROMPT_TILE
    assert d == D_MODEL and seq % tm == 0 and tm % BLOCK == 0 and tm >= WINDOW
    assert x_sample.shape[1] == 1 and w_buf == WINDOW and dec % (batch * (seq // tm)) == 0

    bf = lambda a: a.astype(_BF16)
    row = lambda a: a.reshape(1, -1)
    rows = lambda a: a.reshape(a.shape[0], 1, a.shape[1])
    stacked = dict(g_mix=rows(norm_mix_g), g_ffn=rows(norm_ffn_g), g_ple=rows(norm_ple_g),
                   w_gate=bf(ffn_w_gate), w_up=bf(ffn_w_up), w_down=bf(ffn_w_down),
                   ple_gate=bf(ple_w_gate), ple_proj=bf(ple_w_proj))
    l0 = {n: _Layer(a, 0) for n, a in stacked.items()}
    l0.update(w_in=_Layer(bf(conv_w_in), 0), conv_w=_Layer(conv_w, 0), w_out=_Layer(bf(conv_w_out), 0))
    l1 = {n: _Layer(a, 1) for n, a in stacked.items()}
    l1.update(g_kv=row(kv_norm_g), g_final=row(final_norm_g),
              w_kv=bf(jnp.concatenate([w_k, w_v], axis=1)), w_q=_Layer(bf(w_q), 0), w_o=_Layer(bf(w_o), 0))
    l0_names = ("g_mix", "g_ffn", "g_ple", "w_in", "conv_w", "w_out",
                "w_gate", "w_up", "w_down", "ple_gate", "ple_proj")
    l0_args = [l0[n] for n in l0_names]

    nt = seq // tm
    n_tiles = batch * nt
    cur = lambda s: jnp.minimum(s, n_tiles - 1)
    prev = lambda s: jnp.maximum(s - 1, 0)
    cur_tile = pl.BlockSpec((None, tm, d), lambda s: (cur(s) // nt, cur(s) % nt, 0))
    prev_tile = pl.BlockSpec((None, tm, d), lambda s: (prev(s) // nt, prev(s) % nt, 0))
    prev_ple = lambda layer: pl.BlockSpec((None, None, tm, p_prompt.shape[-1]),
                                          lambda s: (layer, prev(s) // nt, prev(s) % nt, 0))
    xs = x_sample.reshape(dec, d)
    ps = p_sample.reshape(p_sample.shape[0], dec, p_sample.shape[-1])
    sample_in = [xs, _Layer(ps, 0), state_conv[0, :, 0, :], state_conv[0, :, 1, :]]
    sample_rows = pl.BlockSpec((dec, d), lambda s: (0, 0))
    h1, conv_state_prompt, h1s, u_s = pl.pallas_call(
        functools.partial(_layer0_kernel, tiles_per_seq=nt, n_tiles=n_tiles),
        grid=(n_tiles + 1,),
        in_specs=[cur_tile, prev_ple(0)] + [_resident(a) for a in sample_in + l0_args],
        out_specs=[prev_tile, pl.BlockSpec((None, CONV_WIDTH - 1, d), lambda s: (cur(s) // nt, 0, 0)),
                   sample_rows, sample_rows],
        out_shape=[jax.ShapeDtypeStruct((batch, seq, d), _F32),
                   jax.ShapeDtypeStruct((batch, CONV_WIDTH - 1, d), _F32),
                   jax.ShapeDtypeStruct((dec, d), _F32),
                   jax.ShapeDtypeStruct((dec, d), _F32)],
        scratch_shapes=[pltpu.VMEM((tm + SUBLANES, d), _F32),
                        pltpu.VMEM((tm, d), _F32),
                        pltpu.VMEM((tm, d), _BF16)],
        compiler_params=_params(1),
        name="layer0",
    )(x_prompt, p_prompt, *map(_operand, sample_in + l0_args))
    conv_state_sample = jnp.stack([state_conv[0, :, 1, :], u_s], axis=1)[None]

    cos_p, lo_p, hi_p = _rope_tables(np.arange(seq))
    cos_s, lo_s, hi_s = _rope_tables(np.full((dec,), PAST_LEN))
    sink_rows = sinks[0].reshape(N_KV_HEADS, GROUP).T.reshape(N_HEADS, 1)
    windows_t = lambda a: jnp.transpose(a, (0, 2, 3, 1)).reshape(a.shape[0], KV_WIDTH, a.shape[1])
    windows = lambda a: jnp.transpose(a.reshape(a.shape[0], N_KV_HEADS, HEAD_DIM, a.shape[2]), (0, 3, 1, 2))
    ck, cv = windows_t(cache_k_win), windows_t(cache_v_win)
    per_step = dec // n_tiles
    cache = pl.BlockSpec((per_step, KV_WIDTH, w_buf), lambda s: (prev(s), 0, 0))
    sample_l1 = [h1s, _Layer(ps, 1), cos_s, lo_s, hi_s, sink_rows]
    l1_names = ("g_kv", "g_mix", "g_ffn", "g_ple", "g_final", "w_kv", "w_q", "w_o",
                "w_gate", "w_up", "w_down", "ple_gate", "ple_proj")
    l1_args = [l1[n] for n in l1_names]
    y_prompt, k_win_prompt, v_win_prompt, y_sample, k_win_sample, v_win_sample = pl.pallas_call(
        functools.partial(_layer1_kernel, tiles_per_seq=nt, n_tiles=n_tiles),
        grid=(n_tiles + 1,),
        in_specs=[cur_tile, prev_ple(1)]
                 + [pl.BlockSpec((tm, LANES), lambda s: (cur(s) % nt, 0))] * 3
                 + [pl.BlockSpec(memory_space=pltpu.SMEM)]
                 + [_resident(a) for a in sample_l1] + [cache, cache]
                 + [_resident(a) for a in l1_args],
        out_specs=[prev_tile,
                   pl.BlockSpec((None, WINDOW, KV_WIDTH), lambda s: (cur(s) // nt, 0, 0)),
                   pl.BlockSpec((None, WINDOW, KV_WIDTH), lambda s: (cur(s) // nt, 0, 0)),
                   sample_rows, cache, cache],
        out_shape=[jax.ShapeDtypeStruct((batch, seq, d), _F32),
                   jax.ShapeDtypeStruct((batch, WINDOW, KV_WIDTH), _F32),
                   jax.ShapeDtypeStruct((batch, WINDOW, KV_WIDTH), _F32),
                   jax.ShapeDtypeStruct((dec, d), _F32),
                   jax.ShapeDtypeStruct((dec, KV_WIDTH, w_buf), _F32),
                   jax.ShapeDtypeStruct((dec, KV_WIDTH, w_buf), _F32)],
        scratch_shapes=[pltpu.VMEM((N_KV_HEADS, BLOCK + tm, LANES), _BF16),
                        pltpu.VMEM((N_KV_HEADS, BLOCK + tm, LANES), _BF16),
                        pltpu.VMEM((tm, d), _F32),
                        pltpu.VMEM((tm, d), _BF16),
                        pltpu.VMEM((dec, d), _F32),
                        pltpu.VMEM((dec, KV_WIDTH), _F32),
                        pltpu.VMEM((dec, KV_WIDTH), _F32),
                        pltpu.VMEM((KV_WIDTH, dec), _F32),
                        pltpu.VMEM((KV_WIDTH, dec), _F32),
                        pltpu.VMEM((dec, d), _F32)],
        compiler_params=_params(1),
        name="layer1",
    )(h1, p_prompt, cos_p, lo_p, hi_p, sinks[0], *map(_operand, sample_l1), ck, cv,
      *map(_operand, l1_args))

    kv_shape = (N_KV_HEADS, HEAD_DIM)
    return (y_prompt, y_sample.reshape(dec, 1, d),
            conv_state_prompt[None], conv_state_sample,
            k_win_prompt.reshape(batch, w_buf, *kv_shape), v_win_prompt.reshape(batch, w_buf, *kv_shape),
            windows(k_win_sample), windows(v_win_sample))
```

```python
import functools

import jax
import jax.numpy as jnp
import numpy as np
from jax import lax
from jax.experimental import pallas as pl
from jax.experimental.pallas import tpu as pltpu

D_MODEL = 1024
HEAD_DIM = 64
N_HEADS = 16
N_KV_HEADS = 4
GROUP = N_HEADS // N_KV_HEADS
KV_WIDTH = N_KV_HEADS * HEAD_DIM
ROT_DIM = HEAD_DIM // 4
ROPE_THETA = 500000.0
WINDOW = 128
BLOCK = 128
CONV_WIDTH = 3
PAST_LEN = 16384
RMS_EPS = 1e-6
NEG_INF = -1e30

LANES = 128
SUBLANES = 8
MXU_COLS = 256
CHUNK_COLS = 2 * MXU_COLS
VMEM_LIMIT_BYTES = 60000 * 1024

PROMPT_TILE = 256
STAGING_ROWS = 256
SAMPLE_DECODE_WINDOW = (0.05, 0.85)
LAYER0_WINDOWS = ((0.10, 0.92), (0.0, 1.0))
LAYER1_WINDOWS = ((0.04, 0.92), (0.0, 1.0))

_BF16 = jnp.bfloat16
_F32 = jnp.float32


def _rms_scale(x):
    return lax.rsqrt(jnp.mean(x * x, axis=-1, keepdims=True) + RMS_EPS)


def _sigmoid(x):
    return 1.0 / (1.0 + jnp.exp(-x))


def _rope(x, cos, sin_lo, sin_hi):
    half = ROT_DIM // 2
    out = []
    for c in range(x.shape[1] // LANES):
        slab = x[:, c * LANES:(c + 1) * LANES]
        out.append(slab * cos
                   + pltpu.roll(slab, LANES - half, 1) * sin_lo
                   + pltpu.roll(slab, half, 1) * sin_hi)
    return jnp.concatenate(out, axis=1)


def _interleave(*lanes):
    done = [0.0] * len(lanes)
    live = list(range(len(lanes)))

    def position(j):
        _, total, (start, end) = lanes[j]
        return start + (end - start) * done[j] / total

    while live:
        i = min(live, key=position)
        try:
            done[i] += next(lanes[i][0])
        except StopIteration:
            live.remove(i)


def _scaled(steps, factor):
    try:
        while True:
            yield next(steps) * factor
    except StopIteration as done:
        return done.value


def _chunks(n_cols):
    return [(c, min(CHUNK_COLS, n_cols - c)) for c in range(0, n_cols, CHUNK_COLS)]


def _col_dot(a, w_ref, c0, width):
    return jnp.dot(a, w_ref[:, c0:c0 + width], preferred_element_type=_F32)


def _dot_cost(w_ref, width):
    return -(-w_ref.shape[0] // MXU_COLS) * (width // MXU_COLS)


def _chunks_cost(w_ref):
    return _dot_cost(w_ref, w_ref.shape[1])


def _dot_cols(a, w_ref):
    a = a.astype(_BF16)
    cols = []
    for c, width in _chunks(w_ref.shape[1]):
        cols.append(_col_dot(a, w_ref, c, width))
        yield _dot_cost(w_ref, width)
    return jnp.concatenate(cols, axis=1)


def _ffn_norm(h, g_ffn):
    return ((h * _rms_scale(h)) * g_ffn[...]).astype(_BF16)


def _ffn_and_ple_steps(h, hf, p, g_ple, w_gate, w_up, w_down, ple_gate, ple_proj):
    acts = []
    for c, width in _chunks(w_gate.shape[1]):
        g = _col_dot(hf, w_gate, c, width)
        yield _dot_cost(w_gate, width)
        u = _col_dot(hf, w_up, c, width)
        yield _dot_cost(w_up, width)
        acts.append(((g * _sigmoid(g)) * u).astype(_BF16))
    h = h + (yield from _dot_cols(jnp.concatenate(acts, axis=1), w_down))
    hp = (h * _rms_scale(h)) * g_ple[...]
    gate = _sigmoid((yield from _dot_cols(hp, ple_gate)))
    return h + gate * (yield from _dot_cols(p, ple_proj))


def _ffn_and_ple_cost(w_gate, w_down, ple_gate, ple_proj):
    return 2 * _chunks_cost(w_gate) + _chunks_cost(w_down) + _chunks_cost(ple_gate) + _chunks_cost(ple_proj)


def _conv_mixer_steps(x, g_mix, w_in, conv_w, w_out, history):
    hn = ((x * _rms_scale(x)) * g_mix[...]).astype(_BF16)
    yield 0
    us, convs = [], []
    for c, width in _chunks(D_MODEL):
        cols = slice(c, c + width)
        c_gate = _col_dot(hn, w_in, D_MODEL + c, width)
        yield _dot_cost(w_in, width)
        xin = _col_dot(hn, w_in, 2 * D_MODEL + c, width)
        yield _dot_cost(w_in, width)
        u = c_gate * xin
        back2, back1 = history(cols, u)
        conv = conv_w[0:1, cols] * back2
        conv = conv + conv_w[1:2, cols] * back1
        conv = conv + conv_w[2:3, cols] * u
        us.append(u)
        convs.append(conv)
    gated = []
    for j, (c, width) in enumerate(_chunks(D_MODEL)):
        b_gate = _col_dot(hn, w_in, c, width)
        yield _dot_cost(w_in, width)
        gated.append((b_gate * convs[j]).astype(_BF16))
    y = yield from _dot_cols(jnp.concatenate(gated, axis=1), w_out)
    return x + y, jnp.concatenate(us, axis=1)


def _conv_mixer_cost(w_in, w_out):
    return _chunks_cost(w_in) + _chunks_cost(w_out)


def _pipeline_bodies(step, n_tiles, body):
    pl.when(step == 0)(lambda: body(True, False))
    pl.when((step > 0) & (step < n_tiles))(lambda: body(True, True))
    pl.when(step == n_tiles)(lambda: body(False, True))


def _load_weights(plan, staging, sems):
    chunks = [(src, dst, c0, r0, min(STAGING_ROWS, src.shape[0] - r0))
              for src, dst, c0 in plan for r0 in range(0, src.shape[0], STAGING_ROWS)]

    def copy(i):
        src, _, _, r0, n = chunks[i]
        window = staging.at[i % 2, pl.ds(0, n), pl.ds(0, src.shape[1])]
        return pltpu.make_async_copy(src.at[pl.ds(r0, n), :], window, sems.at[i % 2])

    copy(0).start()
    for i, (src, dst, c0, r0, n) in enumerate(chunks):
        if i + 1 < len(chunks):
            copy(i + 1).start()
        copy(i).wait()
        cols = src.shape[1]
        dst[r0:r0 + n, c0:c0 + cols] = staging[i % 2, 0:n, 0:cols].astype(_BF16)


def _layer0_kernel(x_ref, p_ref, xs_ref, ps_ref, s0_ref, s1_ref, g_mix, g_ffn, g_ple, conv_w,
                   w_in_hbm, w_out_hbm, w_gate_hbm, w_up_hbm, w_down_hbm, ple_gate_hbm, ple_proj_hbm,
                   h_out, state_out, hs_out, us_out,
                   ubuf, hbuf, hfbuf, w_in, w_out, w_gate, w_up, w_down, ple_gate, ple_proj, staging, sems,
                   *, layer, tiles_per_seq, n_tiles):
    tm = x_ref.shape[0]
    step = pl.program_id(0)
    ffn_cost = _ffn_and_ple_cost(w_gate, w_down, ple_gate, ple_proj)
    conv_cost = _conv_mixer_cost(w_in, w_out)

    def history(cols, u):
        ubuf[SUBLANES:SUBLANES + tm, cols] = u
        return (ubuf[SUBLANES - 2:SUBLANES - 2 + tm, cols], ubuf[SUBLANES - 1:SUBLANES - 1 + tm, cols])

    def ffn_half():
        out = yield from _ffn_and_ple_steps(hbuf[...], hfbuf[...], p_ref[...], g_ple,
                                            w_gate, w_up, w_down, ple_gate, ple_proj)
        h_out[...] = out

    def conv_half():
        h, _ = yield from _conv_mixer_steps(x_ref[...], g_mix, w_in, conv_w, w_out, history)
        state_out[...] = ubuf[SUBLANES + tm - (CONV_WIDTH - 1):SUBLANES + tm, :]
        ubuf[0:SUBLANES, :] = ubuf[tm:tm + SUBLANES, :]
        hbuf[...] = h
        hfbuf[...] = _ffn_norm(h, g_ffn)

    def sample_layer():
        history = lambda cols, u: (s0_ref[:, cols], s1_ref[:, cols])
        h, u = yield from _conv_mixer_steps(xs_ref[...], g_mix, w_in, conv_w, w_out, history)
        out = yield from _ffn_and_ple_steps(h, _ffn_norm(h, g_ffn), ps_ref[...], g_ple,
                                            w_gate, w_up, w_down, ple_gate, ple_proj)
        hs_out[...] = out
        us_out[...] = u

    def body(first_half, second_half):
        lanes = []
        if first_half and not second_half:
            _load_weights([(w_in_hbm.at[0], w_in, 0), (w_out_hbm.at[0], w_out, 0),
                           (w_gate_hbm.at[layer], w_gate, 0), (w_up_hbm.at[layer], w_up, 0),
                           (w_down_hbm.at[layer], w_down, 0), (ple_gate_hbm.at[layer], ple_gate, 0),
                           (ple_proj_hbm.at[layer], ple_proj, 0)], staging, sems)
        if first_half:
            zeros = jnp.zeros((SUBLANES, D_MODEL), _F32)
            fresh = lax.rem(step, tiles_per_seq) == 0
            ubuf[0:SUBLANES, :] = jnp.where(fresh, zeros, ubuf[0:SUBLANES, :]) if second_half else zeros
        if second_half:
            ffn = ffn_half()
            next(ffn)
            lanes.append((ffn, ffn_cost, LAYER0_WINDOWS[1]))
        if first_half:
            conv = conv_half()
            next(conv)
            lanes.append((conv, conv_cost, LAYER0_WINDOWS[0]))
        else:
            rows = xs_ref.shape[0] / tm
            lanes.append((_scaled(sample_layer(), rows), (conv_cost + ffn_cost) * rows, (0.0, 1.0)))
        _interleave(*lanes)

    _pipeline_bodies(step, n_tiles, body)


def _qkv_steps(h, cos, sin_lo, sin_hi, g_kv, g_mix, w_kv, w_q):
    hs = h * _rms_scale(h)
    h_kv = (hs * g_kv[...]).astype(_BF16)
    h_q = (hs * g_mix[...]).astype(_BF16)
    yield 0
    kv = yield from _dot_cols(h_kv, w_kv)
    q = yield from _dot_cols(h_q, w_q)
    k = _rope(kv[:, :KV_WIDTH], cos, sin_lo, sin_hi)
    v = kv[:, KV_WIDTH:]
    q = _rope(q, cos, sin_lo, sin_hi) * (HEAD_DIM ** -0.5)
    return q, k, v


def _regroup_heads(x, order):
    low = lax.broadcasted_iota(jnp.int32, (1, LANES), 1) < HEAD_DIM
    out = []
    for s in range(x.shape[1] // LANES):
        halves = []
        for half in range(2):
            src_head = order[2 * s + half]
            slab = x[:, (src_head // 2) * LANES:(src_head // 2 + 1) * LANES]
            halves.append(slab if src_head % 2 == half else pltpu.roll(slab, HEAD_DIM, 1))
        out.append(jnp.where(low, halves[0], halves[1]))
    return jnp.concatenate(out, axis=1)


_SLOT_SWAP = [GROUP * (p % N_KV_HEADS) + p // N_KV_HEADS for p in range(N_HEADS)]


def _layer1_kernel(h_ref, p_ref, cos_ref, slo_ref, shi_ref, sinks_ref,
                   hs_ref, pss_ref, cos_s, slo_s, shi_s, sink_rows, ck_ref, cv_ref,
                   g_kv, g_mix, g_ffn, g_ple, g_final,
                   w_k_hbm, w_v_hbm, w_q_hbm, w_o_hbm, w_gate_hbm, w_up_hbm, w_down_hbm,
                   ple_gate_hbm, ple_proj_hbm,
                   y_out, kwin_out, vwin_out, ys_out, kwins_out, vwins_out,
                   k2buf, v2buf, hbuf, obuf, qs_buf, ks_buf, vs_buf, kst_buf, vst_buf, os_buf,
                   w_kv, w_q, w_o, w_gate, w_up, w_down, ple_gate, ple_proj, staging, sems,
                   *, layer, tiles_per_seq, n_tiles):
    tm = h_ref.shape[0]
    step = pl.program_id(0)
    attn_cost = GROUP * BLOCK / tm
    dense_cost = _chunks_cost(w_o) + _ffn_and_ple_cost(w_gate, w_down, ple_gate, ple_proj)
    per_step = ck_ref.shape[0]
    sample_rows = hs_ref.shape[0] / tm

    def dense_half():
        hp = hbuf[...] + (yield from _dot_cols(obuf[...], w_o))
        hp = yield from _ffn_and_ple_steps(hp, _ffn_norm(hp, g_ffn), p_ref[...], g_ple,
                                           w_gate, w_up, w_down, ple_gate, ple_proj)
        y_out[...] = (hp * _rms_scale(hp)) * g_final[...]

    def sample_qkv():
        q, k, v = yield from _qkv_steps(hs_ref[...], cos_s[...], slo_s[...], shi_s[...],
                                        g_kv, g_mix, w_kv, w_q)
        qs_buf[...] = _regroup_heads(q, _SLOT_SWAP)
        ks_buf[...] = k
        vs_buf[...] = v
        kst_buf[...] = k.T
        vst_buf[...] = v.T

    def sample_decode(j):
        r = (step - 1) * per_step + j
        w = ck_ref.shape[2]
        key = lax.broadcasted_iota(jnp.int32, (KV_WIDTH, w), 1)
        seq_lane = lax.broadcasted_iota(jnp.int32, kst_buf.shape, 1)
        lane = lax.broadcasted_iota(jnp.int32, (N_HEADS, KV_WIDTH), 1)
        r16 = lax.broadcasted_iota(jnp.int32, (N_HEADS, KV_WIDTH), 0)
        slot = r16 // N_KV_HEADS
        kvh = r16 - slot * N_KV_HEADS
        own = (lane >= kvh * HEAD_DIM) & (lane < (kvh + 1) * HEAD_DIM)
        sink = sink_rows[...]
        kn = ks_buf[pl.ds(r, 1), :]
        vn = vs_buf[pl.ds(r, 1), :]
        for win_ref, new_t, out_ref in ((ck_ref, kst_buf, kwins_out), (cv_ref, vst_buf, vwins_out)):
            column = jnp.sum(jnp.where(seq_lane == r, new_t[...], 0.0), axis=1, keepdims=True)
            out_ref[j] = jnp.where(key == w - 1, column, pltpu.roll(win_ref[j], w - 1, 1))
        lhs = jnp.zeros((N_HEADS, KV_WIDTH), _F32)
        for i in range(GROUP):
            q_i = jnp.broadcast_to(qs_buf[pl.ds(r, 1), i * KV_WIDTH:(i + 1) * KV_WIDTH], (N_HEADS, KV_WIDTH))
            lhs = jnp.where(own & (slot == i), q_i, lhs)
        lhs = lhs.astype(_BF16)
        s_old = jnp.dot(lhs, ck_ref[j].astype(_BF16), preferred_element_type=_F32)
        yield 1
        s_new = jnp.sum(lhs.astype(_F32) * kn.astype(_BF16).astype(_F32), axis=-1, keepdims=True)
        m = jnp.maximum(jnp.maximum(jnp.max(s_old, axis=-1, keepdims=True), s_new), sink)
        e_old = jnp.exp(s_old - m)
        e_new = jnp.exp(s_new - m)
        inv = 1.0 / (jnp.sum(e_old, axis=-1, keepdims=True) + e_new + jnp.exp(sink - m))
        yield 1
        o16 = lax.dot_general(e_old.astype(_BF16), cv_ref[j].astype(_BF16), (((1,), (1,)), ((), ())),
                              preferred_element_type=_F32)
        o16 = (o16 + e_new.astype(_BF16).astype(_F32) * vn.astype(_BF16).astype(_F32)) * inv
        for i in range(GROUP):
            picked = jnp.where(own & (slot == i), o16, 0.0)
            os_buf[pl.ds(r, 1), i * KV_WIDTH:(i + 1) * KV_WIDTH] = jnp.sum(picked, axis=0, keepdims=True)
        yield 1

    def sample_decodes():
        for j in range(per_step):
            yield from sample_decode(j)

    def sample_tail():
        yield from sample_decodes()
        o = _regroup_heads(os_buf[...], _SLOT_SWAP)
        h = hs_ref[...] + (yield from _scaled(_dot_cols(o, w_o), sample_rows))
        h = yield from _scaled(_ffn_and_ple_steps(h, _ffn_norm(h, g_ffn), pss_ref[...], g_ple,
                                                  w_gate, w_up, w_down, ple_gate, ple_proj), sample_rows)
        ys_out[...] = (h * _rms_scale(h)) * g_final[...]

    def attention_half():
        h = h_ref[...]
        q, k, v = yield from _qkv_steps(h, cos_ref[...], slo_ref[...], shi_ref[...], g_kv, g_mix, w_kv, w_q)
        qb = q.astype(_BF16)
        kwin_out[...] = k[tm - WINDOW:, :]
        vwin_out[...] = v[tm - WINDOW:, :]

        lane = lax.broadcasted_iota(jnp.int32, (1, LANES), 1)
        low = lane < HEAD_DIM
        for arr, buf in ((k, k2buf), (v, v2buf)):
            for s in range(KV_WIDTH // LANES):
                slab = arr[:, s * LANES:(s + 1) * LANES]
                swapped = pltpu.roll(slab, HEAD_DIM, 1)
                buf[2 * s, BLOCK:BLOCK + tm, :] = jnp.where(low, slab, swapped).astype(_BF16)
                buf[2 * s + 1, BLOCK:BLOCK + tm, :] = jnp.where(low, swapped, slab).astype(_BF16)

        row = lax.broadcasted_iota(jnp.int32, (BLOCK, 2 * BLOCK), 0)
        col = lax.broadcasted_iota(jnp.int32, (BLOCK, 2 * BLOCK), 1)
        diff = row + BLOCK - col
        band = (diff >= 0) & (diff <= WINDOW)
        first_col = jnp.where(lax.rem(step, tiles_per_seq) == 0, BLOCK, 0)
        zero = jnp.zeros((), _BF16)
        ones = jnp.ones((2 * BLOCK, LANES), _BF16)
        units = [(jb, g) for jb in range(tm // BLOCK) for g in range(N_KV_HEADS)]

        def scores(jb, g):
            r0 = jb * BLOCK
            parts = []
            for s in (2 * g, 2 * g + 1):
                slab = qb[r0:r0 + BLOCK, s * LANES:(s + 1) * LANES]
                parts.append(jnp.where(low, slab, zero))
                parts.append(jnp.where(low, zero, slab))
            qs = jnp.concatenate(parts, axis=0)
            return lax.dot_general(qs, k2buf[g, r0:r0 + 2 * BLOCK, :],
                                   (((1,), (1,)), ((), ())), preferred_element_type=_F32)

        def weights(jb, g, sc):
            valid = band & (col >= first_col) if jb == 0 else band
            es, tail = [], []
            for i in range(GROUP):
                sink = sinks_ref[GROUP * g + i]
                sp = jnp.where(valid, sc[i * BLOCK:(i + 1) * BLOCK], NEG_INF)
                m = jnp.maximum(jnp.max(sp, axis=-1, keepdims=True), sink)
                es.append(jnp.exp(sp - m).astype(_BF16))
                tail.append(jnp.exp(sink - m))
            return jnp.concatenate(es, axis=0), tail

        def values(jb, g, e, tail):
            r0 = jb * BLOCK
            vv = jnp.concatenate([v2buf[g, r0:r0 + 2 * BLOCK, :], ones], axis=1)
            o3 = jnp.dot(e, vv, preferred_element_type=_F32)
            o2 = []
            for i in range(GROUP):
                piece = o3[i * BLOCK:(i + 1) * BLOCK]
                o2.append(piece[:, :LANES] * (1.0 / (piece[:, LANES:] + tail[i])))
            obuf[r0:r0 + BLOCK, (2 * g) * LANES:(2 * g + 1) * LANES] = (
                jnp.where(low, o2[0], o2[1]).astype(_BF16))
            obuf[r0:r0 + BLOCK, (2 * g + 1) * LANES:(2 * g + 2) * LANES] = (
                jnp.where(low, o2[2], o2[3]).astype(_BF16))

        sc, ew = {}, {}
        for n in range(len(units) + 2):
            if n < len(units):
                sc[n] = scores(*units[n])
                yield attn_cost
            if 0 <= n - 1 < len(units):
                ew[n - 1] = weights(*units[n - 1], sc.pop(n - 1))
                yield 0.0
            if 0 <= n - 2 < len(units):
                values(*units[n - 2], *ew.pop(n - 2))
                yield attn_cost
        hbuf[...] = h
        k2buf[:, 0:BLOCK, :] = k2buf[:, tm:tm + BLOCK, :]
        v2buf[:, 0:BLOCK, :] = v2buf[:, tm:tm + BLOCK, :]

    def body(first_half, second_half):
        lanes = []
        qkv_cost = _chunks_cost(w_kv) + _chunks_cost(w_q)
        if first_half and not second_half:
            _load_weights([(w_k_hbm, w_kv, 0), (w_v_hbm, w_kv, KV_WIDTH),
                           (w_q_hbm.at[0], w_q, 0), (w_o_hbm.at[0], w_o, 0),
                           (w_gate_hbm.at[layer], w_gate, 0), (w_up_hbm.at[layer], w_up, 0),
                           (w_down_hbm.at[layer], w_down, 0), (ple_gate_hbm.at[layer], ple_gate, 0),
                           (ple_proj_hbm.at[layer], ple_proj, 0)], staging, sems)
            zeros = jnp.zeros((N_KV_HEADS, BLOCK, LANES), _BF16)
            k2buf[:, 0:BLOCK, :] = zeros
            v2buf[:, 0:BLOCK, :] = zeros
        if second_half:
            dense = dense_half()
            next(dense)
            lanes.append((dense, dense_cost, LAYER1_WINDOWS[1]))
        if first_half:
            n_units = (tm // BLOCK) * N_KV_HEADS
            attention = attention_half()
            next(attention)
            lanes.append((attention, qkv_cost + 2 * n_units * attn_cost, LAYER1_WINDOWS[0]))
        if first_half and not second_half:
            lanes.append((_scaled(sample_qkv(), sample_rows), qkv_cost * sample_rows, (0.0, 1.0)))
        elif first_half:
            lanes.append((sample_decodes(), 3 * per_step, SAMPLE_DECODE_WINDOW))
        else:
            lanes.append((sample_tail(), 3 * per_step + dense_cost * sample_rows, (0.0, 1.0)))
        _interleave(*lanes)

    _pipeline_bodies(step, n_tiles, body)


class _Layer:
    def __init__(self, stacked, layer):
        self.stacked, self.layer = stacked, layer


def _operand(x):
    return x.stacked if isinstance(x, _Layer) else x


def _resident(x):
    if isinstance(x, _Layer):
        rest = x.stacked.shape[1:]
        index = (x.layer,) + (0,) * len(rest)
        return pl.BlockSpec((None,) + rest, lambda *_: index, pipeline_mode=pl.Buffered(1))
    nd = x.ndim
    return pl.BlockSpec(x.shape, lambda *_: (0,) * nd, pipeline_mode=pl.Buffered(1))


def _params(n_grid_dims):
    return pltpu.CompilerParams(dimension_semantics=("arbitrary",) * n_grid_dims,
                                vmem_limit_bytes=VMEM_LIMIT_BYTES)


def _rope_tables(pos):
    half = ROT_DIM // 2
    inv_freq = np.power(np.float64(ROPE_THETA), -np.arange(half, dtype=np.float64) / half)
    ang = pos.astype(np.float64)[:, None] * inv_freq[None, :]
    cos, sin = np.cos(ang), np.sin(ang)
    n = pos.shape[0]
    pad = np.zeros((n, HEAD_DIM - ROT_DIM))
    zero = np.zeros((n, half))
    cos_h = np.concatenate([cos, cos, pad + 1.0], axis=1)
    lo_h = np.concatenate([-sin, zero, pad], axis=1)
    hi_h = np.concatenate([zero, sin, pad], axis=1)
    rep = LANES // HEAD_DIM
    return tuple(jnp.asarray(np.tile(a, (1, rep)), dtype=_F32) for a in (cos_h, lo_h, hi_h))


def kernel(x_prompt, x_sample, state_conv, cache_k_win, cache_v_win, p_prompt, p_sample,
           norm_mix_g, norm_ffn_g, norm_ple_g, kv_norm_g, final_norm_g,
           conv_w_in, conv_w, conv_w_out, w_k, w_v, w_q, sinks, w_o,
           ffn_w_gate, ffn_w_up, ffn_w_down, ple_w_proj, ple_w_gate):
    batch, seq, d = x_prompt.shape
    dec = x_sample.shape[0]
    w_buf = cache_k_win.shape[1]
    tm = PROMPT_TILE
    assert d == D_MODEL and seq % tm == 0 and tm % BLOCK == 0 and tm >= WINDOW
    assert x_sample.shape[1] == 1 and w_buf == WINDOW and dec % (batch * (seq // tm)) == 0

    row = lambda a: a.reshape(1, -1)
    rows = lambda a: a.reshape(a.shape[0], 1, a.shape[1])
    gains = dict(g_mix=rows(norm_mix_g), g_ffn=rows(norm_ffn_g), g_ple=rows(norm_ple_g))
    hbm = pl.BlockSpec(memory_space=pl.ANY)
    ffn_weights = [ffn_w_gate, ffn_w_up, ffn_w_down, ple_w_gate, ple_w_proj]
    staging = [pltpu.VMEM((2, STAGING_ROWS, max(w.shape[-1] for w in [conv_w_in] + ffn_weights)), _F32),
               pltpu.SemaphoreType.DMA((2,))]
    resident_bf16 = lambda ws: [pltpu.VMEM(w.shape[-2:], _BF16) for w in ws]

    nt = seq // tm
    n_tiles = batch * nt
    cur = lambda s: jnp.minimum(s, n_tiles - 1)
    prev = lambda s: jnp.maximum(s - 1, 0)
    cur_tile = pl.BlockSpec((None, tm, d), lambda s: (cur(s) // nt, cur(s) % nt, 0))
    prev_tile = pl.BlockSpec((None, tm, d), lambda s: (prev(s) // nt, prev(s) % nt, 0))
    prev_ple = lambda layer: pl.BlockSpec((None, None, tm, p_prompt.shape[-1]),
                                          lambda s: (layer, prev(s) // nt, prev(s) % nt, 0))
    xs = x_sample.reshape(dec, d)
    ps = p_sample.reshape(p_sample.shape[0], dec, p_sample.shape[-1])
    sample_in = [xs, _Layer(ps, 0), state_conv[0, :, 0, :], state_conv[0, :, 1, :]]
    l0_small = [_Layer(gains[n], 0) for n in ("g_mix", "g_ffn", "g_ple")] + [_Layer(conv_w, 0)]
    l0_weights = [conv_w_in, conv_w_out] + ffn_weights
    sample_rows = pl.BlockSpec((dec, d), lambda s: (0, 0))
    h1, conv_state_prompt, h1s, u_s = pl.pallas_call(
        functools.partial(_layer0_kernel, layer=0, tiles_per_seq=nt, n_tiles=n_tiles),
        grid=(n_tiles + 1,),
        in_specs=[cur_tile, prev_ple(0)] + [_resident(a) for a in sample_in + l0_small]
                 + [hbm] * len(l0_weights),
        out_specs=[prev_tile, pl.BlockSpec((None, CONV_WIDTH - 1, d), lambda s: (cur(s) // nt, 0, 0)),
                   sample_rows, sample_rows],
        out_shape=[jax.ShapeDtypeStruct((batch, seq, d), _F32),
                   jax.ShapeDtypeStruct((batch, CONV_WIDTH - 1, d), _F32),
                   jax.ShapeDtypeStruct((dec, d), _F32),
                   jax.ShapeDtypeStruct((dec, d), _F32)],
        scratch_shapes=[pltpu.VMEM((tm + SUBLANES, d), _F32),
                        pltpu.VMEM((tm, d), _F32),
                        pltpu.VMEM((tm, d), _BF16)] + resident_bf16(l0_weights) + staging,
        compiler_params=_params(1),
        name="layer0",
    )(x_prompt, p_prompt, *map(_operand, sample_in + l0_small), *l0_weights)
    conv_state_sample = jnp.stack([state_conv[0, :, 1, :], u_s], axis=1)[None]

    cos_p, lo_p, hi_p = _rope_tables(np.arange(seq))
    cos_s, lo_s, hi_s = _rope_tables(np.full((dec,), PAST_LEN))
    sink_rows = sinks[0].reshape(N_KV_HEADS, GROUP).T.reshape(N_HEADS, 1)
    windows_t = lambda a: jnp.transpose(a, (0, 2, 3, 1)).reshape(a.shape[0], KV_WIDTH, a.shape[1])
    windows = lambda a: jnp.transpose(a.reshape(a.shape[0], N_KV_HEADS, HEAD_DIM, a.shape[2]), (0, 3, 1, 2))
    ck, cv = windows_t(cache_k_win), windows_t(cache_v_win)
    per_step = dec // n_tiles
    cache = pl.BlockSpec((per_step, KV_WIDTH, w_buf), lambda s: (prev(s), 0, 0))
    sample_l1 = [h1s, _Layer(ps, 1), cos_s, lo_s, hi_s, sink_rows]
    l1_small = [row(kv_norm_g)] + [_Layer(gains[n], 1) for n in ("g_mix", "g_ffn", "g_ple")] + [row(final_norm_g)]
    l1_weights = [w_k, w_v, w_q, w_o] + ffn_weights
    l1_resident = [pltpu.VMEM((d, 2 * KV_WIDTH), _BF16)] + resident_bf16([w_q, w_o] + ffn_weights)
    y_prompt, k_win_prompt, v_win_prompt, y_sample, k_win_sample, v_win_sample = pl.pallas_call(
        functools.partial(_layer1_kernel, layer=1, tiles_per_seq=nt, n_tiles=n_tiles),
        grid=(n_tiles + 1,),
        in_specs=[cur_tile, prev_ple(1)]
                 + [pl.BlockSpec((tm, LANES), lambda s: (cur(s) % nt, 0))] * 3
                 + [pl.BlockSpec(memory_space=pltpu.SMEM)]
                 + [_resident(a) for a in sample_l1] + [cache, cache]
                 + [_resident(a) for a in l1_small] + [hbm] * len(l1_weights),
        out_specs=[prev_tile,
                   pl.BlockSpec((None, WINDOW, KV_WIDTH), lambda s: (cur(s) // nt, 0, 0)),
                   pl.BlockSpec((None, WINDOW, KV_WIDTH), lambda s: (cur(s) // nt, 0, 0)),
                   sample_rows, cache, cache],
        out_shape=[jax.ShapeDtypeStruct((batch, seq, d), _F32),
                   jax.ShapeDtypeStruct((batch, WINDOW, KV_WIDTH), _F32),
                   jax.ShapeDtypeStruct((batch, WINDOW, KV_WIDTH), _F32),
                   jax.ShapeDtypeStruct((dec, d), _F32),
                   jax.ShapeDtypeStruct((dec, KV_WIDTH, w_buf), _F32),
                   jax.ShapeDtypeStruct((dec, KV_WIDTH, w_buf), _F32)],
        scratch_shapes=[pltpu.VMEM((N_KV_HEADS, BLOCK + tm, LANES), _BF16),
                        pltpu.VMEM((N_KV_HEADS, BLOCK + tm, LANES), _BF16),
                        pltpu.VMEM((tm, d), _F32),
                        pltpu.VMEM((tm, d), _BF16),
                        pltpu.VMEM((dec, d), _F32),
                        pltpu.VMEM((dec, KV_WIDTH), _F32),
                        pltpu.VMEM((dec, KV_WIDTH), _F32),
                        pltpu.VMEM((KV_WIDTH, dec), _F32),
                        pltpu.VMEM((KV_WIDTH, dec), _F32),
                        pltpu.VMEM((dec, d), _F32)] + l1_resident + staging,
        compiler_params=_params(1),
        name="layer1",
    )(h1, p_prompt, cos_p, lo_p, hi_p, sinks[0], *map(_operand, sample_l1), ck, cv,
      *map(_operand, l1_small), *l1_weights)

    kv_shape = (N_KV_HEADS, HEAD_DIM)
    return (y_prompt, y_sample.reshape(dec, 1, d),
            conv_state_prompt[None], conv_state_sample,
            k_win_prompt.reshape(batch, w_buf, *kv_shape), v_win_prompt.reshape(batch, w_buf, *kv_shape),
            windows(k_win_sample), windows(v_win_sample))
```

```python
import functools

import jax
import jax.numpy as jnp
import numpy as np
from jax import lax
from jax.experimental import pallas as pl
from jax.experimental.pallas import tpu as pltpu

D_MODEL = 1024
HEAD_DIM = 64
N_HEADS = 16
N_KV_HEADS = 4
GROUP = N_HEADS // N_KV_HEADS
KV_WIDTH = N_KV_HEADS * HEAD_DIM
ROT_DIM = HEAD_DIM // 4
ROPE_THETA = 500000.0
WINDOW = 128
BLOCK = 128
CONV_WIDTH = 3
PAST_LEN = 16384
RMS_EPS = 1e-6
NEG_INF = -1e30

LANES = 128
SUBLANES = 8
MXU_COLS = 256
CHUNK_COLS = 2 * MXU_COLS
VMEM_LIMIT_BYTES = 60000 * 1024

PROMPT_TILE = 256
STAGING_ROWS = 256
SAMPLE_DECODE_WINDOW = (0.05, 0.85)
LAYER0_WINDOWS = ((0.10, 0.92), (0.0, 1.0))
LAYER1_WINDOWS = ((0.04, 0.92), (0.0, 1.0))

_BF16 = jnp.bfloat16
_F32 = jnp.float32


def _rms_scale(x):
    return lax.rsqrt(jnp.mean(x * x, axis=-1, keepdims=True) + RMS_EPS)


def _sigmoid(x):
    return 1.0 / (1.0 + jnp.exp(-x))


def _rope(x, cos, sin_lo, sin_hi):
    half = ROT_DIM // 2
    out = []
    for c in range(x.shape[1] // LANES):
        slab = x[:, c * LANES:(c + 1) * LANES]
        out.append(slab * cos
                   + pltpu.roll(slab, LANES - half, 1) * sin_lo
                   + pltpu.roll(slab, half, 1) * sin_hi)
    return jnp.concatenate(out, axis=1)


def _interleave(*lanes):
    done = [0.0] * len(lanes)
    live = list(range(len(lanes)))

    def position(j):
        _, total, (start, end) = lanes[j]
        return start + (end - start) * done[j] / total

    while live:
        i = min(live, key=position)
        try:
            done[i] += next(lanes[i][0])
        except StopIteration:
            live.remove(i)


def _scaled(steps, factor):
    try:
        while True:
            yield next(steps) * factor
    except StopIteration as done:
        return done.value


def _chunks(n_cols):
    return [(c, min(CHUNK_COLS, n_cols - c)) for c in range(0, n_cols, CHUNK_COLS)]


def _col_dot(a, w_ref, c0, width):
    return jnp.dot(a, w_ref[:, c0:c0 + width], preferred_element_type=_F32)


def _dot_cost(w_ref, width):
    return -(-w_ref.shape[0] // MXU_COLS) * (width // MXU_COLS)


def _chunks_cost(w_ref):
    return _dot_cost(w_ref, w_ref.shape[1])


def _dot_cols(a, w_ref):
    a = a.astype(_BF16)
    cols = []
    for c, width in _chunks(w_ref.shape[1]):
        cols.append(_col_dot(a, w_ref, c, width))
        yield _dot_cost(w_ref, width)
    return jnp.concatenate(cols, axis=1)


def _ffn_norm(h, g_ffn):
    return ((h * _rms_scale(h)) * g_ffn[...]).astype(_BF16)


def _ffn_and_ple_steps(h, hf, p, g_ple, w_gate, w_up, w_down, ple_gate, ple_proj):
    acts = []
    for c, width in _chunks(w_gate.shape[1]):
        g = _col_dot(hf, w_gate, c, width)
        yield _dot_cost(w_gate, width)
        u = _col_dot(hf, w_up, c, width)
        yield _dot_cost(w_up, width)
        acts.append(((g * _sigmoid(g)) * u).astype(_BF16))
    h = h + (yield from _dot_cols(jnp.concatenate(acts, axis=1), w_down))
    hp = (h * _rms_scale(h)) * g_ple[...]
    gate = _sigmoid((yield from _dot_cols(hp, ple_gate)))
    return h + gate * (yield from _dot_cols(p, ple_proj))


def _ffn_and_ple_cost(w_gate, w_down, ple_gate, ple_proj):
    return 2 * _chunks_cost(w_gate) + _chunks_cost(w_down) + _chunks_cost(ple_gate) + _chunks_cost(ple_proj)


def _conv_mixer_steps(x, g_mix, w_in, conv_w, w_out, history):
    hn = ((x * _rms_scale(x)) * g_mix[...]).astype(_BF16)
    yield 0
    us, convs = [], []
    for c, width in _chunks(D_MODEL):
        cols = slice(c, c + width)
        c_gate = _col_dot(hn, w_in, D_MODEL + c, width)
        yield _dot_cost(w_in, width)
        xin = _col_dot(hn, w_in, 2 * D_MODEL + c, width)
        yield _dot_cost(w_in, width)
        u = c_gate * xin
        back2, back1 = history(cols, u)
        conv = conv_w[0:1, cols] * back2
        conv = conv + conv_w[1:2, cols] * back1
        conv = conv + conv_w[2:3, cols] * u
        us.append(u)
        convs.append(conv)
    gated = []
    for j, (c, width) in enumerate(_chunks(D_MODEL)):
        b_gate = _col_dot(hn, w_in, c, width)
        yield _dot_cost(w_in, width)
        gated.append((b_gate * convs[j]).astype(_BF16))
    y = yield from _dot_cols(jnp.concatenate(gated, axis=1), w_out)
    return x + y, jnp.concatenate(us, axis=1)


def _conv_mixer_cost(w_in, w_out):
    return _chunks_cost(w_in) + _chunks_cost(w_out)


def _pipeline_bodies(step, n_tiles, body):
    pl.when(step == 0)(lambda: body(True, False))
    pl.when((step > 0) & (step < n_tiles))(lambda: body(True, True))
    pl.when(step == n_tiles)(lambda: body(False, True))


def _drain(steps):
    for _ in steps:
        pass


def _weight_load_steps(plan, staging, sems):
    chunks = [(src, dst, c0, r0, min(STAGING_ROWS, src.shape[0] - r0))
              for src, dst, c0 in plan for r0 in range(0, src.shape[0], STAGING_ROWS)]

    def copy(i):
        src, _, _, r0, n = chunks[i]
        window = staging.at[i % 2, pl.ds(0, n), pl.ds(0, src.shape[1])]
        return pltpu.make_async_copy(src.at[pl.ds(r0, n), :], window, sems.at[i % 2])

    copy(0).start()
    for i, (src, dst, c0, r0, n) in enumerate(chunks):
        if i + 1 < len(chunks):
            copy(i + 1).start()
        copy(i).wait()
        cols = src.shape[1]
        dst[r0:r0 + n, c0:c0 + cols] = staging[i % 2, 0:n, 0:cols].astype(_BF16)
        yield 1


def _load_chunks(plan):
    return sum(-(-src.shape[0] // STAGING_ROWS) for src, _, _ in plan)


def _layer0_kernel(x_ref, p_ref, xs_ref, ps_ref, s0_ref, s1_ref, g_mix, g_ffn, g_ple, conv_w,
                   w_in_hbm, w_out_hbm, w_gate_hbm, w_up_hbm, w_down_hbm, ple_gate_hbm, ple_proj_hbm,
                   h_out, state_out, hs_out, us_out,
                   ubuf, hbuf, hfbuf, w_in, w_out, w_gate, w_up, w_down, ple_gate, ple_proj, staging, sems,
                   *, layer, tiles_per_seq, n_tiles):
    tm = x_ref.shape[0]
    step = pl.program_id(0)
    ffn_cost = _ffn_and_ple_cost(w_gate, w_down, ple_gate, ple_proj)
    conv_cost = _conv_mixer_cost(w_in, w_out)

    def history(cols, u):
        ubuf[SUBLANES:SUBLANES + tm, cols] = u
        return (ubuf[SUBLANES - 2:SUBLANES - 2 + tm, cols], ubuf[SUBLANES - 1:SUBLANES - 1 + tm, cols])

    def ffn_half():
        out = yield from _ffn_and_ple_steps(hbuf[...], hfbuf[...], p_ref[...], g_ple,
                                            w_gate, w_up, w_down, ple_gate, ple_proj)
        h_out[...] = out

    def conv_half():
        h, _ = yield from _conv_mixer_steps(x_ref[...], g_mix, w_in, conv_w, w_out, history)
        state_out[...] = ubuf[SUBLANES + tm - (CONV_WIDTH - 1):SUBLANES + tm, :]
        ubuf[0:SUBLANES, :] = ubuf[tm:tm + SUBLANES, :]
        hbuf[...] = h
        hfbuf[...] = _ffn_norm(h, g_ffn)

    def sample_layer():
        history = lambda cols, u: (s0_ref[:, cols], s1_ref[:, cols])
        h, u = yield from _conv_mixer_steps(xs_ref[...], g_mix, w_in, conv_w, w_out, history)
        out = yield from _ffn_and_ple_steps(h, _ffn_norm(h, g_ffn), ps_ref[...], g_ple,
                                            w_gate, w_up, w_down, ple_gate, ple_proj)
        hs_out[...] = out
        us_out[...] = u

    def body(first_half, second_half):
        lanes = []
        if first_half and not second_half:
            _drain(_weight_load_steps([(w_in_hbm.at[0], w_in, 0), (w_out_hbm.at[0], w_out, 0)], staging, sems))
            later = [(w_gate_hbm.at[layer], w_gate, 0), (w_up_hbm.at[layer], w_up, 0),
                     (w_down_hbm.at[layer], w_down, 0), (ple_gate_hbm.at[layer], ple_gate, 0),
                     (ple_proj_hbm.at[layer], ple_proj, 0)]
            lanes.append((_weight_load_steps(later, staging, sems), _load_chunks(later), (0.0, 1.0)))
        if first_half:
            zeros = jnp.zeros((SUBLANES, D_MODEL), _F32)
            fresh = lax.rem(step, tiles_per_seq) == 0
            ubuf[0:SUBLANES, :] = jnp.where(fresh, zeros, ubuf[0:SUBLANES, :]) if second_half else zeros
        if second_half:
            ffn = ffn_half()
            next(ffn)
            lanes.append((ffn, ffn_cost, LAYER0_WINDOWS[1]))
        if first_half:
            conv = conv_half()
            next(conv)
            lanes.append((conv, conv_cost, LAYER0_WINDOWS[0]))
        else:
            rows = xs_ref.shape[0] / tm
            lanes.append((_scaled(sample_layer(), rows), (conv_cost + ffn_cost) * rows, (0.0, 1.0)))
        _interleave(*lanes)

    _pipeline_bodies(step, n_tiles, body)


def _qkv_steps(h, cos, sin_lo, sin_hi, g_kv, g_mix, w_kv, w_q):
    hs = h * _rms_scale(h)
    h_kv = (hs * g_kv[...]).astype(_BF16)
    h_q = (hs * g_mix[...]).astype(_BF16)
    yield 0
    kv = yield from _dot_cols(h_kv, w_kv)
    q = yield from _dot_cols(h_q, w_q)
    k = _rope(kv[:, :KV_WIDTH], cos, sin_lo, sin_hi)
    v = kv[:, KV_WIDTH:]
    q = _rope(q, cos, sin_lo, sin_hi) * (HEAD_DIM ** -0.5)
    return q, k, v


def _regroup_heads(x, order):
    low = lax.broadcasted_iota(jnp.int32, (1, LANES), 1) < HEAD_DIM
    out = []
    for s in range(x.shape[1] // LANES):
        halves = []
        for half in range(2):
            src_head = order[2 * s + half]
            slab = x[:, (src_head // 2) * LANES:(src_head // 2 + 1) * LANES]
            halves.append(slab if src_head % 2 == half else pltpu.roll(slab, HEAD_DIM, 1))
        out.append(jnp.where(low, halves[0], halves[1]))
    return jnp.concatenate(out, axis=1)


_SLOT_SWAP = [GROUP * (p % N_KV_HEADS) + p // N_KV_HEADS for p in range(N_HEADS)]


def _layer1_kernel(h_ref, p_ref, cos_ref, slo_ref, shi_ref, sinks_ref,
                   hs_ref, pss_ref, cos_s, slo_s, shi_s, sink_rows, ck_ref, cv_ref,
                   g_kv, g_mix, g_ffn, g_ple, g_final,
                   w_k_hbm, w_v_hbm, w_q_hbm, w_o_hbm, w_gate_hbm, w_up_hbm, w_down_hbm,
                   ple_gate_hbm, ple_proj_hbm,
                   y_out, kwin_out, vwin_out, ys_out, kwins_out, vwins_out,
                   k2buf, v2buf, hbuf, obuf, qs_buf, ks_buf, vs_buf, kst_buf, vst_buf, os_buf,
                   w_kv, w_q, w_o, w_gate, w_up, w_down, ple_gate, ple_proj, staging, sems,
                   *, layer, tiles_per_seq, n_tiles):
    tm = h_ref.shape[0]
    step = pl.program_id(0)
    attn_cost = GROUP * BLOCK / tm
    dense_cost = _chunks_cost(w_o) + _ffn_and_ple_cost(w_gate, w_down, ple_gate, ple_proj)
    per_step = ck_ref.shape[0]
    sample_rows = hs_ref.shape[0] / tm

    def dense_half():
        hp = hbuf[...] + (yield from _dot_cols(obuf[...], w_o))
        hp = yield from _ffn_and_ple_steps(hp, _ffn_norm(hp, g_ffn), p_ref[...], g_ple,
                                           w_gate, w_up, w_down, ple_gate, ple_proj)
        y_out[...] = (hp * _rms_scale(hp)) * g_final[...]

    def sample_qkv():
        q, k, v = yield from _qkv_steps(hs_ref[...], cos_s[...], slo_s[...], shi_s[...],
                                        g_kv, g_mix, w_kv, w_q)
        qs_buf[...] = _regroup_heads(q, _SLOT_SWAP)
        ks_buf[...] = k
        vs_buf[...] = v
        kst_buf[...] = k.T
        vst_buf[...] = v.T

    def sample_decode(j):
        r = (step - 1) * per_step + j
        w = ck_ref.shape[2]
        key = lax.broadcasted_iota(jnp.int32, (KV_WIDTH, w), 1)
        seq_lane = lax.broadcasted_iota(jnp.int32, kst_buf.shape, 1)
        lane = lax.broadcasted_iota(jnp.int32, (N_HEADS, KV_WIDTH), 1)
        r16 = lax.broadcasted_iota(jnp.int32, (N_HEADS, KV_WIDTH), 0)
        slot = r16 // N_KV_HEADS
        kvh = r16 - slot * N_KV_HEADS
        own = (lane >= kvh * HEAD_DIM) & (lane < (kvh + 1) * HEAD_DIM)
        sink = sink_rows[...]
        kn = ks_buf[pl.ds(r, 1), :]
        vn = vs_buf[pl.ds(r, 1), :]
        for win_ref, new_t, out_ref in ((ck_ref, kst_buf, kwins_out), (cv_ref, vst_buf, vwins_out)):
            column = jnp.sum(jnp.where(seq_lane == r, new_t[...], 0.0), axis=1, keepdims=True)
            out_ref[j] = jnp.where(key == w - 1, column, pltpu.roll(win_ref[j], w - 1, 1))
        lhs = jnp.zeros((N_HEADS, KV_WIDTH), _F32)
        for i in range(GROUP):
            q_i = jnp.broadcast_to(qs_buf[pl.ds(r, 1), i * KV_WIDTH:(i + 1) * KV_WIDTH], (N_HEADS, KV_WIDTH))
            lhs = jnp.where(own & (slot == i), q_i, lhs)
        lhs = lhs.astype(_BF16)
        s_old = jnp.dot(lhs, ck_ref[j].astype(_BF16), preferred_element_type=_F32)
        yield 1
        s_new = jnp.sum(lhs.astype(_F32) * kn.astype(_BF16).astype(_F32), axis=-1, keepdims=True)
        m = jnp.maximum(jnp.maximum(jnp.max(s_old, axis=-1, keepdims=True), s_new), sink)
        e_old = jnp.exp(s_old - m)
        e_new = jnp.exp(s_new - m)
        inv = 1.0 / (jnp.sum(e_old, axis=-1, keepdims=True) + e_new + jnp.exp(sink - m))
        yield 1
        o16 = lax.dot_general(e_old.astype(_BF16), cv_ref[j].astype(_BF16), (((1,), (1,)), ((), ())),
                              preferred_element_type=_F32)
        o16 = (o16 + e_new.astype(_BF16).astype(_F32) * vn.astype(_BF16).astype(_F32)) * inv
        for i in range(GROUP):
            picked = jnp.where(own & (slot == i), o16, 0.0)
            os_buf[pl.ds(r, 1), i * KV_WIDTH:(i + 1) * KV_WIDTH] = jnp.sum(picked, axis=0, keepdims=True)
        yield 1

    def sample_decodes():
        for j in range(per_step):
            yield from sample_decode(j)

    def sample_tail():
        yield from sample_decodes()
        o = _regroup_heads(os_buf[...], _SLOT_SWAP)
        h = hs_ref[...] + (yield from _scaled(_dot_cols(o, w_o), sample_rows))
        h = yield from _scaled(_ffn_and_ple_steps(h, _ffn_norm(h, g_ffn), pss_ref[...], g_ple,
                                                  w_gate, w_up, w_down, ple_gate, ple_proj), sample_rows)
        ys_out[...] = (h * _rms_scale(h)) * g_final[...]

    def attention_half():
        h = h_ref[...]
        q, k, v = yield from _qkv_steps(h, cos_ref[...], slo_ref[...], shi_ref[...], g_kv, g_mix, w_kv, w_q)
        qb = q.astype(_BF16)
        kwin_out[...] = k[tm - WINDOW:, :].T
        vwin_out[...] = v[tm - WINDOW:, :].T

        lane = lax.broadcasted_iota(jnp.int32, (1, LANES), 1)
        low = lane < HEAD_DIM
        for arr, buf in ((k, k2buf), (v, v2buf)):
            for s in range(KV_WIDTH // LANES):
                slab = arr[:, s * LANES:(s + 1) * LANES]
                swapped = pltpu.roll(slab, HEAD_DIM, 1)
                buf[2 * s, BLOCK:BLOCK + tm, :] = jnp.where(low, slab, swapped).astype(_BF16)
                buf[2 * s + 1, BLOCK:BLOCK + tm, :] = jnp.where(low, swapped, slab).astype(_BF16)

        row = lax.broadcasted_iota(jnp.int32, (BLOCK, 2 * BLOCK), 0)
        col = lax.broadcasted_iota(jnp.int32, (BLOCK, 2 * BLOCK), 1)
        diff = row + BLOCK - col
        band = (diff >= 0) & (diff <= WINDOW)
        first_col = jnp.where(lax.rem(step, tiles_per_seq) == 0, BLOCK, 0)
        zero = jnp.zeros((), _BF16)
        ones = jnp.ones((2 * BLOCK, LANES), _BF16)
        units = [(jb, g) for jb in range(tm // BLOCK) for g in range(N_KV_HEADS)]

        def scores(jb, g):
            r0 = jb * BLOCK
            parts = []
            for s in (2 * g, 2 * g + 1):
                slab = qb[r0:r0 + BLOCK, s * LANES:(s + 1) * LANES]
                parts.append(jnp.where(low, slab, zero))
                parts.append(jnp.where(low, zero, slab))
            qs = jnp.concatenate(parts, axis=0)
            return lax.dot_general(qs, k2buf[g, r0:r0 + 2 * BLOCK, :],
                                   (((1,), (1,)), ((), ())), preferred_element_type=_F32)

        def weights(jb, g, sc):
            valid = band & (col >= first_col) if jb == 0 else band
            es, tail = [], []
            for i in range(GROUP):
                sink = sinks_ref[GROUP * g + i]
                sp = jnp.where(valid, sc[i * BLOCK:(i + 1) * BLOCK], NEG_INF)
                m = jnp.maximum(jnp.max(sp, axis=-1, keepdims=True), sink)
                es.append(jnp.exp(sp - m).astype(_BF16))
                tail.append(jnp.exp(sink - m))
            return jnp.concatenate(es, axis=0), tail

        def values(jb, g, e, tail):
            r0 = jb * BLOCK
            vv = jnp.concatenate([v2buf[g, r0:r0 + 2 * BLOCK, :], ones], axis=1)
            o3 = jnp.dot(e, vv, preferred_element_type=_F32)
            o2 = []
            for i in range(GROUP):
                piece = o3[i * BLOCK:(i + 1) * BLOCK]
                o2.append(piece[:, :LANES] * (1.0 / (piece[:, LANES:] + tail[i])))
            obuf[r0:r0 + BLOCK, (2 * g) * LANES:(2 * g + 1) * LANES] = (
                jnp.where(low, o2[0], o2[1]).astype(_BF16))
            obuf[r0:r0 + BLOCK, (2 * g + 1) * LANES:(2 * g + 2) * LANES] = (
                jnp.where(low, o2[2], o2[3]).astype(_BF16))

        sc, ew = {}, {}
        for n in range(len(units) + 2):
            if n < len(units):
                sc[n] = scores(*units[n])
                yield attn_cost
            if 0 <= n - 1 < len(units):
                ew[n - 1] = weights(*units[n - 1], sc.pop(n - 1))
                yield 0.0
            if 0 <= n - 2 < len(units):
                values(*units[n - 2], *ew.pop(n - 2))
                yield attn_cost
        hbuf[...] = h
        k2buf[:, 0:BLOCK, :] = k2buf[:, tm:tm + BLOCK, :]
        v2buf[:, 0:BLOCK, :] = v2buf[:, tm:tm + BLOCK, :]

    def body(first_half, second_half):
        lanes = []
        qkv_cost = _chunks_cost(w_kv) + _chunks_cost(w_q)
        if first_half and not second_half:
            _drain(_weight_load_steps([(w_k_hbm, w_kv, 0), (w_v_hbm, w_kv, KV_WIDTH),
                                       (w_q_hbm.at[0], w_q, 0)], staging, sems))
            later = [(w_o_hbm.at[0], w_o, 0),
                     (w_gate_hbm.at[layer], w_gate, 0), (w_up_hbm.at[layer], w_up, 0),
                     (w_down_hbm.at[layer], w_down, 0), (ple_gate_hbm.at[layer], ple_gate, 0),
                     (ple_proj_hbm.at[layer], ple_proj, 0)]
            lanes.append((_weight_load_steps(later, staging, sems), _load_chunks(later), (0.0, 1.0)))
            zeros = jnp.zeros((N_KV_HEADS, BLOCK, LANES), _BF16)
            k2buf[:, 0:BLOCK, :] = zeros
            v2buf[:, 0:BLOCK, :] = zeros
        if second_half:
            dense = dense_half()
            next(dense)
            lanes.append((dense, dense_cost, LAYER1_WINDOWS[1]))
        if first_half:
            n_units = (tm // BLOCK) * N_KV_HEADS
            attention = attention_half()
            next(attention)
            lanes.append((attention, qkv_cost + 2 * n_units * attn_cost, LAYER1_WINDOWS[0]))
        if first_half and not second_half:
            lanes.append((_scaled(sample_qkv(), sample_rows), qkv_cost * sample_rows, (0.0, 1.0)))
        elif first_half:
            lanes.append((sample_decodes(), 3 * per_step, SAMPLE_DECODE_WINDOW))
        else:
            lanes.append((sample_tail(), 3 * per_step + dense_cost * sample_rows, (0.0, 1.0)))
        _interleave(*lanes)

    _pipeline_bodies(step, n_tiles, body)


class _Layer:
    def __init__(self, stacked, layer):
        self.stacked, self.layer = stacked, layer


def _operand(x):
    return x.stacked if isinstance(x, _Layer) else x


def _resident(x):
    if isinstance(x, _Layer):
        rest = x.stacked.shape[1:]
        index = (x.layer,) + (0,) * len(rest)
        return pl.BlockSpec((None,) + rest, lambda *_: index, pipeline_mode=pl.Buffered(1))
    nd = x.ndim
    return pl.BlockSpec(x.shape, lambda *_: (0,) * nd, pipeline_mode=pl.Buffered(1))


def _params(n_grid_dims):
    return pltpu.CompilerParams(dimension_semantics=("arbitrary",) * n_grid_dims,
                                vmem_limit_bytes=VMEM_LIMIT_BYTES)


def _rope_tables(pos):
    half = ROT_DIM // 2
    inv_freq = np.power(np.float64(ROPE_THETA), -np.arange(half, dtype=np.float64) / half)
    ang = pos.astype(np.float64)[:, None] * inv_freq[None, :]
    cos, sin = np.cos(ang), np.sin(ang)
    n = pos.shape[0]
    pad = np.zeros((n, HEAD_DIM - ROT_DIM))
    zero = np.zeros((n, half))
    cos_h = np.concatenate([cos, cos, pad + 1.0], axis=1)
    lo_h = np.concatenate([-sin, zero, pad], axis=1)
    hi_h = np.concatenate([zero, sin, pad], axis=1)
    rep = LANES // HEAD_DIM
    return tuple(jnp.asarray(np.tile(a, (1, rep)), dtype=_F32) for a in (cos_h, lo_h, hi_h))


def kernel(x_prompt, x_sample, state_conv, cache_k_win, cache_v_win, p_prompt, p_sample,
           norm_mix_g, norm_ffn_g, norm_ple_g, kv_norm_g, final_norm_g,
           conv_w_in, conv_w, conv_w_out, w_k, w_v, w_q, sinks, w_o,
           ffn_w_gate, ffn_w_up, ffn_w_down, ple_w_proj, ple_w_gate):
    batch, seq, d = x_prompt.shape
    dec = x_sample.shape[0]
    w_buf = cache_k_win.shape[1]
    tm = PROMPT_TILE
    assert d == D_MODEL and seq % tm == 0 and tm % BLOCK == 0 and tm >= WINDOW
    assert x_sample.shape[1] == 1 and w_buf == WINDOW and dec % (batch * (seq // tm)) == 0

    row = lambda a: a.reshape(1, -1)
    rows = lambda a: a.reshape(a.shape[0], 1, a.shape[1])
    gains = dict(g_mix=rows(norm_mix_g), g_ffn=rows(norm_ffn_g), g_ple=rows(norm_ple_g))
    hbm = pl.BlockSpec(memory_space=pl.ANY)
    ffn_weights = [ffn_w_gate, ffn_w_up, ffn_w_down, ple_w_gate, ple_w_proj]
    staging = [pltpu.VMEM((2, STAGING_ROWS, max(w.shape[-1] for w in [conv_w_in] + ffn_weights)), _F32),
               pltpu.SemaphoreType.DMA((2,))]
    resident_bf16 = lambda ws: [pltpu.VMEM(w.shape[-2:], _BF16) for w in ws]

    nt = seq // tm
    n_tiles = batch * nt
    cur = lambda s: jnp.minimum(s, n_tiles - 1)
    prev = lambda s: jnp.maximum(s - 1, 0)
    cur_tile = pl.BlockSpec((None, tm, d), lambda s: (cur(s) // nt, cur(s) % nt, 0))
    prev_tile = pl.BlockSpec((None, tm, d), lambda s: (prev(s) // nt, prev(s) % nt, 0))
    prev_ple = lambda layer: pl.BlockSpec((None, None, tm, p_prompt.shape[-1]),
                                          lambda s: (layer, prev(s) // nt, prev(s) % nt, 0))
    xs = x_sample.reshape(dec, d)
    ps = p_sample.reshape(p_sample.shape[0], dec, p_sample.shape[-1])
    sample_in = [xs, _Layer(ps, 0), state_conv[0, :, 0, :], state_conv[0, :, 1, :]]
    l0_small = [_Layer(gains[n], 0) for n in ("g_mix", "g_ffn", "g_ple")] + [_Layer(conv_w, 0)]
    l0_weights = [conv_w_in, conv_w_out] + ffn_weights
    sample_rows = pl.BlockSpec((dec, d), lambda s: (0, 0))
    h1, conv_state_prompt, h1s, u_s = pl.pallas_call(
        functools.partial(_layer0_kernel, layer=0, tiles_per_seq=nt, n_tiles=n_tiles),
        grid=(n_tiles + 1,),
        in_specs=[cur_tile, prev_ple(0)] + [_resident(a) for a in sample_in + l0_small]
                 + [hbm] * len(l0_weights),
        out_specs=[prev_tile, pl.BlockSpec((None, CONV_WIDTH - 1, d), lambda s: (cur(s) // nt, 0, 0)),
                   sample_rows, sample_rows],
        out_shape=[jax.ShapeDtypeStruct((batch, seq, d), _F32),
                   jax.ShapeDtypeStruct((batch, CONV_WIDTH - 1, d), _F32),
                   jax.ShapeDtypeStruct((dec, d), _F32),
                   jax.ShapeDtypeStruct((dec, d), _F32)],
        scratch_shapes=[pltpu.VMEM((tm + SUBLANES, d), _F32),
                        pltpu.VMEM((tm, d), _F32),
                        pltpu.VMEM((tm, d), _BF16)] + resident_bf16(l0_weights) + staging,
        compiler_params=_params(1),
        name="layer0",
    )(x_prompt, p_prompt, *map(_operand, sample_in + l0_small), *l0_weights)
    conv_state_sample = jnp.stack([state_conv[0, :, 1, :], u_s], axis=1)[None]

    cos_p, lo_p, hi_p = _rope_tables(np.arange(seq))
    cos_s, lo_s, hi_s = _rope_tables(np.full((dec,), PAST_LEN))
    sink_rows = sinks[0].reshape(N_KV_HEADS, GROUP).T.reshape(N_HEADS, 1)
    windows_t = lambda a: jnp.transpose(a, (0, 2, 3, 1)).reshape(a.shape[0], KV_WIDTH, a.shape[1])
    windows = lambda a: jnp.transpose(a.reshape(a.shape[0], N_KV_HEADS, HEAD_DIM, a.shape[2]), (0, 3, 1, 2))
    ck, cv = windows_t(cache_k_win), windows_t(cache_v_win)
    per_step = dec // n_tiles
    cache = pl.BlockSpec((per_step, KV_WIDTH, w_buf), lambda s: (prev(s), 0, 0))
    sample_l1 = [h1s, _Layer(ps, 1), cos_s, lo_s, hi_s, sink_rows]
    l1_small = [row(kv_norm_g)] + [_Layer(gains[n], 1) for n in ("g_mix", "g_ffn", "g_ple")] + [row(final_norm_g)]
    l1_weights = [w_k, w_v, w_q, w_o] + ffn_weights
    l1_resident = [pltpu.VMEM((d, 2 * KV_WIDTH), _BF16)] + resident_bf16([w_q, w_o] + ffn_weights)
    y_prompt, k_win_prompt, v_win_prompt, y_sample, k_win_sample, v_win_sample = pl.pallas_call(
        functools.partial(_layer1_kernel, layer=1, tiles_per_seq=nt, n_tiles=n_tiles),
        grid=(n_tiles + 1,),
        in_specs=[cur_tile, prev_ple(1)]
                 + [pl.BlockSpec((tm, LANES), lambda s: (cur(s) % nt, 0))] * 3
                 + [pl.BlockSpec(memory_space=pltpu.SMEM)]
                 + [_resident(a) for a in sample_l1] + [cache, cache]
                 + [_resident(a) for a in l1_small] + [hbm] * len(l1_weights),
        out_specs=[prev_tile,
                   pl.BlockSpec((None, KV_WIDTH, WINDOW), lambda s: (cur(s) // nt, 0, 0)),
                   pl.BlockSpec((None, KV_WIDTH, WINDOW), lambda s: (cur(s) // nt, 0, 0)),
                   sample_rows, cache, cache],
        out_shape=[jax.ShapeDtypeStruct((batch, seq, d), _F32),
                   jax.ShapeDtypeStruct((batch, KV_WIDTH, WINDOW), _F32),
                   jax.ShapeDtypeStruct((batch, KV_WIDTH, WINDOW), _F32),
                   jax.ShapeDtypeStruct((dec, d), _F32),
                   jax.ShapeDtypeStruct((dec, KV_WIDTH, w_buf), _F32),
                   jax.ShapeDtypeStruct((dec, KV_WIDTH, w_buf), _F32)],
        scratch_shapes=[pltpu.VMEM((N_KV_HEADS, BLOCK + tm, LANES), _BF16),
                        pltpu.VMEM((N_KV_HEADS, BLOCK + tm, LANES), _BF16),
                        pltpu.VMEM((tm, d), _F32),
                        pltpu.VMEM((tm, d), _BF16),
                        pltpu.VMEM((dec, d), _F32),
                        pltpu.VMEM((dec, KV_WIDTH), _F32),
                        pltpu.VMEM((dec, KV_WIDTH), _F32),
                        pltpu.VMEM((KV_WIDTH, dec), _F32),
                        pltpu.VMEM((KV_WIDTH, dec), _F32),
                        pltpu.VMEM((dec, d), _F32)] + l1_resident + staging,
        compiler_params=_params(1),
        name="layer1",
    )(h1, p_prompt, cos_p, lo_p, hi_p, sinks[0], *map(_operand, sample_l1), ck, cv,
      *map(_operand, l1_small), *l1_weights)

    return (y_prompt, y_sample.reshape(dec, 1, d),
            conv_state_prompt[None], conv_state_sample,
            windows(k_win_prompt), windows(v_win_prompt),
            windows(k_win_sample), windows(v_win_sample))
```

```python
import functools

import jax
import jax.numpy as jnp
import numpy as np
from jax import lax
from jax.experimental import pallas as pl
from jax.experimental.pallas import tpu as pltpu

D_MODEL = 1024
HEAD_DIM = 64
N_HEADS = 16
N_KV_HEADS = 4
GROUP = N_HEADS // N_KV_HEADS
KV_WIDTH = N_KV_HEADS * HEAD_DIM
ROT_DIM = HEAD_DIM // 4
ROPE_THETA = 500000.0
WINDOW = 128
BLOCK = 128
CONV_WIDTH = 3
PAST_LEN = 16384
RMS_EPS = 1e-6
NEG_INF = -1e30

LANES = 128
SUBLANES = 8
MXU_COLS = 256
CHUNK_COLS = 2 * MXU_COLS
VMEM_LIMIT_BYTES = 60000 * 1024

PROMPT_TILE = 256
STAGING_ROWS = 256
STAGING_SLOTS = 3
SAMPLE_DECODE_WINDOW = (0.05, 0.85)
LAYER0_WINDOWS = ((0.10, 0.92), (0.0, 1.0))
LAYER1_WINDOWS = ((0.04, 0.92), (0.0, 1.0))

_BF16 = jnp.bfloat16
_F32 = jnp.float32


def _rms_scale(x):
    return lax.rsqrt(jnp.mean(x * x, axis=-1, keepdims=True) + RMS_EPS)


def _sigmoid(x):
    return 1.0 / (1.0 + jnp.exp(-x))


def _rope(x, cos, sin_lo, sin_hi):
    half = ROT_DIM // 2
    out = []
    for c in range(x.shape[1] // LANES):
        slab = x[:, c * LANES:(c + 1) * LANES]
        out.append(slab * cos
                   + pltpu.roll(slab, LANES - half, 1) * sin_lo
                   + pltpu.roll(slab, half, 1) * sin_hi)
    return jnp.concatenate(out, axis=1)


def _interleave(*lanes):
    done = [0.0] * len(lanes)
    live = list(range(len(lanes)))

    def position(j):
        _, total, (start, end) = lanes[j]
        return start + (end - start) * done[j] / total

    while live:
        i = min(live, key=position)
        try:
            done[i] += next(lanes[i][0])
        except StopIteration:
            live.remove(i)


def _scaled(steps, factor):
    try:
        while True:
            yield next(steps) * factor
    except StopIteration as done:
        return done.value


def _chunks(n_cols):
    return [(c, min(CHUNK_COLS, n_cols - c)) for c in range(0, n_cols, CHUNK_COLS)]


def _col_dot(a, w_ref, c0, width):
    return jnp.dot(a, w_ref[:, c0:c0 + width], preferred_element_type=_F32)


def _dot_cost(w_ref, width):
    return -(-w_ref.shape[0] // MXU_COLS) * (width // MXU_COLS)


def _chunks_cost(w_ref):
    return _dot_cost(w_ref, w_ref.shape[1])


def _dot_cols(a, w_ref):
    a = a.astype(_BF16)
    cols = []
    for c, width in _chunks(w_ref.shape[1]):
        cols.append(_col_dot(a, w_ref, c, width))
        yield _dot_cost(w_ref, width)
    return jnp.concatenate(cols, axis=1)


def _ffn_norm(h, g_ffn):
    return ((h * _rms_scale(h)) * g_ffn[...]).astype(_BF16)


def _ffn_and_ple_steps(h, hf, p, g_ple, w_gate, w_up, w_down, ple_gate, ple_proj):
    acts = []
    for c, width in _chunks(w_gate.shape[1]):
        g = _col_dot(hf, w_gate, c, width)
        yield _dot_cost(w_gate, width)
        u = _col_dot(hf, w_up, c, width)
        yield _dot_cost(w_up, width)
        acts.append(((g * _sigmoid(g)) * u).astype(_BF16))
    h = h + (yield from _dot_cols(jnp.concatenate(acts, axis=1), w_down))
    hp = (h * _rms_scale(h)) * g_ple[...]
    gate = _sigmoid((yield from _dot_cols(hp, ple_gate)))
    return h + gate * (yield from _dot_cols(p, ple_proj))


def _ffn_and_ple_cost(w_gate, w_down, ple_gate, ple_proj):
    return 2 * _chunks_cost(w_gate) + _chunks_cost(w_down) + _chunks_cost(ple_gate) + _chunks_cost(ple_proj)


def _conv_mixer_steps(x, g_mix, w_in, conv_w, w_out, history):
    hn = ((x * _rms_scale(x)) * g_mix[...]).astype(_BF16)
    yield 0
    us, convs = [], []
    for c, width in _chunks(D_MODEL):
        cols = slice(c, c + width)
        c_gate = _col_dot(hn, w_in, D_MODEL + c, width)
        yield _dot_cost(w_in, width)
        xin = _col_dot(hn, w_in, 2 * D_MODEL + c, width)
        yield _dot_cost(w_in, width)
        u = c_gate * xin
        back2, back1 = history(cols, u)
        conv = conv_w[0:1, cols] * back2
        conv = conv + conv_w[1:2, cols] * back1
        conv = conv + conv_w[2:3, cols] * u
        us.append(u)
        convs.append(conv)
    gated = []
    for j, (c, width) in enumerate(_chunks(D_MODEL)):
        b_gate = _col_dot(hn, w_in, c, width)
        yield _dot_cost(w_in, width)
        gated.append((b_gate * convs[j]).astype(_BF16))
    y = yield from _dot_cols(jnp.concatenate(gated, axis=1), w_out)
    return x + y, jnp.concatenate(us, axis=1)


def _conv_mixer_cost(w_in, w_out):
    return _chunks_cost(w_in) + _chunks_cost(w_out)


def _pipeline_bodies(step, n_tiles, body):
    pl.when(step == 0)(lambda: body(True, False))
    pl.when((step > 0) & (step < n_tiles))(lambda: body(True, True))
    pl.when(step == n_tiles)(lambda: body(False, True))


def _drain(steps):
    for _ in steps:
        pass


def _weight_load_steps(plan, staging, sems):
    chunks = [(src, dst, c0, r0, min(STAGING_ROWS, src.shape[0] - r0))
              for src, dst, c0 in plan for r0 in range(0, src.shape[0], STAGING_ROWS)]
    slots = staging.shape[0]

    def copy(i):
        src, _, _, r0, n = chunks[i]
        window = staging.at[i % slots, pl.ds(0, n), pl.ds(0, src.shape[1])]
        return pltpu.make_async_copy(src.at[pl.ds(r0, n), :], window, sems.at[i % slots])

    for i in range(min(slots - 1, len(chunks))):
        copy(i).start(priority=i % 2)
    for i, (src, dst, c0, r0, n) in enumerate(chunks):
        ahead = i + slots - 1
        if ahead < len(chunks):
            copy(ahead).start(priority=ahead % 2)
        copy(i).wait()
        cols = src.shape[1]
        dst[r0:r0 + n, c0:c0 + cols] = staging[i % slots, 0:n, 0:cols].astype(_BF16)
        yield 1


def _load_chunks(plan):
    return sum(-(-src.shape[0] // STAGING_ROWS) for src, _, _ in plan)


def _layer0_kernel(x_ref, p_ref, xs_ref, ps_ref, s0_ref, s1_ref, g_mix, g_ffn, g_ple, conv_w,
                   w_in_hbm, w_out_hbm, w_gate_hbm, w_up_hbm, w_down_hbm, ple_gate_hbm, ple_proj_hbm,
                   h_out, state_out, hs_out, us_out,
                   ubuf, hbuf, hfbuf, w_in, w_out, w_gate, w_up, w_down, ple_gate, ple_proj, staging, sems,
                   *, layer, tiles_per_seq, n_tiles):
    tm = x_ref.shape[0]
    step = pl.program_id(0)
    ffn_cost = _ffn_and_ple_cost(w_gate, w_down, ple_gate, ple_proj)
    conv_cost = _conv_mixer_cost(w_in, w_out)

    def history(cols, u):
        ubuf[SUBLANES:SUBLANES + tm, cols] = u
        return (ubuf[SUBLANES - 2:SUBLANES - 2 + tm, cols], ubuf[SUBLANES - 1:SUBLANES - 1 + tm, cols])

    def ffn_half():
        out = yield from _ffn_and_ple_steps(hbuf[...], hfbuf[...], p_ref[...], g_ple,
                                            w_gate, w_up, w_down, ple_gate, ple_proj)
        h_out[...] = out

    def conv_half():
        h, _ = yield from _conv_mixer_steps(x_ref[...], g_mix, w_in, conv_w, w_out, history)
        state_out[...] = ubuf[SUBLANES + tm - (CONV_WIDTH - 1):SUBLANES + tm, :]
        ubuf[0:SUBLANES, :] = ubuf[tm:tm + SUBLANES, :]
        hbuf[...] = h
        hfbuf[...] = _ffn_norm(h, g_ffn)

    def sample_layer():
        history = lambda cols, u: (s0_ref[:, cols], s1_ref[:, cols])
        h, u = yield from _conv_mixer_steps(xs_ref[...], g_mix, w_in, conv_w, w_out, history)
        out = yield from _ffn_and_ple_steps(h, _ffn_norm(h, g_ffn), ps_ref[...], g_ple,
                                            w_gate, w_up, w_down, ple_gate, ple_proj)
        hs_out[...] = out
        us_out[...] = u

    def body(first_half, second_half):
        lanes = []
        if first_half and not second_half:
            _drain(_weight_load_steps([(w_in_hbm.at[0], w_in, 0), (w_out_hbm.at[0], w_out, 0)], staging, sems))
            later = [(w_gate_hbm.at[layer], w_gate, 0), (w_up_hbm.at[layer], w_up, 0),
                     (w_down_hbm.at[layer], w_down, 0), (ple_gate_hbm.at[layer], ple_gate, 0),
                     (ple_proj_hbm.at[layer], ple_proj, 0)]
            lanes.append((_weight_load_steps(later, staging, sems), _load_chunks(later), (0.0, 1.0)))
        if first_half:
            zeros = jnp.zeros((SUBLANES, D_MODEL), _F32)
            fresh = lax.rem(step, tiles_per_seq) == 0
            ubuf[0:SUBLANES, :] = jnp.where(fresh, zeros, ubuf[0:SUBLANES, :]) if second_half else zeros
        if second_half:
            ffn = ffn_half()
            next(ffn)
            lanes.append((ffn, ffn_cost, LAYER0_WINDOWS[1]))
        if first_half:
            conv = conv_half()
            next(conv)
            lanes.append((conv, conv_cost, LAYER0_WINDOWS[0]))
        else:
            rows = xs_ref.shape[0] / tm
            lanes.append((_scaled(sample_layer(), rows), (conv_cost + ffn_cost) * rows, (0.0, 1.0)))
        _interleave(*lanes)

    _pipeline_bodies(step, n_tiles, body)


def _qkv_steps(h, cos, sin_lo, sin_hi, g_kv, g_mix, w_kv, w_q):
    hs = h * _rms_scale(h)
    h_kv = (hs * g_kv[...]).astype(_BF16)
    h_q = (hs * g_mix[...]).astype(_BF16)
    yield 0
    kv = yield from _dot_cols(h_kv, w_kv)
    q = yield from _dot_cols(h_q, w_q)
    k = _rope(kv[:, :KV_WIDTH], cos, sin_lo, sin_hi)
    v = kv[:, KV_WIDTH:]
    q = _rope(q, cos, sin_lo, sin_hi) * (HEAD_DIM ** -0.5)
    return q, k, v


def _regroup_heads(x, order):
    low = lax.broadcasted_iota(jnp.int32, (1, LANES), 1) < HEAD_DIM
    out = []
    for s in range(x.shape[1] // LANES):
        halves = []
        for half in range(2):
            src_head = order[2 * s + half]
            slab = x[:, (src_head // 2) * LANES:(src_head // 2 + 1) * LANES]
            halves.append(slab if src_head % 2 == half else pltpu.roll(slab, HEAD_DIM, 1))
        out.append(jnp.where(low, halves[0], halves[1]))
    return jnp.concatenate(out, axis=1)


_SLOT_SWAP = [GROUP * (p % N_KV_HEADS) + p // N_KV_HEADS for p in range(N_HEADS)]


def _layer1_kernel(h_ref, p_ref, cos_ref, slo_ref, shi_ref, sinks_ref,
                   hs_ref, pss_ref, cos_s, slo_s, shi_s, sink_rows, ck_ref, cv_ref,
                   g_kv, g_mix, g_ffn, g_ple, g_final,
                   w_k_hbm, w_v_hbm, w_q_hbm, w_o_hbm, w_gate_hbm, w_up_hbm, w_down_hbm,
                   ple_gate_hbm, ple_proj_hbm,
                   y_out, kwin_out, vwin_out, ys_out, kwins_out, vwins_out,
                   k2buf, v2buf, hbuf, obuf, qs_buf, ks_buf, vs_buf, kst_buf, vst_buf, os_buf,
                   w_kv, w_q, w_o, w_gate, w_up, w_down, ple_gate, ple_proj, staging, sems,
                   *, layer, tiles_per_seq, n_tiles):
    tm = h_ref.shape[0]
    step = pl.program_id(0)
    attn_cost = GROUP * BLOCK / tm
    dense_cost = _chunks_cost(w_o) + _ffn_and_ple_cost(w_gate, w_down, ple_gate, ple_proj)
    per_step = ck_ref.shape[0]
    sample_rows = hs_ref.shape[0] / tm

    def dense_half():
        hp = hbuf[...] + (yield from _dot_cols(obuf[...], w_o))
        hp = yield from _ffn_and_ple_steps(hp, _ffn_norm(hp, g_ffn), p_ref[...], g_ple,
                                           w_gate, w_up, w_down, ple_gate, ple_proj)
        y_out[...] = (hp * _rms_scale(hp)) * g_final[...]

    def sample_qkv():
        q, k, v = yield from _qkv_steps(hs_ref[...], cos_s[...], slo_s[...], shi_s[...],
                                        g_kv, g_mix, w_kv, w_q)
        qs_buf[...] = _regroup_heads(q, _SLOT_SWAP)
        ks_buf[...] = k
        vs_buf[...] = v
        kst_buf[...] = k.T
        vst_buf[...] = v.T

    def sample_decode(j):
        r = (step - 1) * per_step + j
        w = ck_ref.shape[2]
        key = lax.broadcasted_iota(jnp.int32, (KV_WIDTH, w), 1)
        seq_lane = lax.broadcasted_iota(jnp.int32, kst_buf.shape, 1)
        lane = lax.broadcasted_iota(jnp.int32, (N_HEADS, KV_WIDTH), 1)
        r16 = lax.broadcasted_iota(jnp.int32, (N_HEADS, KV_WIDTH), 0)
        slot = r16 // N_KV_HEADS
        kvh = r16 - slot * N_KV_HEADS
        own = (lane >= kvh * HEAD_DIM) & (lane < (kvh + 1) * HEAD_DIM)
        sink = sink_rows[...]
        kn = ks_buf[pl.ds(r, 1), :]
        vn = vs_buf[pl.ds(r, 1), :]
        for win_ref, new_t, out_ref in ((ck_ref, kst_buf, kwins_out), (cv_ref, vst_buf, vwins_out)):
            column = jnp.sum(jnp.where(seq_lane == r, new_t[...], 0.0), axis=1, keepdims=True)
            out_ref[j] = jnp.where(key == w - 1, column, pltpu.roll(win_ref[j], w - 1, 1))
        lhs = jnp.zeros((N_HEADS, KV_WIDTH), _F32)
        for i in range(GROUP):
            q_i = jnp.broadcast_to(qs_buf[pl.ds(r, 1), i * KV_WIDTH:(i + 1) * KV_WIDTH], (N_HEADS, KV_WIDTH))
            lhs = jnp.where(own & (slot == i), q_i, lhs)
        lhs = lhs.astype(_BF16)
        s_old = jnp.dot(lhs, ck_ref[j].astype(_BF16), preferred_element_type=_F32)
        yield 1
        s_new = jnp.sum(lhs.astype(_F32) * kn.astype(_BF16).astype(_F32), axis=-1, keepdims=True)
        m = jnp.maximum(jnp.maximum(jnp.max(s_old, axis=-1, keepdims=True), s_new), sink)
        e_old = jnp.exp(s_old - m)
        e_new = jnp.exp(s_new - m)
        inv = 1.0 / (jnp.sum(e_old, axis=-1, keepdims=True) + e_new + jnp.exp(sink - m))
        yield 1
        o16 = lax.dot_general(e_old.astype(_BF16), cv_ref[j].astype(_BF16), (((1,), (1,)), ((), ())),
                              preferred_element_type=_F32)
        o16 = (o16 + e_new.astype(_BF16).astype(_F32) * vn.astype(_BF16).astype(_F32)) * inv
        for i in range(GROUP):
            picked = jnp.where(own & (slot == i), o16, 0.0)
            os_buf[pl.ds(r, 1), i * KV_WIDTH:(i + 1) * KV_WIDTH] = jnp.sum(picked, axis=0, keepdims=True)
        yield 1

    def sample_decodes():
        for j in range(per_step):
            yield from sample_decode(j)

    def sample_tail():
        yield from sample_decodes()
        o = _regroup_heads(os_buf[...], _SLOT_SWAP)
        h = hs_ref[...] + (yield from _scaled(_dot_cols(o, w_o), sample_rows))
        h = yield from _scaled(_ffn_and_ple_steps(h, _ffn_norm(h, g_ffn), pss_ref[...], g_ple,
                                                  w_gate, w_up, w_down, ple_gate, ple_proj), sample_rows)
        ys_out[...] = (h * _rms_scale(h)) * g_final[...]

    def attention_half():
        h = h_ref[...]
        q, k, v = yield from _qkv_steps(h, cos_ref[...], slo_ref[...], shi_ref[...], g_kv, g_mix, w_kv, w_q)
        qb = q.astype(_BF16)
        kwin_out[...] = k[tm - WINDOW:, :].T
        vwin_out[...] = v[tm - WINDOW:, :].T

        lane = lax.broadcasted_iota(jnp.int32, (1, LANES), 1)
        low = lane < HEAD_DIM
        for arr, buf in ((k, k2buf), (v, v2buf)):
            for s in range(KV_WIDTH // LANES):
                slab = arr[:, s * LANES:(s + 1) * LANES]
                swapped = pltpu.roll(slab, HEAD_DIM, 1)
                buf[2 * s, BLOCK:BLOCK + tm, :] = jnp.where(low, slab, swapped).astype(_BF16)
                buf[2 * s + 1, BLOCK:BLOCK + tm, :] = jnp.where(low, swapped, slab).astype(_BF16)

        row = lax.broadcasted_iota(jnp.int32, (BLOCK, 2 * BLOCK), 0)
        col = lax.broadcasted_iota(jnp.int32, (BLOCK, 2 * BLOCK), 1)
        diff = row + BLOCK - col
        band = (diff >= 0) & (diff <= WINDOW)
        first_col = jnp.where(lax.rem(step, tiles_per_seq) == 0, BLOCK, 0)
        zero = jnp.zeros((), _BF16)
        ones = jnp.ones((2 * BLOCK, LANES), _BF16)
        units = [(jb, g) for jb in range(tm // BLOCK) for g in range(N_KV_HEADS)]

        def scores(jb, g):
            r0 = jb * BLOCK
            parts = []
            for s in (2 * g, 2 * g + 1):
                slab = qb[r0:r0 + BLOCK, s * LANES:(s + 1) * LANES]
                parts.append(jnp.where(low, slab, zero))
                parts.append(jnp.where(low, zero, slab))
            qs = jnp.concatenate(parts, axis=0)
            return lax.dot_general(qs, k2buf[g, r0:r0 + 2 * BLOCK, :],
                                   (((1,), (1,)), ((), ())), preferred_element_type=_F32)

        def weights(jb, g, sc):
            valid = band & (col >= first_col) if jb == 0 else band
            es, tail = [], []
            for i in range(GROUP):
                sink = sinks_ref[GROUP * g + i]
                sp = jnp.where(valid, sc[i * BLOCK:(i + 1) * BLOCK], NEG_INF)
                m = jnp.maximum(jnp.max(sp, axis=-1, keepdims=True), sink)
                es.append(jnp.exp(sp - m).astype(_BF16))
                tail.append(jnp.exp(sink - m))
            return jnp.concatenate(es, axis=0), tail

        def values(jb, g, e, tail):
            r0 = jb * BLOCK
            vv = jnp.concatenate([v2buf[g, r0:r0 + 2 * BLOCK, :], ones], axis=1)
            o3 = jnp.dot(e, vv, preferred_element_type=_F32)
            o2 = []
            for i in range(GROUP):
                piece = o3[i * BLOCK:(i + 1) * BLOCK]
                o2.append(piece[:, :LANES] * (1.0 / (piece[:, LANES:] + tail[i])))
            obuf[r0:r0 + BLOCK, (2 * g) * LANES:(2 * g + 1) * LANES] = (
                jnp.where(low, o2[0], o2[1]).astype(_BF16))
            obuf[r0:r0 + BLOCK, (2 * g + 1) * LANES:(2 * g + 2) * LANES] = (
                jnp.where(low, o2[2], o2[3]).astype(_BF16))

        sc, ew = {}, {}
        for n in range(len(units) + 2):
            if n < len(units):
                sc[n] = scores(*units[n])
                yield attn_cost
            if 0 <= n - 1 < len(units):
                ew[n - 1] = weights(*units[n - 1], sc.pop(n - 1))
                yield 0.0
            if 0 <= n - 2 < len(units):
                values(*units[n - 2], *ew.pop(n - 2))
                yield attn_cost
        hbuf[...] = h
        k2buf[:, 0:BLOCK, :] = k2buf[:, tm:tm + BLOCK, :]
        v2buf[:, 0:BLOCK, :] = v2buf[:, tm:tm + BLOCK, :]

    def body(first_half, second_half):
        lanes = []
        qkv_cost = _chunks_cost(w_kv) + _chunks_cost(w_q)
        if first_half and not second_half:
            _drain(_weight_load_steps([(w_k_hbm, w_kv, 0), (w_v_hbm, w_kv, KV_WIDTH),
                                       (w_q_hbm.at[0], w_q, 0)], staging, sems))
            later = [(w_o_hbm.at[0], w_o, 0),
                     (w_gate_hbm.at[layer], w_gate, 0), (w_up_hbm.at[layer], w_up, 0),
                     (w_down_hbm.at[layer], w_down, 0), (ple_gate_hbm.at[layer], ple_gate, 0),
                     (ple_proj_hbm.at[layer], ple_proj, 0)]
            lanes.append((_weight_load_steps(later, staging, sems), _load_chunks(later), (0.0, 1.0)))
            zeros = jnp.zeros((N_KV_HEADS, BLOCK, LANES), _BF16)
            k2buf[:, 0:BLOCK, :] = zeros
            v2buf[:, 0:BLOCK, :] = zeros
        if second_half:
            dense = dense_half()
            next(dense)
            lanes.append((dense, dense_cost, LAYER1_WINDOWS[1]))
        if first_half:
            n_units = (tm // BLOCK) * N_KV_HEADS
            attention = attention_half()
            next(attention)
            lanes.append((attention, qkv_cost + 2 * n_units * attn_cost, LAYER1_WINDOWS[0]))
        if first_half and not second_half:
            lanes.append((_scaled(sample_qkv(), sample_rows), qkv_cost * sample_rows, (0.0, 1.0)))
        elif first_half:
            lanes.append((sample_decodes(), 3 * per_step, SAMPLE_DECODE_WINDOW))
        else:
            lanes.append((sample_tail(), 3 * per_step + dense_cost * sample_rows, (0.0, 1.0)))
        _interleave(*lanes)

    _pipeline_bodies(step, n_tiles, body)


class _Layer:
    def __init__(self, stacked, layer):
        self.stacked, self.layer = stacked, layer


def _operand(x):
    return x.stacked if isinstance(x, _Layer) else x


def _resident(x):
    if isinstance(x, _Layer):
        rest = x.stacked.shape[1:]
        index = (x.layer,) + (0,) * len(rest)
        return pl.BlockSpec((None,) + rest, lambda *_: index, pipeline_mode=pl.Buffered(1))
    nd = x.ndim
    return pl.BlockSpec(x.shape, lambda *_: (0,) * nd, pipeline_mode=pl.Buffered(1))


def _params(n_grid_dims):
    return pltpu.CompilerParams(dimension_semantics=("arbitrary",) * n_grid_dims,
                                vmem_limit_bytes=VMEM_LIMIT_BYTES)


def _rope_tables(pos):
    half = ROT_DIM // 2
    inv_freq = np.power(np.float64(ROPE_THETA), -np.arange(half, dtype=np.float64) / half)
    ang = pos.astype(np.float64)[:, None] * inv_freq[None, :]
    cos, sin = np.cos(ang), np.sin(ang)
    n = pos.shape[0]
    pad = np.zeros((n, HEAD_DIM - ROT_DIM))
    zero = np.zeros((n, half))
    cos_h = np.concatenate([cos, cos, pad + 1.0], axis=1)
    lo_h = np.concatenate([-sin, zero, pad], axis=1)
    hi_h = np.concatenate([zero, sin, pad], axis=1)
    rep = LANES // HEAD_DIM
    return tuple(jnp.asarray(np.tile(a, (1, rep)), dtype=_F32) for a in (cos_h, lo_h, hi_h))


def kernel(x_prompt, x_sample, state_conv, cache_k_win, cache_v_win, p_prompt, p_sample,
           norm_mix_g, norm_ffn_g, norm_ple_g, kv_norm_g, final_norm_g,
           conv_w_in, conv_w, conv_w_out, w_k, w_v, w_q, sinks, w_o,
           ffn_w_gate, ffn_w_up, ffn_w_down, ple_w_proj, ple_w_gate):
    batch, seq, d = x_prompt.shape
    dec = x_sample.shape[0]
    w_buf = cache_k_win.shape[1]
    tm = PROMPT_TILE
    assert d == D_MODEL and seq % tm == 0 and tm % BLOCK == 0 and tm >= WINDOW
    assert x_sample.shape[1] == 1 and w_buf == WINDOW and dec % (batch * (seq // tm)) == 0

    row = lambda a: a.reshape(1, -1)
    rows = lambda a: a.reshape(a.shape[0], 1, a.shape[1])
    gains = dict(g_mix=rows(norm_mix_g), g_ffn=rows(norm_ffn_g), g_ple=rows(norm_ple_g))
    hbm = pl.BlockSpec(memory_space=pl.ANY)
    ffn_weights = [ffn_w_gate, ffn_w_up, ffn_w_down, ple_w_gate, ple_w_proj]
    staging = [pltpu.VMEM((STAGING_SLOTS, STAGING_ROWS, max(w.shape[-1] for w in [conv_w_in] + ffn_weights)), _F32),
               pltpu.SemaphoreType.DMA((STAGING_SLOTS,))]
    resident_bf16 = lambda ws: [pltpu.VMEM(w.shape[-2:], _BF16) for w in ws]

    nt = seq // tm
    n_tiles = batch * nt
    cur = lambda s: jnp.minimum(s, n_tiles - 1)
    prev = lambda s: jnp.maximum(s - 1, 0)
    cur_tile = pl.BlockSpec((None, tm, d), lambda s: (cur(s) // nt, cur(s) % nt, 0))
    prev_tile = pl.BlockSpec((None, tm, d), lambda s: (prev(s) // nt, prev(s) % nt, 0))
    prev_ple = lambda layer: pl.BlockSpec((None, None, tm, p_prompt.shape[-1]),
                                          lambda s: (layer, prev(s) // nt, prev(s) % nt, 0))
    xs = x_sample.reshape(dec, d)
    ps = p_sample.reshape(p_sample.shape[0], dec, p_sample.shape[-1])
    sample_in = [xs, _Layer(ps, 0), state_conv[0, :, 0, :], state_conv[0, :, 1, :]]
    l0_small = [_Layer(gains[n], 0) for n in ("g_mix", "g_ffn", "g_ple")] + [_Layer(conv_w, 0)]
    l0_weights = [conv_w_in, conv_w_out] + ffn_weights
    sample_rows = pl.BlockSpec((dec, d), lambda s: (0, 0))
    h1, conv_state_prompt, h1s, u_s = pl.pallas_call(
        functools.partial(_layer0_kernel, layer=0, tiles_per_seq=nt, n_tiles=n_tiles),
        grid=(n_tiles + 1,),
        in_specs=[cur_tile, prev_ple(0)] + [_resident(a) for a in sample_in + l0_small]
                 + [hbm] * len(l0_weights),
        out_specs=[prev_tile, pl.BlockSpec((None, CONV_WIDTH - 1, d), lambda s: (cur(s) // nt, 0, 0)),
                   sample_rows, sample_rows],
        out_shape=[jax.ShapeDtypeStruct((batch, seq, d), _F32),
                   jax.ShapeDtypeStruct((batch, CONV_WIDTH - 1, d), _F32),
                   jax.ShapeDtypeStruct((dec, d), _F32),
                   jax.ShapeDtypeStruct((dec, d), _F32)],
        scratch_shapes=[pltpu.VMEM((tm + SUBLANES, d), _F32),
                        pltpu.VMEM((tm, d), _F32),
                        pltpu.VMEM((tm, d), _BF16)] + resident_bf16(l0_weights) + staging,
        compiler_params=_params(1),
        name="layer0",
    )(x_prompt, p_prompt, *map(_operand, sample_in + l0_small), *l0_weights)
    conv_state_sample = jnp.stack([state_conv[0, :, 1, :], u_s], axis=1)[None]

    cos_p, lo_p, hi_p = _rope_tables(np.arange(seq))
    cos_s, lo_s, hi_s = _rope_tables(np.full((dec,), PAST_LEN))
    sink_rows = sinks[0].reshape(N_KV_HEADS, GROUP).T.reshape(N_HEADS, 1)
    windows_t = lambda a: jnp.transpose(a, (0, 2, 3, 1)).reshape(a.shape[0], KV_WIDTH, a.shape[1])
    windows = lambda a: jnp.transpose(a.reshape(a.shape[0], N_KV_HEADS, HEAD_DIM, a.shape[2]), (0, 3, 1, 2))
    ck, cv = windows_t(cache_k_win), windows_t(cache_v_win)
    per_step = dec // n_tiles
    cache = pl.BlockSpec((per_step, KV_WIDTH, w_buf), lambda s: (prev(s), 0, 0))
    sample_l1 = [h1s, _Layer(ps, 1), cos_s, lo_s, hi_s, sink_rows]
    l1_small = [row(kv_norm_g)] + [_Layer(gains[n], 1) for n in ("g_mix", "g_ffn", "g_ple")] + [row(final_norm_g)]
    l1_weights = [w_k, w_v, w_q, w_o] + ffn_weights
    l1_resident = [pltpu.VMEM((d, 2 * KV_WIDTH), _BF16)] + resident_bf16([w_q, w_o] + ffn_weights)
    y_prompt, k_win_prompt, v_win_prompt, y_sample, k_win_sample, v_win_sample = pl.pallas_call(
        functools.partial(_layer1_kernel, layer=1, tiles_per_seq=nt, n_tiles=n_tiles),
        grid=(n_tiles + 1,),
        in_specs=[cur_tile, prev_ple(1)]
                 + [pl.BlockSpec((tm, LANES), lambda s: (cur(s) % nt, 0))] * 3
                 + [pl.BlockSpec(memory_space=pltpu.SMEM)]
                 + [_resident(a) for a in sample_l1] + [cache, cache]
                 + [_resident(a) for a in l1_small] + [hbm] * len(l1_weights),
        out_specs=[prev_tile,
                   pl.BlockSpec((None, KV_WIDTH, WINDOW), lambda s: (cur(s) // nt, 0, 0)),
                   pl.BlockSpec((None, KV_WIDTH, WINDOW), lambda s: (cur(s) // nt, 0, 0)),
                   sample_rows, cache, cache],
        out_shape=[jax.ShapeDtypeStruct((batch, seq, d), _F32),
                   jax.ShapeDtypeStruct((batch, KV_WIDTH, WINDOW), _F32),
                   jax.ShapeDtypeStruct((batch, KV_WIDTH, WINDOW), _F32),
                   jax.ShapeDtypeStruct((dec, d), _F32),
                   jax.ShapeDtypeStruct((dec, KV_WIDTH, w_buf), _F32),
                   jax.ShapeDtypeStruct((dec, KV_WIDTH, w_buf), _F32)],
        scratch_shapes=[pltpu.VMEM((N_KV_HEADS, BLOCK + tm, LANES), _BF16),
                        pltpu.VMEM((N_KV_HEADS, BLOCK + tm, LANES), _BF16),
                        pltpu.VMEM((tm, d), _F32),
                        pltpu.VMEM((tm, d), _BF16),
                        pltpu.VMEM((dec, d), _F32),
                        pltpu.VMEM((dec, KV_WIDTH), _F32),
                        pltpu.VMEM((dec, KV_WIDTH), _F32),
                        pltpu.VMEM((KV_WIDTH, dec), _F32),
                        pltpu.VMEM((KV_WIDTH, dec), _F32),
                        pltpu.VMEM((dec, d), _F32)] + l1_resident + staging,
        compiler_params=_params(1),
        name="layer1",
    )(h1, p_prompt, cos_p, lo_p, hi_p, sinks[0], *map(_operand, sample_l1), ck, cv,
      *map(_operand, l1_small), *l1_weights)

    return (y_prompt, y_sample.reshape(dec, 1, d),
            conv_state_prompt[None], conv_state_sample,
            windows(k_win_prompt), windows(v_win_prompt),
            windows(k_win_sample), windows(v_win_sample))
```

```python
import functools

import jax
import jax.numpy as jnp
import numpy as np
from jax import lax
from jax.experimental import pallas as pl
from jax.experimental.pallas import tpu as pltpu

D_MODEL = 1024
HEAD_DIM = 64
N_HEADS = 16
N_KV_HEADS = 4
GROUP = N_HEADS // N_KV_HEADS
KV_WIDTH = N_KV_HEADS * HEAD_DIM
ROT_DIM = HEAD_DIM // 4
ROPE_THETA = 500000.0
WINDOW = 128
BLOCK = 128
CONV_WIDTH = 3
PAST_LEN = 16384
RMS_EPS = 1e-6
NEG_INF = -1e30

LANES = 128
SUBLANES = 8
MXU_COLS = 256
CHUNK_COLS = 2 * MXU_COLS
VMEM_LIMIT_BYTES = 60000 * 1024

PROMPT_TILE = 256
STAGING_ROWS = 256
STAGING_SLOTS = 3
SAMPLE_DECODE_WINDOW = (0.05, 0.85)
LAYER0_WINDOWS = ((0.10, 0.92), (0.0, 1.0))
LAYER1_WINDOWS = ((0.04, 0.92), (0.0, 1.0))

_BF16 = jnp.bfloat16
_F32 = jnp.float32


def _rms_scale(x):
    return lax.rsqrt(jnp.mean(x * x, axis=-1, keepdims=True) + RMS_EPS)


def _sigmoid(x):
    return 1.0 / (1.0 + jnp.exp(-x))


def _rope(x, cos, sin_lo, sin_hi):
    half = ROT_DIM // 2
    out = []
    for c in range(x.shape[1] // LANES):
        slab = x[:, c * LANES:(c + 1) * LANES]
        out.append(slab * cos
                   + pltpu.roll(slab, LANES - half, 1) * sin_lo
                   + pltpu.roll(slab, half, 1) * sin_hi)
    return jnp.concatenate(out, axis=1)


def _interleave(*lanes):
    done = [0.0] * len(lanes)
    live = list(range(len(lanes)))

    def position(j):
        _, total, (start, end) = lanes[j]
        return start + (end - start) * done[j] / total

    while live:
        i = min(live, key=position)
        try:
            done[i] += next(lanes[i][0])
        except StopIteration:
            live.remove(i)


def _scaled(steps, factor):
    try:
        while True:
            yield next(steps) * factor
    except StopIteration as done:
        return done.value


def _chunks(n_cols):
    return [(c, min(CHUNK_COLS, n_cols - c)) for c in range(0, n_cols, CHUNK_COLS)]


def _col_dot(a, w_ref, c0, width):
    return jnp.dot(a, w_ref[:, c0:c0 + width], preferred_element_type=_F32)


def _dot_cost(w_ref, width):
    return -(-w_ref.shape[0] // MXU_COLS) * (width // MXU_COLS)


def _chunks_cost(w_ref):
    return _dot_cost(w_ref, w_ref.shape[1])


def _dot_cols(a, w_ref):
    a = a.astype(_BF16)
    cols = []
    for c, width in _chunks(w_ref.shape[1]):
        cols.append(_col_dot(a, w_ref, c, width))
        yield _dot_cost(w_ref, width)
    return jnp.concatenate(cols, axis=1)


def _ffn_norm(h, g_ffn):
    return ((h * _rms_scale(h)) * g_ffn[...]).astype(_BF16)


def _ffn_and_ple_steps(h, hf, p, g_ple, w_gate, w_up, w_down, ple_gate, ple_proj):
    acts = []
    for c, width in _chunks(w_gate.shape[1]):
        g = _col_dot(hf, w_gate, c, width)
        yield _dot_cost(w_gate, width)
        u = _col_dot(hf, w_up, c, width)
        yield _dot_cost(w_up, width)
        acts.append(((g * _sigmoid(g)) * u).astype(_BF16))
    h = h + (yield from _dot_cols(jnp.concatenate(acts, axis=1), w_down))
    hp = (h * _rms_scale(h)) * g_ple[...]
    gate = _sigmoid((yield from _dot_cols(hp, ple_gate)))
    return h + gate * (yield from _dot_cols(p, ple_proj))


def _ffn_and_ple_cost(w_gate, w_down, ple_gate, ple_proj):
    return 2 * _chunks_cost(w_gate) + _chunks_cost(w_down) + _chunks_cost(ple_gate) + _chunks_cost(ple_proj)


def _conv_mixer_steps(x, g_mix, w_in, conv_w, w_out, history):
    hn = ((x * _rms_scale(x)) * g_mix[...]).astype(_BF16)
    yield 0
    us, convs = [], []
    for c, width in _chunks(D_MODEL):
        cols = slice(c, c + width)
        c_gate = _col_dot(hn, w_in, D_MODEL + c, width)
        yield _dot_cost(w_in, width)
        xin = _col_dot(hn, w_in, 2 * D_MODEL + c, width)
        yield _dot_cost(w_in, width)
        u = c_gate * xin
        back2, back1 = history(cols, u)
        conv = conv_w[0:1, cols] * back2
        conv = conv + conv_w[1:2, cols] * back1
        conv = conv + conv_w[2:3, cols] * u
        us.append(u)
        convs.append(conv)
    gated = []
    for j, (c, width) in enumerate(_chunks(D_MODEL)):
        b_gate = _col_dot(hn, w_in, c, width)
        yield _dot_cost(w_in, width)
        gated.append((b_gate * convs[j]).astype(_BF16))
    y = yield from _dot_cols(jnp.concatenate(gated, axis=1), w_out)
    return x + y, jnp.concatenate(us, axis=1)


def _conv_mixer_cost(w_in, w_out):
    return _chunks_cost(w_in) + _chunks_cost(w_out)


def _pipeline_bodies(step, n_tiles, body):
    pl.when(step == 0)(lambda: body(True, False))
    pl.when((step > 0) & (step < n_tiles))(lambda: body(True, True))
    pl.when(step == n_tiles)(lambda: body(False, True))


def _load_weights(plan, staging, sems):
    chunks = [(src, dst, c0, r0, min(STAGING_ROWS, src.shape[0] - r0))
              for src, dst, c0 in plan for r0 in range(0, src.shape[0], STAGING_ROWS)]
    slots = staging.shape[0]

    def copy(i):
        src, _, _, r0, n = chunks[i]
        window = staging.at[i % slots, pl.ds(0, n), pl.ds(0, src.shape[1])]
        return pltpu.make_async_copy(src.at[pl.ds(r0, n), :], window, sems.at[i % slots])

    for i in range(min(slots - 1, len(chunks))):
        copy(i).start(priority=i % 2)
    for i, (src, dst, c0, r0, n) in enumerate(chunks):
        ahead = i + slots - 1
        if ahead < len(chunks):
            copy(ahead).start(priority=ahead % 2)
        copy(i).wait()
        cols = src.shape[1]
        dst[r0:r0 + n, c0:c0 + cols] = staging[i % slots, 0:n, 0:cols].astype(_BF16)


def _layer0_kernel(x_ref, p_ref, xs_ref, ps_ref, s0_ref, s1_ref, g_mix, g_ffn, g_ple, conv_w,
                   w_in_hbm, w_out_hbm, w_gate_hbm, w_up_hbm, w_down_hbm, ple_gate_hbm, ple_proj_hbm,
                   h_out, state_out, hs_out, us_out,
                   ubuf, hbuf, hfbuf, w_in, w_out, w_gate, w_up, w_down, ple_gate, ple_proj, staging, sems,
                   *, layer, tiles_per_seq, n_tiles):
    tm = x_ref.shape[0]
    step = pl.program_id(0)
    ffn_cost = _ffn_and_ple_cost(w_gate, w_down, ple_gate, ple_proj)
    conv_cost = _conv_mixer_cost(w_in, w_out)

    def history(cols, u):
        ubuf[SUBLANES:SUBLANES + tm, cols] = u
        return (ubuf[SUBLANES - 2:SUBLANES - 2 + tm, cols], ubuf[SUBLANES - 1:SUBLANES - 1 + tm, cols])

    def ffn_half():
        out = yield from _ffn_and_ple_steps(hbuf[...], hfbuf[...], p_ref[...], g_ple,
                                            w_gate, w_up, w_down, ple_gate, ple_proj)
        h_out[...] = out

    def conv_half():
        h, _ = yield from _conv_mixer_steps(x_ref[...], g_mix, w_in, conv_w, w_out, history)
        state_out[...] = ubuf[SUBLANES + tm - (CONV_WIDTH - 1):SUBLANES + tm, :]
        ubuf[0:SUBLANES, :] = ubuf[tm:tm + SUBLANES, :]
        hbuf[...] = h
        hfbuf[...] = _ffn_norm(h, g_ffn)

    def sample_layer():
        history = lambda cols, u: (s0_ref[:, cols], s1_ref[:, cols])
        h, u = yield from _conv_mixer_steps(xs_ref[...], g_mix, w_in, conv_w, w_out, history)
        out = yield from _ffn_and_ple_steps(h, _ffn_norm(h, g_ffn), ps_ref[...], g_ple,
                                            w_gate, w_up, w_down, ple_gate, ple_proj)
        hs_out[...] = out
        us_out[...] = u

    def body(first_half, second_half):
        lanes = []
        if first_half and not second_half:
            _load_weights([(w_in_hbm.at[0], w_in, 0), (w_out_hbm.at[0], w_out, 0),
                           (w_gate_hbm.at[layer], w_gate, 0), (w_up_hbm.at[layer], w_up, 0),
                           (w_down_hbm.at[layer], w_down, 0), (ple_gate_hbm.at[layer], ple_gate, 0),
                           (ple_proj_hbm.at[layer], ple_proj, 0)], staging, sems)
        if first_half:
            zeros = jnp.zeros((SUBLANES, D_MODEL), _F32)
            fresh = lax.rem(step, tiles_per_seq) == 0
            ubuf[0:SUBLANES, :] = jnp.where(fresh, zeros, ubuf[0:SUBLANES, :]) if second_half else zeros
        if second_half:
            ffn = ffn_half()
            next(ffn)
            lanes.append((ffn, ffn_cost, LAYER0_WINDOWS[1]))
        if first_half:
            conv = conv_half()
            next(conv)
            lanes.append((conv, conv_cost, LAYER0_WINDOWS[0]))
        else:
            rows = xs_ref.shape[0] / tm
            lanes.append((_scaled(sample_layer(), rows), (conv_cost + ffn_cost) * rows, (0.0, 1.0)))
        _interleave(*lanes)

    _pipeline_bodies(step, n_tiles, body)


def _qkv_steps(h, cos, sin_lo, sin_hi, g_kv, g_mix, w_kv, w_q):
    hs = h * _rms_scale(h)
    h_kv = (hs * g_kv[...]).astype(_BF16)
    h_q = (hs * g_mix[...]).astype(_BF16)
    yield 0
    kv = yield from _dot_cols(h_kv, w_kv)
    q = yield from _dot_cols(h_q, w_q)
    k = _rope(kv[:, :KV_WIDTH], cos, sin_lo, sin_hi)
    v = kv[:, KV_WIDTH:]
    q = _rope(q, cos, sin_lo, sin_hi) * (HEAD_DIM ** -0.5)
    return q, k, v


def _regroup_heads(x, order):
    low = lax.broadcasted_iota(jnp.int32, (1, LANES), 1) < HEAD_DIM
    out = []
    for s in range(x.shape[1] // LANES):
        halves = []
        for half in range(2):
            src_head = order[2 * s + half]
            slab = x[:, (src_head // 2) * LANES:(src_head // 2 + 1) * LANES]
            halves.append(slab if src_head % 2 == half else pltpu.roll(slab, HEAD_DIM, 1))
        out.append(jnp.where(low, halves[0], halves[1]))
    return jnp.concatenate(out, axis=1)


_SLOT_SWAP = [GROUP * (p % N_KV_HEADS) + p // N_KV_HEADS for p in range(N_HEADS)]


def _layer1_kernel(h_ref, p_ref, cos_ref, slo_ref, shi_ref, sinks_ref,
                   hs_ref, pss_ref, cos_s, slo_s, shi_s, sink_rows, ck_ref, cv_ref,
                   g_kv, g_mix, g_ffn, g_ple, g_final,
                   w_k_hbm, w_v_hbm, w_q_hbm, w_o_hbm, w_gate_hbm, w_up_hbm, w_down_hbm,
                   ple_gate_hbm, ple_proj_hbm,
                   y_out, kwin_out, vwin_out, ys_out, kwins_out, vwins_out,
                   k2buf, v2buf, hbuf, obuf, qs_buf, ks_buf, vs_buf, kst_buf, vst_buf, os_buf,
                   w_kv, w_q, w_o, w_gate, w_up, w_down, ple_gate, ple_proj, staging, sems,
                   *, layer, tiles_per_seq, n_tiles):
    tm = h_ref.shape[0]
    step = pl.program_id(0)
    attn_cost = GROUP * BLOCK / tm
    dense_cost = _chunks_cost(w_o) + _ffn_and_ple_cost(w_gate, w_down, ple_gate, ple_proj)
    per_step = ck_ref.shape[0]
    sample_rows = hs_ref.shape[0] / tm

    def dense_half():
        hp = hbuf[...] + (yield from _dot_cols(obuf[...], w_o))
        hp = yield from _ffn_and_ple_steps(hp, _ffn_norm(hp, g_ffn), p_ref[...], g_ple,
                                           w_gate, w_up, w_down, ple_gate, ple_proj)
        y_out[...] = (hp * _rms_scale(hp)) * g_final[...]

    def sample_qkv():
        q, k, v = yield from _qkv_steps(hs_ref[...], cos_s[...], slo_s[...], shi_s[...],
                                        g_kv, g_mix, w_kv, w_q)
        qs_buf[...] = _regroup_heads(q, _SLOT_SWAP)
        ks_buf[...] = k
        vs_buf[...] = v
        kst_buf[...] = k.T
        vst_buf[...] = v.T

    def sample_decode(j):
        r = (step - 1) * per_step + j
        w = ck_ref.shape[2]
        key = lax.broadcasted_iota(jnp.int32, (KV_WIDTH, w), 1)
        seq_lane = lax.broadcasted_iota(jnp.int32, kst_buf.shape, 1)
        lane = lax.broadcasted_iota(jnp.int32, (N_HEADS, KV_WIDTH), 1)
        r16 = lax.broadcasted_iota(jnp.int32, (N_HEADS, KV_WIDTH), 0)
        slot = r16 // N_KV_HEADS
        kvh = r16 - slot * N_KV_HEADS
        own = (lane >= kvh * HEAD_DIM) & (lane < (kvh + 1) * HEAD_DIM)
        sink = sink_rows[...]
        kn = ks_buf[pl.ds(r, 1), :]
        vn = vs_buf[pl.ds(r, 1), :]
        for win_ref, new_t, out_ref in ((ck_ref, kst_buf, kwins_out), (cv_ref, vst_buf, vwins_out)):
            column = jnp.sum(jnp.where(seq_lane == r, new_t[...], 0.0), axis=1, keepdims=True)
            out_ref[j] = jnp.where(key == w - 1, column, pltpu.roll(win_ref[j], w - 1, 1))
        lhs = jnp.zeros((N_HEADS, KV_WIDTH), _F32)
        for i in range(GROUP):
            q_i = jnp.broadcast_to(qs_buf[pl.ds(r, 1), i * KV_WIDTH:(i + 1) * KV_WIDTH], (N_HEADS, KV_WIDTH))
            lhs = jnp.where(own & (slot == i), q_i, lhs)
        lhs = lhs.astype(_BF16)
        s_old = jnp.dot(lhs, ck_ref[j].astype(_BF16), preferred_element_type=_F32)
        yield 1
        s_new = jnp.sum(lhs.astype(_F32) * kn.astype(_BF16).astype(_F32), axis=-1, keepdims=True)
        m = jnp.maximum(jnp.maximum(jnp.max(s_old, axis=-1, keepdims=True), s_new), sink)
        e_old = jnp.exp(s_old - m)
        e_new = jnp.exp(s_new - m)
        inv = 1.0 / (jnp.sum(e_old, axis=-1, keepdims=True) + e_new + jnp.exp(sink - m))
        yield 1
        o16 = lax.dot_general(e_old.astype(_BF16), cv_ref[j].astype(_BF16), (((1,), (1,)), ((), ())),
                              preferred_element_type=_F32)
        o16 = (o16 + e_new.astype(_BF16).astype(_F32) * vn.astype(_BF16).astype(_F32)) * inv
        for i in range(GROUP):
            picked = jnp.where(own & (slot == i), o16, 0.0)
            os_buf[pl.ds(r, 1), i * KV_WIDTH:(i + 1) * KV_WIDTH] = jnp.sum(picked, axis=0, keepdims=True)
        yield 1

    def sample_decodes():
        for j in range(per_step):
            yield from sample_decode(j)

    def sample_tail():
        yield from sample_decodes()
        o = _regroup_heads(os_buf[...], _SLOT_SWAP)
        h = hs_ref[...] + (yield from _scaled(_dot_cols(o, w_o), sample_rows))
        h = yield from _scaled(_ffn_and_ple_steps(h, _ffn_norm(h, g_ffn), pss_ref[...], g_ple,
                                                  w_gate, w_up, w_down, ple_gate, ple_proj), sample_rows)
        ys_out[...] = (h * _rms_scale(h)) * g_final[...]

    def attention_half():
        h = h_ref[...]
        q, k, v = yield from _qkv_steps(h, cos_ref[...], slo_ref[...], shi_ref[...], g_kv, g_mix, w_kv, w_q)
        qb = q.astype(_BF16)
        kwin_out[...] = k[tm - WINDOW:, :].T
        vwin_out[...] = v[tm - WINDOW:, :].T

        lane = lax.broadcasted_iota(jnp.int32, (1, LANES), 1)
        low = lane < HEAD_DIM
        for arr, buf in ((k, k2buf), (v, v2buf)):
            for s in range(KV_WIDTH // LANES):
                slab = arr[:, s * LANES:(s + 1) * LANES]
                swapped = pltpu.roll(slab, HEAD_DIM, 1)
                buf[2 * s, BLOCK:BLOCK + tm, :] = jnp.where(low, slab, swapped).astype(_BF16)
                buf[2 * s + 1, BLOCK:BLOCK + tm, :] = jnp.where(low, swapped, slab).astype(_BF16)

        row = lax.broadcasted_iota(jnp.int32, (BLOCK, 2 * BLOCK), 0)
        col = lax.broadcasted_iota(jnp.int32, (BLOCK, 2 * BLOCK), 1)
        diff = row + BLOCK - col
        band = (diff >= 0) & (diff <= WINDOW)
        first_col = jnp.where(lax.rem(step, tiles_per_seq) == 0, BLOCK, 0)
        zero = jnp.zeros((), _BF16)
        ones = jnp.ones((2 * BLOCK, LANES), _BF16)
        units = [(jb, g) for jb in range(tm // BLOCK) for g in range(N_KV_HEADS)]

        def scores(jb, g):
            r0 = jb * BLOCK
            parts = []
            for s in (2 * g, 2 * g + 1):
                slab = qb[r0:r0 + BLOCK, s * LANES:(s + 1) * LANES]
                parts.append(jnp.where(low, slab, zero))
                parts.append(jnp.where(low, zero, slab))
            qs = jnp.concatenate(parts, axis=0)
            return lax.dot_general(qs, k2buf[g, r0:r0 + 2 * BLOCK, :],
                                   (((1,), (1,)), ((), ())), preferred_element_type=_F32)

        def weights(jb, g, sc):
            valid = band & (col >= first_col) if jb == 0 else band
            es, tail = [], []
            for i in range(GROUP):
                sink = sinks_ref[GROUP * g + i]
                sp = jnp.where(valid, sc[i * BLOCK:(i + 1) * BLOCK], NEG_INF)
                m = jnp.maximum(jnp.max(sp, axis=-1, keepdims=True), sink)
                es.append(jnp.exp(sp - m).astype(_BF16))
                tail.append(jnp.exp(sink - m))
            return jnp.concatenate(es, axis=0), tail

        def values(jb, g, e, tail):
            r0 = jb * BLOCK
            vv = jnp.concatenate([v2buf[g, r0:r0 + 2 * BLOCK, :], ones], axis=1)
            o3 = jnp.dot(e, vv, preferred_element_type=_F32)
            o2 = []
            for i in range(GROUP):
                piece = o3[i * BLOCK:(i + 1) * BLOCK]
                o2.append(piece[:, :LANES] * (1.0 / (piece[:, LANES:] + tail[i])))
            obuf[r0:r0 + BLOCK, (2 * g) * LANES:(2 * g + 1) * LANES] = (
                jnp.where(low, o2[0], o2[1]).astype(_BF16))
            obuf[r0:r0 + BLOCK, (2 * g + 1) * LANES:(2 * g + 2) * LANES] = (
                jnp.where(low, o2[2], o2[3]).astype(_BF16))

        sc, ew = {}, {}
        for n in range(len(units) + 2):
            if n < len(units):
                sc[n] = scores(*units[n])
                yield attn_cost
            if 0 <= n - 1 < len(units):
                ew[n - 1] = weights(*units[n - 1], sc.pop(n - 1))
                yield 0.0
            if 0 <= n - 2 < len(units):
                values(*units[n - 2], *ew.pop(n - 2))
                yield attn_cost
        hbuf[...] = h
        k2buf[:, 0:BLOCK, :] = k2buf[:, tm:tm + BLOCK, :]
        v2buf[:, 0:BLOCK, :] = v2buf[:, tm:tm + BLOCK, :]

    def body(first_half, second_half):
        lanes = []
        qkv_cost = _chunks_cost(w_kv) + _chunks_cost(w_q)
        if first_half and not second_half:
            _load_weights([(w_k_hbm, w_kv, 0), (w_v_hbm, w_kv, KV_WIDTH),
                           (w_q_hbm.at[0], w_q, 0), (w_o_hbm.at[0], w_o, 0),
                           (w_gate_hbm.at[layer], w_gate, 0), (w_up_hbm.at[layer], w_up, 0),
                           (w_down_hbm.at[layer], w_down, 0), (ple_gate_hbm.at[layer], ple_gate, 0),
                           (ple_proj_hbm.at[layer], ple_proj, 0)], staging, sems)
            zeros = jnp.zeros((N_KV_HEADS, BLOCK, LANES), _BF16)
            k2buf[:, 0:BLOCK, :] = zeros
            v2buf[:, 0:BLOCK, :] = zeros
        if second_half:
            dense = dense_half()
            next(dense)
            lanes.append((dense, dense_cost, LAYER1_WINDOWS[1]))
        if first_half:
            n_units = (tm // BLOCK) * N_KV_HEADS
            attention = attention_half()
            next(attention)
            lanes.append((attention, qkv_cost + 2 * n_units * attn_cost, LAYER1_WINDOWS[0]))
        if first_half and not second_half:
            lanes.append((_scaled(sample_qkv(), sample_rows), qkv_cost * sample_rows, (0.0, 1.0)))
        elif first_half:
            lanes.append((sample_decodes(), 3 * per_step, SAMPLE_DECODE_WINDOW))
        else:
            lanes.append((sample_tail(), 3 * per_step + dense_cost * sample_rows, (0.0, 1.0)))
        _interleave(*lanes)

    _pipeline_bodies(step, n_tiles, body)


class _Layer:
    def __init__(self, stacked, layer):
        self.stacked, self.layer = stacked, layer


def _operand(x):
    return x.stacked if isinstance(x, _Layer) else x


def _resident(x):
    if isinstance(x, _Layer):
        rest = x.stacked.shape[1:]
        index = (x.layer,) + (0,) * len(rest)
        return pl.BlockSpec((None,) + rest, lambda *_: index, pipeline_mode=pl.Buffered(1))
    nd = x.ndim
    return pl.BlockSpec(x.shape, lambda *_: (0,) * nd, pipeline_mode=pl.Buffered(1))


def _params(n_grid_dims):
    return pltpu.CompilerParams(dimension_semantics=("arbitrary",) * n_grid_dims,
                                vmem_limit_bytes=VMEM_LIMIT_BYTES)


def _rope_tables(pos):
    half = ROT_DIM // 2
    inv_freq = np.power(np.float64(ROPE_THETA), -np.arange(half, dtype=np.float64) / half)
    ang = pos.astype(np.float64)[:, None] * inv_freq[None, :]
    cos, sin = np.cos(ang), np.sin(ang)
    n = pos.shape[0]
    pad = np.zeros((n, HEAD_DIM - ROT_DIM))
    zero = np.zeros((n, half))
    cos_h = np.concatenate([cos, cos, pad + 1.0], axis=1)
    lo_h = np.concatenate([-sin, zero, pad], axis=1)
    hi_h = np.concatenate([zero, sin, pad], axis=1)
    rep = LANES // HEAD_DIM
    return tuple(jnp.asarray(np.tile(a, (1, rep)), dtype=_F32) for a in (cos_h, lo_h, hi_h))


def kernel(x_prompt, x_sample, state_conv, cache_k_win, cache_v_win, p_prompt, p_sample,
           norm_mix_g, norm_ffn_g, norm_ple_g, kv_norm_g, final_norm_g,
           conv_w_in, conv_w, conv_w_out, w_k, w_v, w_q, sinks, w_o,
           ffn_w_gate, ffn_w_up, ffn_w_down, ple_w_proj, ple_w_gate):
    batch, seq, d = x_prompt.shape
    dec = x_sample.shape[0]
    w_buf = cache_k_win.shape[1]
    tm = PROMPT_TILE
    assert d == D_MODEL and seq % tm == 0 and tm % BLOCK == 0 and tm >= WINDOW
    assert x_sample.shape[1] == 1 and w_buf == WINDOW and dec % (batch * (seq // tm)) == 0

    row = lambda a: a.reshape(1, -1)
    rows = lambda a: a.reshape(a.shape[0], 1, a.shape[1])
    gains = dict(g_mix=rows(norm_mix_g), g_ffn=rows(norm_ffn_g), g_ple=rows(norm_ple_g))
    hbm = pl.BlockSpec(memory_space=pl.ANY)
    ffn_weights = [ffn_w_gate, ffn_w_up, ffn_w_down, ple_w_gate, ple_w_proj]
    staging = [pltpu.VMEM((STAGING_SLOTS, STAGING_ROWS, max(w.shape[-1] for w in [conv_w_in] + ffn_weights)), _F32),
               pltpu.SemaphoreType.DMA((STAGING_SLOTS,))]
    resident_bf16 = lambda ws: [pltpu.VMEM(w.shape[-2:], _BF16) for w in ws]

    nt = seq // tm
    n_tiles = batch * nt
    cur = lambda s: jnp.minimum(s, n_tiles - 1)
    prev = lambda s: jnp.maximum(s - 1, 0)
    cur_tile = pl.BlockSpec((None, tm, d), lambda s: (cur(s) // nt, cur(s) % nt, 0))
    prev_tile = pl.BlockSpec((None, tm, d), lambda s: (prev(s) // nt, prev(s) % nt, 0))
    prev_ple = lambda layer: pl.BlockSpec((None, None, tm, p_prompt.shape[-1]),
                                          lambda s: (layer, prev(s) // nt, prev(s) % nt, 0))
    xs = x_sample.reshape(dec, d)
    ps = p_sample.reshape(p_sample.shape[0], dec, p_sample.shape[-1])
    sample_in = [xs, _Layer(ps, 0), state_conv[0, :, 0, :], state_conv[0, :, 1, :]]
    l0_small = [_Layer(gains[n], 0) for n in ("g_mix", "g_ffn", "g_ple")] + [_Layer(conv_w, 0)]
    l0_weights = [conv_w_in, conv_w_out] + ffn_weights
    sample_rows = pl.BlockSpec((dec, d), lambda s: (0, 0))
    h1, conv_state_prompt, h1s, u_s = pl.pallas_call(
        functools.partial(_layer0_kernel, layer=0, tiles_per_seq=nt, n_tiles=n_tiles),
        grid=(n_tiles + 1,),
        in_specs=[cur_tile, prev_ple(0)] + [_resident(a) for a in sample_in + l0_small]
                 + [hbm] * len(l0_weights),
        out_specs=[prev_tile, pl.BlockSpec((None, CONV_WIDTH - 1, d), lambda s: (cur(s) // nt, 0, 0)),
                   sample_rows, sample_rows],
        out_shape=[jax.ShapeDtypeStruct((batch, seq, d), _F32),
                   jax.ShapeDtypeStruct((batch, CONV_WIDTH - 1, d), _F32),
                   jax.ShapeDtypeStruct((dec, d), _F32),
                   jax.ShapeDtypeStruct((dec, d), _F32)],
        scratch_shapes=[pltpu.VMEM((tm + SUBLANES, d), _F32),
                        pltpu.VMEM((tm, d), _F32),
                        pltpu.VMEM((tm, d), _BF16)] + resident_bf16(l0_weights) + staging,
        compiler_params=_params(1),
        name="layer0",
    )(x_prompt, p_prompt, *map(_operand, sample_in + l0_small), *l0_weights)
    conv_state_sample = jnp.stack([state_conv[0, :, 1, :], u_s], axis=1)[None]

    cos_p, lo_p, hi_p = _rope_tables(np.arange(seq))
    cos_s, lo_s, hi_s = _rope_tables(np.full((dec,), PAST_LEN))
    sink_rows = sinks[0].reshape(N_KV_HEADS, GROUP).T.reshape(N_HEADS, 1)
    windows_t = lambda a: jnp.transpose(a, (0, 2, 3, 1)).reshape(a.shape[0], KV_WIDTH, a.shape[1])
    windows = lambda a: jnp.transpose(a.reshape(a.shape[0], N_KV_HEADS, HEAD_DIM, a.shape[2]), (0, 3, 1, 2))
    ck, cv = windows_t(cache_k_win), windows_t(cache_v_win)
    per_step = dec // n_tiles
    cache = pl.BlockSpec((per_step, KV_WIDTH, w_buf), lambda s: (prev(s), 0, 0))
    sample_l1 = [h1s, _Layer(ps, 1), cos_s, lo_s, hi_s, sink_rows]
    l1_small = [row(kv_norm_g)] + [_Layer(gains[n], 1) for n in ("g_mix", "g_ffn", "g_ple")] + [row(final_norm_g)]
    l1_weights = [w_k, w_v, w_q, w_o] + ffn_weights
    l1_resident = [pltpu.VMEM((d, 2 * KV_WIDTH), _BF16)] + resident_bf16([w_q, w_o] + ffn_weights)
    y_prompt, k_win_prompt, v_win_prompt, y_sample, k_win_sample, v_win_sample = pl.pallas_call(
        functools.partial(_layer1_kernel, layer=1, tiles_per_seq=nt, n_tiles=n_tiles),
        grid=(n_tiles + 1,),
        in_specs=[cur_tile, prev_ple(1)]
                 + [pl.BlockSpec((tm, LANES), lambda s: (cur(s) % nt, 0))] * 3
                 + [pl.BlockSpec(memory_space=pltpu.SMEM)]
                 + [_resident(a) for a in sample_l1] + [cache, cache]
                 + [_resident(a) for a in l1_small] + [hbm] * len(l1_weights),
        out_specs=[prev_tile,
                   pl.BlockSpec((None, KV_WIDTH, WINDOW), lambda s: (cur(s) // nt, 0, 0)),
                   pl.BlockSpec((None, KV_WIDTH, WINDOW), lambda s: (cur(s) // nt, 0, 0)),
                   sample_rows, cache, cache],
        out_shape=[jax.ShapeDtypeStruct((batch, seq, d), _F32),
                   jax.ShapeDtypeStruct((batch, KV_WIDTH, WINDOW), _F32),
                   jax.ShapeDtypeStruct((batch, KV_WIDTH, WINDOW), _F32),
                   jax.ShapeDtypeStruct((dec, d), _F32),
                   jax.ShapeDtypeStruct((dec, KV_WIDTH, w_buf), _F32),
                   jax.ShapeDtypeStruct((dec, KV_WIDTH, w_buf), _F32)],
        scratch_shapes=[pltpu.VMEM((N_KV_HEADS, BLOCK + tm, LANES), _BF16),
                        pltpu.VMEM((N_KV_HEADS, BLOCK + tm, LANES), _BF16),
                        pltpu.VMEM((tm, d), _F32),
                        pltpu.VMEM((tm, d), _BF16),
                        pltpu.VMEM((dec, d), _F32),
                        pltpu.VMEM((dec, KV_WIDTH), _F32),
                        pltpu.VMEM((dec, KV_WIDTH), _F32),
                        pltpu.VMEM((KV_WIDTH, dec), _F32),
                        pltpu.VMEM((KV_WIDTH, dec), _F32),
                        pltpu.VMEM((dec, d), _F32)] + l1_resident + staging,
        compiler_params=_params(1),
        name="layer1",
    )(h1, p_prompt, cos_p, lo_p, hi_p, sinks[0], *map(_operand, sample_l1), ck, cv,
      *map(_operand, l1_small), *l1_weights)

    return (y_prompt, y_sample.reshape(dec, 1, d),
            conv_state_prompt[None], conv_state_sample,
            windows(k_win_prompt), windows(v_win_prompt),
            windows(k_win_sample), windows(v_win_sample))
```

```python
import functools

import jax
import jax.numpy as jnp
import numpy as np
from jax import lax
from jax.experimental import pallas as pl
from jax.experimental.pallas import tpu as pltpu

D_MODEL = 1024
HEAD_DIM = 64
N_HEADS = 16
N_KV_HEADS = 4
GROUP = N_HEADS // N_KV_HEADS
KV_WIDTH = N_KV_HEADS * HEAD_DIM
ROT_DIM = HEAD_DIM // 4
ROPE_THETA = 500000.0
WINDOW = 128
BLOCK = 128
CONV_WIDTH = 3
PAST_LEN = 16384
RMS_EPS = 1e-6
NEG_INF = -1e30

LANES = 128
SUBLANES = 8
MXU_COLS = 256
CHUNK_COLS = 2 * MXU_COLS
VMEM_LIMIT_BYTES = 60000 * 1024

PROMPT_TILE = 256
STAGING_ROWS = 256
STAGING_SLOTS = 3
SAMPLE_DECODE_WINDOW = (0.05, 0.85)
LAYER0_WINDOWS = ((0.10, 0.92), (0.0, 1.0))
LAYER1_WINDOWS = ((0.04, 0.92), (0.0, 1.0))

_BF16 = jnp.bfloat16
_F32 = jnp.float32


def _rms_scale(x):
    return lax.rsqrt(jnp.mean(x * x, axis=-1, keepdims=True) + RMS_EPS)


def _sigmoid(x):
    return 1.0 / (1.0 + jnp.exp(-x))


def _rope(x, cos, sin_lo, sin_hi):
    half = ROT_DIM // 2
    out = []
    for c in range(x.shape[1] // LANES):
        slab = x[:, c * LANES:(c + 1) * LANES]
        out.append(slab * cos
                   + pltpu.roll(slab, LANES - half, 1) * sin_lo
                   + pltpu.roll(slab, half, 1) * sin_hi)
    return jnp.concatenate(out, axis=1)


def _interleave(*lanes):
    done = [0.0] * len(lanes)
    live = list(range(len(lanes)))

    def position(j):
        _, total, (start, end) = lanes[j]
        return start + (end - start) * done[j] / total

    while live:
        i = min(live, key=position)
        try:
            done[i] += next(lanes[i][0])
        except StopIteration:
            live.remove(i)


def _scaled(steps, factor):
    try:
        while True:
            yield next(steps) * factor
    except StopIteration as done:
        return done.value


def _chunks(n_cols):
    return [(c, min(CHUNK_COLS, n_cols - c)) for c in range(0, n_cols, CHUNK_COLS)]


def _col_dot(a, w_ref, c0, width):
    return jnp.dot(a, w_ref[:, c0:c0 + width], preferred_element_type=_F32)


def _dot_cost(w_ref, width):
    return -(-w_ref.shape[0] // MXU_COLS) * (width // MXU_COLS)


def _chunks_cost(w_ref):
    return _dot_cost(w_ref, w_ref.shape[1])


def _dot_cols(a, w_ref):
    a = a.astype(_BF16)
    cols = []
    for c, width in _chunks(w_ref.shape[1]):
        cols.append(_col_dot(a, w_ref, c, width))
        yield _dot_cost(w_ref, width)
    return jnp.concatenate(cols, axis=1)


def _ffn_norm(h, g_ffn):
    return ((h * _rms_scale(h)) * g_ffn[...]).astype(_BF16)


def _ffn_and_ple_steps(h, hf, p, g_ple, w_gate, w_up, w_down, ple_gate, ple_proj):
    acts = []
    for c, width in _chunks(w_gate.shape[1]):
        g = _col_dot(hf, w_gate, c, width)
        yield _dot_cost(w_gate, width)
        u = _col_dot(hf, w_up, c, width)
        yield _dot_cost(w_up, width)
        acts.append(((g * _sigmoid(g)) * u).astype(_BF16))
    h = h + (yield from _dot_cols(jnp.concatenate(acts, axis=1), w_down))
    hp = (h * _rms_scale(h)) * g_ple[...]
    gate = _sigmoid((yield from _dot_cols(hp, ple_gate)))
    return h + gate * (yield from _dot_cols(p, ple_proj))


def _ffn_and_ple_cost(w_gate, w_down, ple_gate, ple_proj):
    return 2 * _chunks_cost(w_gate) + _chunks_cost(w_down) + _chunks_cost(ple_gate) + _chunks_cost(ple_proj)


def _conv_mixer_steps(x, g_mix, w_in, conv_w, w_out, history):
    hn = ((x * _rms_scale(x)) * g_mix[...]).astype(_BF16)
    yield 0
    us, convs = [], []
    for c, width in _chunks(D_MODEL):
        cols = slice(c, c + width)
        c_gate = _col_dot(hn, w_in, D_MODEL + c, width)
        yield _dot_cost(w_in, width)
        xin = _col_dot(hn, w_in, 2 * D_MODEL + c, width)
        yield _dot_cost(w_in, width)
        u = c_gate * xin
        back2, back1 = history(cols, u)
        conv = conv_w[0:1, cols] * back2
        conv = conv + conv_w[1:2, cols] * back1
        conv = conv + conv_w[2:3, cols] * u
        us.append(u)
        convs.append(conv)
    gated = []
    for j, (c, width) in enumerate(_chunks(D_MODEL)):
        b_gate = _col_dot(hn, w_in, c, width)
        yield _dot_cost(w_in, width)
        gated.append((b_gate * convs[j]).astype(_BF16))
    y = yield from _dot_cols(jnp.concatenate(gated, axis=1), w_out)
    return x + y, jnp.concatenate(us, axis=1)


def _conv_mixer_cost(w_in, w_out):
    return _chunks_cost(w_in) + _chunks_cost(w_out)


def _pipeline_bodies(step, n_tiles, body):
    pl.when(step == 0)(lambda: body(True, False))
    pl.when((step > 0) & (step < n_tiles))(lambda: body(True, True))
    pl.when(step == n_tiles)(lambda: body(False, True))


def _load_weights(plan, staging, sems):
    chunks = [(src, dst, c0, r0, min(STAGING_ROWS, src.shape[0] - r0))
              for src, dst, c0 in plan for r0 in range(0, src.shape[0], STAGING_ROWS)]
    slots = staging.shape[0]

    def copy(i):
        src, _, _, r0, n = chunks[i]
        window = staging.at[i % slots, pl.ds(0, n), pl.ds(0, src.shape[1])]
        return pltpu.make_async_copy(src.at[pl.ds(r0, n), :], window, sems.at[i % slots])

    for i in range(min(slots - 1, len(chunks))):
        copy(i).start(priority=i % 2)
    for i, (src, dst, c0, r0, n) in enumerate(chunks):
        ahead = i + slots - 1
        if ahead < len(chunks):
            copy(ahead).start(priority=ahead % 2)
        copy(i).wait()
        cols = src.shape[1]
        dst[r0:r0 + n, c0:c0 + cols] = staging[i % slots, 0:n, 0:cols].astype(_BF16)


def _layer0_kernel(x_ref, p_ref, xs_ref, ps_ref, s0_ref, s1_ref, g_mix, g_ffn, g_ple, conv_w,
                   w_in_hbm, w_out_hbm, w_gate_hbm, w_up_hbm, w_down_hbm, ple_gate_hbm, ple_proj_hbm,
                   h_out, state_out, hs_out, us_out,
                   ubuf, hbuf, hfbuf, w_in, w_out, w_gate, w_up, w_down, ple_gate, ple_proj, staging, sems,
                   *, layer, tiles_per_seq, n_tiles):
    tm = x_ref.shape[0]
    step = pl.program_id(0)
    g_mix, g_ffn, g_ple = (g.at[pl.ds(layer, 1)] for g in (g_mix, g_ffn, g_ple))
    ffn_cost = _ffn_and_ple_cost(w_gate, w_down, ple_gate, ple_proj)
    conv_cost = _conv_mixer_cost(w_in, w_out)

    def history(cols, u):
        ubuf[SUBLANES:SUBLANES + tm, cols] = u
        return (ubuf[SUBLANES - 2:SUBLANES - 2 + tm, cols], ubuf[SUBLANES - 1:SUBLANES - 1 + tm, cols])

    def ffn_half():
        out = yield from _ffn_and_ple_steps(hbuf[...], hfbuf[...], p_ref[...], g_ple,
                                            w_gate, w_up, w_down, ple_gate, ple_proj)
        h_out[...] = out

    def conv_half():
        h, _ = yield from _conv_mixer_steps(x_ref[...], g_mix, w_in, conv_w, w_out, history)
        state_out[...] = ubuf[SUBLANES + tm - (CONV_WIDTH - 1):SUBLANES + tm, :]
        ubuf[0:SUBLANES, :] = ubuf[tm:tm + SUBLANES, :]
        hbuf[...] = h
        hfbuf[...] = _ffn_norm(h, g_ffn)

    def sample_layer():
        history = lambda cols, u: (s0_ref[:, cols], s1_ref[:, cols])
        h, u = yield from _conv_mixer_steps(xs_ref[...], g_mix, w_in, conv_w, w_out, history)
        out = yield from _ffn_and_ple_steps(h, _ffn_norm(h, g_ffn), ps_ref[...], g_ple,
                                            w_gate, w_up, w_down, ple_gate, ple_proj)
        hs_out[...] = out
        us_out[...] = u

    def body(first_half, second_half):
        lanes = []
        if first_half and not second_half:
            _load_weights([(w_in_hbm.at[0], w_in, 0), (w_out_hbm.at[0], w_out, 0),
                           (w_gate_hbm.at[layer], w_gate, 0), (w_up_hbm.at[layer], w_up, 0),
                           (w_down_hbm.at[layer], w_down, 0), (ple_gate_hbm.at[layer], ple_gate, 0),
                           (ple_proj_hbm.at[layer], ple_proj, 0)], staging, sems)
        if first_half:
            zeros = jnp.zeros((SUBLANES, D_MODEL), _F32)
            fresh = lax.rem(step, tiles_per_seq) == 0
            ubuf[0:SUBLANES, :] = jnp.where(fresh, zeros, ubuf[0:SUBLANES, :]) if second_half else zeros
        if second_half:
            ffn = ffn_half()
            next(ffn)
            lanes.append((ffn, ffn_cost, LAYER0_WINDOWS[1]))
        if first_half:
            conv = conv_half()
            next(conv)
            lanes.append((conv, conv_cost, LAYER0_WINDOWS[0]))
        else:
            rows = xs_ref.shape[0] / tm
            lanes.append((_scaled(sample_layer(), rows), (conv_cost + ffn_cost) * rows, (0.0, 1.0)))
        _interleave(*lanes)

    _pipeline_bodies(step, n_tiles, body)


def _qkv_steps(h, cos, sin_lo, sin_hi, g_kv, g_mix, w_kv, w_q):
    hs = h * _rms_scale(h)
    h_kv = (hs * g_kv[...]).astype(_BF16)
    h_q = (hs * g_mix[...]).astype(_BF16)
    yield 0
    kv = yield from _dot_cols(h_kv, w_kv)
    q = yield from _dot_cols(h_q, w_q)
    k = _rope(kv[:, :KV_WIDTH], cos, sin_lo, sin_hi)
    v = kv[:, KV_WIDTH:]
    q = _rope(q, cos, sin_lo, sin_hi) * (HEAD_DIM ** -0.5)
    return q, k, v


def _regroup_heads(x, order):
    low = lax.broadcasted_iota(jnp.int32, (1, LANES), 1) < HEAD_DIM
    out = []
    for s in range(x.shape[1] // LANES):
        halves = []
        for half in range(2):
            src_head = order[2 * s + half]
            slab = x[:, (src_head // 2) * LANES:(src_head // 2 + 1) * LANES]
            halves.append(slab if src_head % 2 == half else pltpu.roll(slab, HEAD_DIM, 1))
        out.append(jnp.where(low, halves[0], halves[1]))
    return jnp.concatenate(out, axis=1)


_SLOT_SWAP = [GROUP * (p % N_KV_HEADS) + p // N_KV_HEADS for p in range(N_HEADS)]


def _layer1_kernel(h_ref, p_ref, cos_ref, slo_ref, shi_ref, sinks_ref,
                   hs_ref, pss_ref, cos_s, slo_s, shi_s, ck_ref, cv_ref,
                   g_kv, g_mix, g_ffn, g_ple, g_final,
                   w_k_hbm, w_v_hbm, w_q_hbm, w_o_hbm, w_gate_hbm, w_up_hbm, w_down_hbm,
                   ple_gate_hbm, ple_proj_hbm,
                   y_out, kwin_out, vwin_out, ys_out, kwins_out, vwins_out,
                   k2buf, v2buf, hbuf, obuf, qs_buf, ks_buf, vs_buf, kst_buf, vst_buf, os_buf,
                   w_kv, w_q, w_o, w_gate, w_up, w_down, ple_gate, ple_proj, staging, sems,
                   *, layer, tiles_per_seq, n_tiles):
    tm = h_ref.shape[0]
    step = pl.program_id(0)
    g_mix, g_ffn, g_ple = (g.at[pl.ds(layer, 1)] for g in (g_mix, g_ffn, g_ple))
    attn_cost = GROUP * BLOCK / tm
    dense_cost = _chunks_cost(w_o) + _ffn_and_ple_cost(w_gate, w_down, ple_gate, ple_proj)
    per_step = ck_ref.shape[0]
    sample_rows = hs_ref.shape[0] / tm

    def dense_half():
        hp = hbuf[...] + (yield from _dot_cols(obuf[...], w_o))
        hp = yield from _ffn_and_ple_steps(hp, _ffn_norm(hp, g_ffn), p_ref[...], g_ple,
                                           w_gate, w_up, w_down, ple_gate, ple_proj)
        y_out[...] = (hp * _rms_scale(hp)) * g_final[...]

    def sample_qkv():
        q, k, v = yield from _qkv_steps(hs_ref[...], cos_s[...], slo_s[...], shi_s[...],
                                        g_kv, g_mix, w_kv, w_q)
        qs_buf[...] = _regroup_heads(q, _SLOT_SWAP)
        ks_buf[...] = k
        vs_buf[...] = v
        kst_buf[...] = k.T
        vst_buf[...] = v.T

    def sample_decode(j):
        r = (step - 1) * per_step + j
        w = ck_ref.shape[2]
        key = lax.broadcasted_iota(jnp.int32, (KV_WIDTH, w), 1)
        seq_lane = lax.broadcasted_iota(jnp.int32, kst_buf.shape, 1)
        lane = lax.broadcasted_iota(jnp.int32, (N_HEADS, KV_WIDTH), 1)
        r16 = lax.broadcasted_iota(jnp.int32, (N_HEADS, KV_WIDTH), 0)
        slot = r16 // N_KV_HEADS
        kvh = r16 - slot * N_KV_HEADS
        own = (lane >= kvh * HEAD_DIM) & (lane < (kvh + 1) * HEAD_DIM)
        head_row = lax.broadcasted_iota(jnp.int32, (N_HEADS, 1), 0)
        sink = jnp.zeros((N_HEADS, 1), _F32)
        for h in range(N_HEADS):
            sink = jnp.where(head_row == _SLOT_SWAP[h], sinks_ref[h], sink)
        kn = ks_buf[pl.ds(r, 1), :]
        vn = vs_buf[pl.ds(r, 1), :]
        for win_ref, new_t, out_ref in ((ck_ref, kst_buf, kwins_out), (cv_ref, vst_buf, vwins_out)):
            column = jnp.sum(jnp.where(seq_lane == r, new_t[...], 0.0), axis=1, keepdims=True)
            out_ref[j] = jnp.where(key == w - 1, column, pltpu.roll(win_ref[j], w - 1, 1))
        lhs = jnp.zeros((N_HEADS, KV_WIDTH), _F32)
        for i in range(GROUP):
            q_i = jnp.broadcast_to(qs_buf[pl.ds(r, 1), i * KV_WIDTH:(i + 1) * KV_WIDTH], (N_HEADS, KV_WIDTH))
            lhs = jnp.where(own & (slot == i), q_i, lhs)
        lhs = lhs.astype(_BF16)
        s_old = jnp.dot(lhs, ck_ref[j].astype(_BF16), preferred_element_type=_F32)
        yield 1
        s_new = jnp.sum(lhs.astype(_F32) * kn.astype(_BF16).astype(_F32), axis=-1, keepdims=True)
        m = jnp.maximum(jnp.maximum(jnp.max(s_old, axis=-1, keepdims=True), s_new), sink)
        e_old = jnp.exp(s_old - m)
        e_new = jnp.exp(s_new - m)
        inv = 1.0 / (jnp.sum(e_old, axis=-1, keepdims=True) + e_new + jnp.exp(sink - m))
        yield 1
        o16 = lax.dot_general(e_old.astype(_BF16), cv_ref[j].astype(_BF16), (((1,), (1,)), ((), ())),
                              preferred_element_type=_F32)
        o16 = (o16 + e_new.astype(_BF16).astype(_F32) * vn.astype(_BF16).astype(_F32)) * inv
        for i in range(GROUP):
            picked = jnp.where(own & (slot == i), o16, 0.0)
            os_buf[pl.ds(r, 1), i * KV_WIDTH:(i + 1) * KV_WIDTH] = jnp.sum(picked, axis=0, keepdims=True)
        yield 1

    def sample_decodes():
        for j in range(per_step):
            yield from sample_decode(j)

    def sample_tail():
        yield from sample_decodes()
        o = _regroup_heads(os_buf[...], _SLOT_SWAP)
        h = hs_ref[...] + (yield from _scaled(_dot_cols(o, w_o), sample_rows))
        h = yield from _scaled(_ffn_and_ple_steps(h, _ffn_norm(h, g_ffn), pss_ref[...], g_ple,
                                                  w_gate, w_up, w_down, ple_gate, ple_proj), sample_rows)
        ys_out[...] = (h * _rms_scale(h)) * g_final[...]

    def attention_half():
        h = h_ref[...]
        q, k, v = yield from _qkv_steps(h, cos_ref[...], slo_ref[...], shi_ref[...], g_kv, g_mix, w_kv, w_q)
        qb = q.astype(_BF16)
        kwin_out[...] = k[tm - WINDOW:, :].T
        vwin_out[...] = v[tm - WINDOW:, :].T

        lane = lax.broadcasted_iota(jnp.int32, (1, LANES), 1)
        low = lane < HEAD_DIM
        for arr, buf in ((k, k2buf), (v, v2buf)):
            for s in range(KV_WIDTH // LANES):
                slab = arr[:, s * LANES:(s + 1) * LANES]
                swapped = pltpu.roll(slab, HEAD_DIM, 1)
                buf[2 * s, BLOCK:BLOCK + tm, :] = jnp.where(low, slab, swapped).astype(_BF16)
                buf[2 * s + 1, BLOCK:BLOCK + tm, :] = jnp.where(low, swapped, slab).astype(_BF16)

        row = lax.broadcasted_iota(jnp.int32, (BLOCK, 2 * BLOCK), 0)
        col = lax.broadcasted_iota(jnp.int32, (BLOCK, 2 * BLOCK), 1)
        diff = row + BLOCK - col
        band = (diff >= 0) & (diff <= WINDOW)
        first_col = jnp.where(lax.rem(step, tiles_per_seq) == 0, BLOCK, 0)
        zero = jnp.zeros((), _BF16)
        ones = jnp.ones((2 * BLOCK, LANES), _BF16)
        units = [(jb, g) for jb in range(tm // BLOCK) for g in range(N_KV_HEADS)]

        def scores(jb, g):
            r0 = jb * BLOCK
            parts = []
            for s in (2 * g, 2 * g + 1):
                slab = qb[r0:r0 + BLOCK, s * LANES:(s + 1) * LANES]
                parts.append(jnp.where(low, slab, zero))
                parts.append(jnp.where(low, zero, slab))
            qs = jnp.concatenate(parts, axis=0)
            return lax.dot_general(qs, k2buf[g, r0:r0 + 2 * BLOCK, :],
                                   (((1,), (1,)), ((), ())), preferred_element_type=_F32)

        def weights(jb, g, sc):
            valid = band & (col >= first_col) if jb == 0 else band
            es, tail = [], []
            for i in range(GROUP):
                sink = sinks_ref[GROUP * g + i]
                sp = jnp.where(valid, sc[i * BLOCK:(i + 1) * BLOCK], NEG_INF)
                m = jnp.maximum(jnp.max(sp, axis=-1, keepdims=True), sink)
                es.append(jnp.exp(sp - m).astype(_BF16))
                tail.append(jnp.exp(sink - m))
            return jnp.concatenate(es, axis=0), tail

        def values(jb, g, e, tail):
            r0 = jb * BLOCK
            vv = jnp.concatenate([v2buf[g, r0:r0 + 2 * BLOCK, :], ones], axis=1)
            o3 = jnp.dot(e, vv, preferred_element_type=_F32)
            o2 = []
            for i in range(GROUP):
                piece = o3[i * BLOCK:(i + 1) * BLOCK]
                o2.append(piece[:, :LANES] * (1.0 / (piece[:, LANES:] + tail[i])))
            obuf[r0:r0 + BLOCK, (2 * g) * LANES:(2 * g + 1) * LANES] = (
                jnp.where(low, o2[0], o2[1]).astype(_BF16))
            obuf[r0:r0 + BLOCK, (2 * g + 1) * LANES:(2 * g + 2) * LANES] = (
                jnp.where(low, o2[2], o2[3]).astype(_BF16))

        sc, ew = {}, {}
        for n in range(len(units) + 2):
            if n < len(units):
                sc[n] = scores(*units[n])
                yield attn_cost
            if 0 <= n - 1 < len(units):
                ew[n - 1] = weights(*units[n - 1], sc.pop(n - 1))
                yield 0.0
            if 0 <= n - 2 < len(units):
                values(*units[n - 2], *ew.pop(n - 2))
                yield attn_cost
        hbuf[...] = h
        k2buf[:, 0:BLOCK, :] = k2buf[:, tm:tm + BLOCK, :]
        v2buf[:, 0:BLOCK, :] = v2buf[:, tm:tm + BLOCK, :]

    def body(first_half, second_half):
        lanes = []
        qkv_cost = _chunks_cost(w_kv) + _chunks_cost(w_q)
        if first_half and not second_half:
            _load_weights([(w_k_hbm, w_kv, 0), (w_v_hbm, w_kv, KV_WIDTH),
                           (w_q_hbm.at[0], w_q, 0), (w_o_hbm.at[0], w_o, 0),
                           (w_gate_hbm.at[layer], w_gate, 0), (w_up_hbm.at[layer], w_up, 0),
                           (w_down_hbm.at[layer], w_down, 0), (ple_gate_hbm.at[layer], ple_gate, 0),
                           (ple_proj_hbm.at[layer], ple_proj, 0)], staging, sems)
            zeros = jnp.zeros((N_KV_HEADS, BLOCK, LANES), _BF16)
            k2buf[:, 0:BLOCK, :] = zeros
            v2buf[:, 0:BLOCK, :] = zeros
        if second_half:
            dense = dense_half()
            next(dense)
            lanes.append((dense, dense_cost, LAYER1_WINDOWS[1]))
        if first_half:
            n_units = (tm // BLOCK) * N_KV_HEADS
            attention = attention_half()
            next(attention)
            lanes.append((attention, qkv_cost + 2 * n_units * attn_cost, LAYER1_WINDOWS[0]))
        if first_half and not second_half:
            lanes.append((_scaled(sample_qkv(), sample_rows), qkv_cost * sample_rows, (0.0, 1.0)))
        elif first_half:
            lanes.append((sample_decodes(), 3 * per_step, SAMPLE_DECODE_WINDOW))
        else:
            lanes.append((sample_tail(), 3 * per_step + dense_cost * sample_rows, (0.0, 1.0)))
        _interleave(*lanes)

    _pipeline_bodies(step, n_tiles, body)


class _Layer:
    def __init__(self, stacked, layer):
        self.stacked, self.layer = stacked, layer


def _operand(x):
    return x.stacked if isinstance(x, _Layer) else x


def _resident(x):
    if isinstance(x, _Layer):
        rest = x.stacked.shape[1:]
        index = (x.layer,) + (0,) * len(rest)
        return pl.BlockSpec((None,) + rest, lambda *_: index, pipeline_mode=pl.Buffered(1))
    nd = x.ndim
    return pl.BlockSpec(x.shape, lambda *_: (0,) * nd, pipeline_mode=pl.Buffered(1))


def _params(n_grid_dims):
    return pltpu.CompilerParams(dimension_semantics=("arbitrary",) * n_grid_dims,
                                vmem_limit_bytes=VMEM_LIMIT_BYTES)


def _rope_tables(pos):
    half = ROT_DIM // 2
    inv_freq = np.power(np.float64(ROPE_THETA), -np.arange(half, dtype=np.float64) / half)
    ang = pos.astype(np.float64)[:, None] * inv_freq[None, :]
    cos, sin = np.cos(ang), np.sin(ang)
    n = pos.shape[0]
    pad = np.zeros((n, HEAD_DIM - ROT_DIM))
    zero = np.zeros((n, half))
    cos_h = np.concatenate([cos, cos, pad + 1.0], axis=1)
    lo_h = np.concatenate([-sin, zero, pad], axis=1)
    hi_h = np.concatenate([zero, sin, pad], axis=1)
    rep = LANES // HEAD_DIM
    return tuple(jnp.asarray(np.tile(a, (1, rep)), dtype=_F32) for a in (cos_h, lo_h, hi_h))


def kernel(x_prompt, x_sample, state_conv, cache_k_win, cache_v_win, p_prompt, p_sample,
           norm_mix_g, norm_ffn_g, norm_ple_g, kv_norm_g, final_norm_g,
           conv_w_in, conv_w, conv_w_out, w_k, w_v, w_q, sinks, w_o,
           ffn_w_gate, ffn_w_up, ffn_w_down, ple_w_proj, ple_w_gate):
    batch, seq, d = x_prompt.shape
    dec = x_sample.shape[0]
    w_buf = cache_k_win.shape[1]
    tm = PROMPT_TILE
    assert d == D_MODEL and seq % tm == 0 and tm % BLOCK == 0 and tm >= WINDOW
    assert x_sample.shape[1] == 1 and w_buf == WINDOW and dec % (batch * (seq // tm)) == 0

    row = lambda a: a.reshape(1, -1)
    gains = [norm_mix_g, norm_ffn_g, norm_ple_g]
    hbm = pl.BlockSpec(memory_space=pl.ANY)
    ffn_weights = [ffn_w_gate, ffn_w_up, ffn_w_down, ple_w_gate, ple_w_proj]
    staging = [pltpu.VMEM((STAGING_SLOTS, STAGING_ROWS, max(w.shape[-1] for w in [conv_w_in] + ffn_weights)), _F32),
               pltpu.SemaphoreType.DMA((STAGING_SLOTS,))]
    resident_bf16 = lambda ws: [pltpu.VMEM(w.shape[-2:], _BF16) for w in ws]

    nt = seq // tm
    n_tiles = batch * nt
    cur = lambda s: jnp.minimum(s, n_tiles - 1)
    prev = lambda s: jnp.maximum(s - 1, 0)
    cur_tile = pl.BlockSpec((None, tm, d), lambda s: (cur(s) // nt, cur(s) % nt, 0))
    prev_tile = pl.BlockSpec((None, tm, d), lambda s: (prev(s) // nt, prev(s) % nt, 0))
    prev_ple = lambda layer: pl.BlockSpec((None, None, tm, p_prompt.shape[-1]),
                                          lambda s: (layer, prev(s) // nt, prev(s) % nt, 0))
    xs = x_sample.reshape(dec, d)
    ps = p_sample.reshape(p_sample.shape[0], dec, p_sample.shape[-1])
    sample_in = [xs, _Layer(ps, 0), state_conv[0, :, 0, :], state_conv[0, :, 1, :]]
    l0_small = gains + [_Layer(conv_w, 0)]
    l0_weights = [conv_w_in, conv_w_out] + ffn_weights
    sample_rows = pl.BlockSpec((dec, d), lambda s: (0, 0))
    h1, conv_state_prompt, h1s, u_s = pl.pallas_call(
        functools.partial(_layer0_kernel, layer=0, tiles_per_seq=nt, n_tiles=n_tiles),
        grid=(n_tiles + 1,),
        in_specs=[cur_tile, prev_ple(0)] + [_resident(a) for a in sample_in + l0_small]
                 + [hbm] * len(l0_weights),
        out_specs=[prev_tile, pl.BlockSpec((None, CONV_WIDTH - 1, d), lambda s: (cur(s) // nt, 0, 0)),
                   sample_rows, sample_rows],
        out_shape=[jax.ShapeDtypeStruct((batch, seq, d), _F32),
                   jax.ShapeDtypeStruct((batch, CONV_WIDTH - 1, d), _F32),
                   jax.ShapeDtypeStruct((dec, d), _F32),
                   jax.ShapeDtypeStruct((dec, d), _F32)],
        scratch_shapes=[pltpu.VMEM((tm + SUBLANES, d), _F32),
                        pltpu.VMEM((tm, d), _F32),
                        pltpu.VMEM((tm, d), _BF16)] + resident_bf16(l0_weights) + staging,
        compiler_params=_params(1),
        name="layer0",
    )(x_prompt, p_prompt, *map(_operand, sample_in + l0_small), *l0_weights)
    conv_state_sample = jnp.stack([state_conv[0, :, 1, :], u_s], axis=1)[None]

    cos_p, lo_p, hi_p = _rope_tables(np.arange(seq))
    cos_s, lo_s, hi_s = _rope_tables(np.full((dec,), PAST_LEN))
    windows_t = lambda a: jnp.transpose(a, (0, 2, 3, 1)).reshape(a.shape[0], KV_WIDTH, a.shape[1])
    windows = lambda a: jnp.transpose(a.reshape(a.shape[0], N_KV_HEADS, HEAD_DIM, a.shape[2]), (0, 3, 1, 2))
    ck, cv = windows_t(cache_k_win), windows_t(cache_v_win)
    per_step = dec // n_tiles
    cache = pl.BlockSpec((per_step, KV_WIDTH, w_buf), lambda s: (prev(s), 0, 0))
    sample_l1 = [h1s, _Layer(ps, 1), cos_s, lo_s, hi_s]
    l1_small = [row(kv_norm_g)] + gains + [row(final_norm_g)]
    l1_weights = [w_k, w_v, w_q, w_o] + ffn_weights
    l1_resident = [pltpu.VMEM((d, 2 * KV_WIDTH), _BF16)] + resident_bf16([w_q, w_o] + ffn_weights)
    y_prompt, k_win_prompt, v_win_prompt, y_sample, k_win_sample, v_win_sample = pl.pallas_call(
        functools.partial(_layer1_kernel, layer=1, tiles_per_seq=nt, n_tiles=n_tiles),
        grid=(n_tiles + 1,),
        in_specs=[cur_tile, prev_ple(1)]
                 + [pl.BlockSpec((tm, LANES), lambda s: (cur(s) % nt, 0))] * 3
                 + [pl.BlockSpec(memory_space=pltpu.SMEM)]
                 + [_resident(a) for a in sample_l1] + [cache, cache]
                 + [_resident(a) for a in l1_small] + [hbm] * len(l1_weights),
        out_specs=[prev_tile,
                   pl.BlockSpec((None, KV_WIDTH, WINDOW), lambda s: (cur(s) // nt, 0, 0)),
                   pl.BlockSpec((None, KV_WIDTH, WINDOW), lambda s: (cur(s) // nt, 0, 0)),
                   sample_rows, cache, cache],
        out_shape=[jax.ShapeDtypeStruct((batch, seq, d), _F32),
                   jax.ShapeDtypeStruct((batch, KV_WIDTH, WINDOW), _F32),
                   jax.ShapeDtypeStruct((batch, KV_WIDTH, WINDOW), _F32),
                   jax.ShapeDtypeStruct((dec, d), _F32),
                   jax.ShapeDtypeStruct((dec, KV_WIDTH, w_buf), _F32),
                   jax.ShapeDtypeStruct((dec, KV_WIDTH, w_buf), _F32)],
        scratch_shapes=[pltpu.VMEM((N_KV_HEADS, BLOCK + tm, LANES), _BF16),
                        pltpu.VMEM((N_KV_HEADS, BLOCK + tm, LANES), _BF16),
                        pltpu.VMEM((tm, d), _F32),
                        pltpu.VMEM((tm, d), _BF16),
                        pltpu.VMEM((dec, d), _F32),
                        pltpu.VMEM((dec, KV_WIDTH), _F32),
                        pltpu.VMEM((dec, KV_WIDTH), _F32),
                        pltpu.VMEM((KV_WIDTH, dec), _F32),
                        pltpu.VMEM((KV_WIDTH, dec), _F32),
                        pltpu.VMEM((dec, d), _F32)] + l1_resident + staging,
        compiler_params=_params(1),
        name="layer1",
    )(h1, p_prompt, cos_p, lo_p, hi_p, sinks[0], *map(_operand, sample_l1), ck, cv,
      *map(_operand, l1_small), *l1_weights)

    return (y_prompt, y_sample.reshape(dec, 1, d),
            conv_state_prompt[None], conv_state_sample,
            windows(k_win_prompt), windows(v_win_prompt),
            windows(k_win_sample), windows(v_win_sample))
```

```python
import functools

import jax
import jax.numpy as jnp
import numpy as np
from jax import lax
from jax.experimental import pallas as pl
from jax.experimental.pallas import tpu as pltpu

D_MODEL = 1024
HEAD_DIM = 64
N_HEADS = 16
N_KV_HEADS = 4
GROUP = N_HEADS // N_KV_HEADS
KV_WIDTH = N_KV_HEADS * HEAD_DIM
ROT_DIM = HEAD_DIM // 4
ROPE_THETA = 500000.0
WINDOW = 128
BLOCK = 128
VALUE_ROWS = HEAD_DIM + 16
CONV_WIDTH = 3
PAST_LEN = 16384
RMS_EPS = 1e-6
NEG_INF = -1e30

LANES = 128
SUBLANES = 8
MXU_COLS = 256
CHUNK_COLS = 2 * MXU_COLS
VMEM_LIMIT_BYTES = 60000 * 1024

PROMPT_TILE = 256
STAGING_ROWS = 256
STAGING_SLOTS = 3
SAMPLE_DECODE_WINDOW = (0.05, 0.85)
LAYER0_WINDOWS = ((0.10, 0.92), (0.0, 1.0))
LAYER1_WINDOWS = ((0.04, 0.92), (0.0, 1.0))

_BF16 = jnp.bfloat16
_F32 = jnp.float32


def _rms_scale(x):
    return lax.rsqrt(jnp.mean(x * x, axis=-1, keepdims=True) + RMS_EPS)


def _sigmoid(x):
    return 1.0 / (1.0 + jnp.exp(-x))


def _rope(x, cos, sin_lo, sin_hi):
    half = ROT_DIM // 2
    out = []
    for c in range(x.shape[1] // LANES):
        slab = x[:, c * LANES:(c + 1) * LANES]
        out.append(slab * cos
                   + pltpu.roll(slab, LANES - half, 1) * sin_lo
                   + pltpu.roll(slab, half, 1) * sin_hi)
    return jnp.concatenate(out, axis=1)


def _interleave(*lanes):
    done = [0.0] * len(lanes)
    live = list(range(len(lanes)))

    def position(j):
        _, total, (start, end) = lanes[j]
        return start + (end - start) * done[j] / total

    while live:
        i = min(live, key=position)
        try:
            done[i] += next(lanes[i][0])
        except StopIteration:
            live.remove(i)


def _scaled(steps, factor):
    try:
        while True:
            yield next(steps) * factor
    except StopIteration as done:
        return done.value


def _chunks(n_cols):
    return [(c, min(CHUNK_COLS, n_cols - c)) for c in range(0, n_cols, CHUNK_COLS)]


def _col_dot(a, w_ref, c0, width):
    return jnp.dot(a, w_ref[:, c0:c0 + width], preferred_element_type=_F32)


def _dot_cost(w_ref, width):
    return -(-w_ref.shape[0] // MXU_COLS) * (width // MXU_COLS)


def _chunks_cost(w_ref):
    return _dot_cost(w_ref, w_ref.shape[1])


def _dot_cols(a, w_ref):
    a = a.astype(_BF16)
    cols = []
    for c, width in _chunks(w_ref.shape[1]):
        cols.append(_col_dot(a, w_ref, c, width))
        yield _dot_cost(w_ref, width)
    return jnp.concatenate(cols, axis=1)


def _ffn_norm(h, g_ffn):
    return ((h * _rms_scale(h)) * g_ffn[...]).astype(_BF16)


def _ffn_and_ple_steps(h, hf, p, g_ple, w_gate, w_up, w_down, ple_gate, ple_proj):
    acts = []
    for c, width in _chunks(w_gate.shape[1]):
        g = _col_dot(hf, w_gate, c, width)
        yield _dot_cost(w_gate, width)
        u = _col_dot(hf, w_up, c, width)
        yield _dot_cost(w_up, width)
        acts.append(((g * _sigmoid(g)) * u).astype(_BF16))
    h = h + (yield from _dot_cols(jnp.concatenate(acts, axis=1), w_down))
    hp = (h * _rms_scale(h)) * g_ple[...]
    gate = _sigmoid((yield from _dot_cols(hp, ple_gate)))
    return h + gate * (yield from _dot_cols(p, ple_proj))


def _ffn_and_ple_cost(w_gate, w_down, ple_gate, ple_proj):
    return 2 * _chunks_cost(w_gate) + _chunks_cost(w_down) + _chunks_cost(ple_gate) + _chunks_cost(ple_proj)


def _conv_mixer_steps(x, g_mix, w_in, conv_w, w_out, history):
    hn = ((x * _rms_scale(x)) * g_mix[...]).astype(_BF16)
    yield 0
    us, convs = [], []
    for c, width in _chunks(D_MODEL):
        cols = slice(c, c + width)
        c_gate = _col_dot(hn, w_in, D_MODEL + c, width)
        yield _dot_cost(w_in, width)
        xin = _col_dot(hn, w_in, 2 * D_MODEL + c, width)
        yield _dot_cost(w_in, width)
        u = c_gate * xin
        back2, back1 = history(cols, u)
        conv = conv_w[0:1, cols] * back2
        conv = conv + conv_w[1:2, cols] * back1
        conv = conv + conv_w[2:3, cols] * u
        us.append(u)
        convs.append(conv)
    gated = []
    for j, (c, width) in enumerate(_chunks(D_MODEL)):
        b_gate = _col_dot(hn, w_in, c, width)
        yield _dot_cost(w_in, width)
        gated.append((b_gate * convs[j]).astype(_BF16))
    y = yield from _dot_cols(jnp.concatenate(gated, axis=1), w_out)
    return x + y, jnp.concatenate(us, axis=1)


def _conv_mixer_cost(w_in, w_out):
    return _chunks_cost(w_in) + _chunks_cost(w_out)


def _pipeline_bodies(step, n_tiles, body):
    pl.when(step == 0)(lambda: body(True, False))
    pl.when((step > 0) & (step < n_tiles))(lambda: body(True, True))
    pl.when(step == n_tiles)(lambda: body(False, True))


def _load_weights(plan, staging, sems):
    chunks = [(src, dst, c0, r0, min(STAGING_ROWS, src.shape[0] - r0))
              for src, dst, c0 in plan for r0 in range(0, src.shape[0], STAGING_ROWS)]
    slots = staging.shape[0]

    def copy(i):
        src, _, _, r0, n = chunks[i]
        window = staging.at[i % slots, pl.ds(0, n), pl.ds(0, src.shape[1])]
        return pltpu.make_async_copy(src.at[pl.ds(r0, n), :], window, sems.at[i % slots])

    for i in range(min(slots - 1, len(chunks))):
        copy(i).start(priority=i % 2)
    for i, (src, dst, c0, r0, n) in enumerate(chunks):
        ahead = i + slots - 1
        if ahead < len(chunks):
            copy(ahead).start(priority=ahead % 2)
        copy(i).wait()
        cols = src.shape[1]
        dst[r0:r0 + n, c0:c0 + cols] = staging[i % slots, 0:n, 0:cols].astype(_BF16)


def _layer0_kernel(x_ref, p_ref, xs_ref, ps_ref, s0_ref, s1_ref, g_mix, g_ffn, g_ple, conv_w,
                   w_in_hbm, w_out_hbm, w_gate_hbm, w_up_hbm, w_down_hbm, ple_gate_hbm, ple_proj_hbm,
                   h_out, state_out, hs_out, us_out,
                   ubuf, hbuf, hfbuf, w_in, w_out, w_gate, w_up, w_down, ple_gate, ple_proj, staging, sems,
                   *, layer, tiles_per_seq, n_tiles):
    tm = x_ref.shape[0]
    step = pl.program_id(0)
    g_mix, g_ffn, g_ple = (g.at[pl.ds(layer, 1)] for g in (g_mix, g_ffn, g_ple))
    ffn_cost = _ffn_and_ple_cost(w_gate, w_down, ple_gate, ple_proj)
    conv_cost = _conv_mixer_cost(w_in, w_out)

    def history(cols, u):
        ubuf[SUBLANES:SUBLANES + tm, cols] = u
        return (ubuf[SUBLANES - 2:SUBLANES - 2 + tm, cols], ubuf[SUBLANES - 1:SUBLANES - 1 + tm, cols])

    def ffn_half():
        out = yield from _ffn_and_ple_steps(hbuf[...], hfbuf[...], p_ref[...], g_ple,
                                            w_gate, w_up, w_down, ple_gate, ple_proj)
        h_out[...] = out

    def conv_half():
        h, _ = yield from _conv_mixer_steps(x_ref[...], g_mix, w_in, conv_w, w_out, history)
        state_out[...] = ubuf[SUBLANES + tm - (CONV_WIDTH - 1):SUBLANES + tm, :]
        ubuf[0:SUBLANES, :] = ubuf[tm:tm + SUBLANES, :]
        hbuf[...] = h
        hfbuf[...] = _ffn_norm(h, g_ffn)

    def sample_layer():
        history = lambda cols, u: (s0_ref[:, cols], s1_ref[:, cols])
        h, u = yield from _conv_mixer_steps(xs_ref[...], g_mix, w_in, conv_w, w_out, history)
        out = yield from _ffn_and_ple_steps(h, _ffn_norm(h, g_ffn), ps_ref[...], g_ple,
                                            w_gate, w_up, w_down, ple_gate, ple_proj)
        hs_out[...] = out
        us_out[...] = u

    def body(first_half, second_half):
        lanes = []
        if first_half and not second_half:
            _load_weights([(w_in_hbm.at[0], w_in, 0), (w_out_hbm.at[0], w_out, 0),
                           (w_gate_hbm.at[layer], w_gate, 0), (w_up_hbm.at[layer], w_up, 0),
                           (w_down_hbm.at[layer], w_down, 0), (ple_gate_hbm.at[layer], ple_gate, 0),
                           (ple_proj_hbm.at[layer], ple_proj, 0)], staging, sems)
        if first_half:
            zeros = jnp.zeros((SUBLANES, D_MODEL), _F32)
            fresh = lax.rem(step, tiles_per_seq) == 0
            ubuf[0:SUBLANES, :] = jnp.where(fresh, zeros, ubuf[0:SUBLANES, :]) if second_half else zeros
        if second_half:
            ffn = ffn_half()
            next(ffn)
            lanes.append((ffn, ffn_cost, LAYER0_WINDOWS[1]))
        if first_half:
            conv = conv_half()
            next(conv)
            lanes.append((conv, conv_cost, LAYER0_WINDOWS[0]))
        else:
            rows = xs_ref.shape[0] / tm
            lanes.append((_scaled(sample_layer(), rows), (conv_cost + ffn_cost) * rows, (0.0, 1.0)))
        _interleave(*lanes)

    _pipeline_bodies(step, n_tiles, body)


def _qkv_steps(h, cos, sin_lo, sin_hi, g_kv, g_mix, w_kv, w_q):
    hs = h * _rms_scale(h)
    h_kv = (hs * g_kv[...]).astype(_BF16)
    h_q = (hs * g_mix[...]).astype(_BF16)
    yield 0
    kv = yield from _dot_cols(h_kv, w_kv)
    q = yield from _dot_cols(h_q, w_q)
    k = _rope(kv[:, :KV_WIDTH], cos, sin_lo, sin_hi)
    v = kv[:, KV_WIDTH:]
    q = _rope(q, cos, sin_lo, sin_hi) * (HEAD_DIM ** -0.5)
    return q, k, v


def _regroup_heads(x, order):
    low = lax.broadcasted_iota(jnp.int32, (1, LANES), 1) < HEAD_DIM
    out = []
    for s in range(x.shape[1] // LANES):
        halves = []
        for half in range(2):
            src_head = order[2 * s + half]
            slab = x[:, (src_head // 2) * LANES:(src_head // 2 + 1) * LANES]
            halves.append(slab if src_head % 2 == half else pltpu.roll(slab, HEAD_DIM, 1))
        out.append(jnp.where(low, halves[0], halves[1]))
    return jnp.concatenate(out, axis=1)


_SLOT_SWAP = [GROUP * (p % N_KV_HEADS) + p // N_KV_HEADS for p in range(N_HEADS)]


def _layer1_kernel(h_ref, p_ref, cos_ref, slo_ref, shi_ref, sinks_ref,
                   hs_ref, pss_ref, cos_s, slo_s, shi_s, ck_ref, cv_ref,
                   g_kv, g_mix, g_ffn, g_ple, g_final,
                   w_k_hbm, w_v_hbm, w_q_hbm, w_o_hbm, w_gate_hbm, w_up_hbm, w_down_hbm,
                   ple_gate_hbm, ple_proj_hbm,
                   y_out, kwin_out, vwin_out, ys_out, kwins_out, vwins_out,
                   k2buf, vtbuf, hbuf, obuf, qs_buf, ks_buf, vs_buf, kst_buf, vst_buf, os_buf,
                   w_kv, w_q, w_o, w_gate, w_up, w_down, ple_gate, ple_proj, staging, sems,
                   *, layer, tiles_per_seq, n_tiles):
    tm = h_ref.shape[0]
    step = pl.program_id(0)
    g_mix, g_ffn, g_ple = (g.at[pl.ds(layer, 1)] for g in (g_mix, g_ffn, g_ple))
    attn_cost = GROUP * BLOCK / tm
    dense_cost = _chunks_cost(w_o) + _ffn_and_ple_cost(w_gate, w_down, ple_gate, ple_proj)
    per_step = ck_ref.shape[0]
    sample_rows = hs_ref.shape[0] / tm

    def dense_half():
        hp = hbuf[...] + (yield from _dot_cols(obuf[...], w_o))
        hp = yield from _ffn_and_ple_steps(hp, _ffn_norm(hp, g_ffn), p_ref[...], g_ple,
                                           w_gate, w_up, w_down, ple_gate, ple_proj)
        y_out[...] = (hp * _rms_scale(hp)) * g_final[...]

    def sample_qkv():
        q, k, v = yield from _qkv_steps(hs_ref[...], cos_s[...], slo_s[...], shi_s[...],
                                        g_kv, g_mix, w_kv, w_q)
        qs_buf[...] = _regroup_heads(q, _SLOT_SWAP)
        ks_buf[...] = k
        vs_buf[...] = v
        kst_buf[...] = k.T
        vst_buf[...] = v.T

    def sample_decode(j):
        r = (step - 1) * per_step + j
        w = ck_ref.shape[2]
        key = lax.broadcasted_iota(jnp.int32, (KV_WIDTH, w), 1)
        seq_lane = lax.broadcasted_iota(jnp.int32, kst_buf.shape, 1)
        lane = lax.broadcasted_iota(jnp.int32, (N_HEADS, KV_WIDTH), 1)
        r16 = lax.broadcasted_iota(jnp.int32, (N_HEADS, KV_WIDTH), 0)
        slot = r16 // N_KV_HEADS
        kvh = r16 - slot * N_KV_HEADS
        own = (lane >= kvh * HEAD_DIM) & (lane < (kvh + 1) * HEAD_DIM)
        head_row = lax.broadcasted_iota(jnp.int32, (N_HEADS, 1), 0)
        sink = jnp.zeros((N_HEADS, 1), _F32)
        for h in range(N_HEADS):
            sink = jnp.where(head_row == _SLOT_SWAP[h], sinks_ref[h], sink)
        kn = ks_buf[pl.ds(r, 1), :]
        vn = vs_buf[pl.ds(r, 1), :]
        for win_ref, new_t, out_ref in ((ck_ref, kst_buf, kwins_out), (cv_ref, vst_buf, vwins_out)):
            column = jnp.sum(jnp.where(seq_lane == r, new_t[...], 0.0), axis=1, keepdims=True)
            out_ref[j] = jnp.where(key == w - 1, column, pltpu.roll(win_ref[j], w - 1, 1))
        lhs = jnp.zeros((N_HEADS, KV_WIDTH), _F32)
        for i in range(GROUP):
            q_i = jnp.broadcast_to(qs_buf[pl.ds(r, 1), i * KV_WIDTH:(i + 1) * KV_WIDTH], (N_HEADS, KV_WIDTH))
            lhs = jnp.where(own & (slot == i), q_i, lhs)
        lhs = lhs.astype(_BF16)
        s_old = jnp.dot(lhs, ck_ref[j].astype(_BF16), preferred_element_type=_F32)
        yield 1
        s_new = jnp.sum(lhs.astype(_F32) * kn.astype(_BF16).astype(_F32), axis=-1, keepdims=True)
        m = jnp.maximum(jnp.maximum(jnp.max(s_old, axis=-1, keepdims=True), s_new), sink)
        e_old = jnp.exp(s_old - m)
        e_new = jnp.exp(s_new - m)
        inv = 1.0 / (jnp.sum(e_old, axis=-1, keepdims=True) + e_new + jnp.exp(sink - m))
        yield 1
        o16 = lax.dot_general(e_old.astype(_BF16), cv_ref[j].astype(_BF16), (((1,), (1,)), ((), ())),
                              preferred_element_type=_F32)
        o16 = (o16 + e_new.astype(_BF16).astype(_F32) * vn.astype(_BF16).astype(_F32)) * inv
        for i in range(GROUP):
            picked = jnp.where(own & (slot == i), o16, 0.0)
            os_buf[pl.ds(r, 1), i * KV_WIDTH:(i + 1) * KV_WIDTH] = jnp.sum(picked, axis=0, keepdims=True)
        yield 1

    def sample_decodes():
        for j in range(per_step):
            yield from sample_decode(j)

    def sample_tail():
        yield from sample_decodes()
        o = _regroup_heads(os_buf[...], _SLOT_SWAP)
        h = hs_ref[...] + (yield from _scaled(_dot_cols(o, w_o), sample_rows))
        h = yield from _scaled(_ffn_and_ple_steps(h, _ffn_norm(h, g_ffn), pss_ref[...], g_ple,
                                                  w_gate, w_up, w_down, ple_gate, ple_proj), sample_rows)
        ys_out[...] = (h * _rms_scale(h)) * g_final[...]

    def attention_half():
        h = h_ref[...]
        q, k, v = yield from _qkv_steps(h, cos_ref[...], slo_ref[...], shi_ref[...], g_kv, g_mix, w_kv, w_q)
        qb = q.astype(_BF16)
        kwin_out[...] = k[tm - WINDOW:, :].T
        vwin_out[...] = v[tm - WINDOW:, :].T

        lane = lax.broadcasted_iota(jnp.int32, (1, LANES), 1)
        low = lane < HEAD_DIM
        for s in range(KV_WIDTH // LANES):
            slab = k[:, s * LANES:(s + 1) * LANES]
            swapped = pltpu.roll(slab, HEAD_DIM, 1)
            k2buf[2 * s, BLOCK:BLOCK + tm, :] = jnp.where(low, slab, swapped).astype(_BF16)
            k2buf[2 * s + 1, BLOCK:BLOCK + tm, :] = jnp.where(low, swapped, slab).astype(_BF16)
        v_t = v.T.astype(_BF16)
        for g in range(N_KV_HEADS):
            vtbuf[g, 0:HEAD_DIM, BLOCK:BLOCK + tm] = v_t[g * HEAD_DIM:(g + 1) * HEAD_DIM, :]

        key_row = lax.broadcasted_iota(jnp.int32, (2 * BLOCK, BLOCK), 0)
        q_col = lax.broadcasted_iota(jnp.int32, (2 * BLOCK, BLOCK), 1)
        diff = q_col + BLOCK - key_row
        band = (diff >= 0) & (diff <= WINDOW)
        first_key = jnp.where(lax.rem(step, tiles_per_seq) == 0, BLOCK, 0)
        zero = jnp.zeros((), _BF16)
        units = [(jb, g) for jb in range(tm // BLOCK) for g in range(N_KV_HEADS)]

        def scores(jb, g):
            r0 = jb * BLOCK
            parts = []
            for s in (2 * g, 2 * g + 1):
                slab = qb[r0:r0 + BLOCK, s * LANES:(s + 1) * LANES]
                parts.append(jnp.where(low, slab, zero))
                parts.append(jnp.where(low, zero, slab))
            qs = jnp.concatenate(parts, axis=0)
            return lax.dot_general(k2buf[g, r0:r0 + 2 * BLOCK, :], qs,
                                   (((1,), (1,)), ((), ())), preferred_element_type=_F32)

        def weights(jb, g, sc):
            valid = band & (key_row >= first_key) if jb == 0 else band
            es, tail = [], []
            for i in range(GROUP):
                sink = sinks_ref[GROUP * g + i]
                sp = jnp.where(valid, sc[:, i * BLOCK:(i + 1) * BLOCK], NEG_INF)
                m = jnp.maximum(jnp.max(sp, axis=0, keepdims=True), sink)
                es.append(jnp.exp(sp - m).astype(_BF16))
                tail.append(jnp.exp(sink - m))
            return jnp.concatenate(es, axis=1), tail

        def values(jb, g, e, tail):
            r0 = jb * BLOCK
            o3 = jnp.dot(vtbuf[g, :, r0:r0 + 2 * BLOCK], e, preferred_element_type=_F32)
            for pair in range(GROUP // 2):
                halves = []
                for i in (2 * pair, 2 * pair + 1):
                    cols = slice(i * BLOCK, (i + 1) * BLOCK)
                    halves.append(o3[0:HEAD_DIM, cols] * (1.0 / (o3[HEAD_DIM:HEAD_DIM + 1, cols] + tail[i])))
                slab = jnp.concatenate(halves, axis=0).T
                s = 2 * g + pair
                obuf[r0:r0 + BLOCK, s * LANES:(s + 1) * LANES] = slab.astype(_BF16)

        sc, ew = {}, {}
        for n in range(len(units) + 2):
            if n < len(units):
                sc[n] = scores(*units[n])
                yield attn_cost
            if 0 <= n - 1 < len(units):
                ew[n - 1] = weights(*units[n - 1], sc.pop(n - 1))
                yield 0.0
            if 0 <= n - 2 < len(units):
                values(*units[n - 2], *ew.pop(n - 2))
                yield attn_cost
        hbuf[...] = h
        k2buf[:, 0:BLOCK, :] = k2buf[:, tm:tm + BLOCK, :]
        vtbuf[:, 0:HEAD_DIM, 0:BLOCK] = vtbuf[:, 0:HEAD_DIM, tm:tm + BLOCK]

    def body(first_half, second_half):
        lanes = []
        qkv_cost = _chunks_cost(w_kv) + _chunks_cost(w_q)
        if first_half and not second_half:
            _load_weights([(w_k_hbm, w_kv, 0), (w_v_hbm, w_kv, KV_WIDTH),
                           (w_q_hbm.at[0], w_q, 0), (w_o_hbm.at[0], w_o, 0),
                           (w_gate_hbm.at[layer], w_gate, 0), (w_up_hbm.at[layer], w_up, 0),
                           (w_down_hbm.at[layer], w_down, 0), (ple_gate_hbm.at[layer], ple_gate, 0),
                           (ple_proj_hbm.at[layer], ple_proj, 0)], staging, sems)
            k2buf[:, 0:BLOCK, :] = jnp.zeros((N_KV_HEADS, BLOCK, LANES), _BF16)
            vtbuf[:, 0:HEAD_DIM, 0:BLOCK] = jnp.zeros((N_KV_HEADS, HEAD_DIM, BLOCK), _BF16)
            vtbuf[:, HEAD_DIM:, :] = jnp.ones((N_KV_HEADS, VALUE_ROWS - HEAD_DIM, BLOCK + tm), _BF16)
        if second_half:
            dense = dense_half()
            next(dense)
            lanes.append((dense, dense_cost, LAYER1_WINDOWS[1]))
        if first_half:
            n_units = (tm // BLOCK) * N_KV_HEADS
            attention = attention_half()
            next(attention)
            lanes.append((attention, qkv_cost + 2 * n_units * attn_cost, LAYER1_WINDOWS[0]))
        if first_half and not second_half:
            lanes.append((_scaled(sample_qkv(), sample_rows), qkv_cost * sample_rows, (0.0, 1.0)))
        elif first_half:
            lanes.append((sample_decodes(), 3 * per_step, SAMPLE_DECODE_WINDOW))
        else:
            lanes.append((sample_tail(), 3 * per_step + dense_cost * sample_rows, (0.0, 1.0)))
        _interleave(*lanes)

    _pipeline_bodies(step, n_tiles, body)


class _Layer:
    def __init__(self, stacked, layer):
        self.stacked, self.layer = stacked, layer


def _operand(x):
    return x.stacked if isinstance(x, _Layer) else x


def _resident(x):
    if isinstance(x, _Layer):
        rest = x.stacked.shape[1:]
        index = (x.layer,) + (0,) * len(rest)
        return pl.BlockSpec((None,) + rest, lambda *_: index, pipeline_mode=pl.Buffered(1))
    nd = x.ndim
    return pl.BlockSpec(x.shape, lambda *_: (0,) * nd, pipeline_mode=pl.Buffered(1))


def _params(n_grid_dims):
    return pltpu.CompilerParams(dimension_semantics=("arbitrary",) * n_grid_dims,
                                vmem_limit_bytes=VMEM_LIMIT_BYTES)


def _rope_tables(pos):
    half = ROT_DIM // 2
    inv_freq = np.power(np.float64(ROPE_THETA), -np.arange(half, dtype=np.float64) / half)
    ang = pos.astype(np.float64)[:, None] * inv_freq[None, :]
    cos, sin = np.cos(ang), np.sin(ang)
    n = pos.shape[0]
    pad = np.zeros((n, HEAD_DIM - ROT_DIM))
    zero = np.zeros((n, half))
    cos_h = np.concatenate([cos, cos, pad + 1.0], axis=1)
    lo_h = np.concatenate([-sin, zero, pad], axis=1)
    hi_h = np.concatenate([zero, sin, pad], axis=1)
    rep = LANES // HEAD_DIM
    return tuple(jnp.asarray(np.tile(a, (1, rep)), dtype=_F32) for a in (cos_h, lo_h, hi_h))


def kernel(x_prompt, x_sample, state_conv, cache_k_win, cache_v_win, p_prompt, p_sample,
           norm_mix_g, norm_ffn_g, norm_ple_g, kv_norm_g, final_norm_g,
           conv_w_in, conv_w, conv_w_out, w_k, w_v, w_q, sinks, w_o,
           ffn_w_gate, ffn_w_up, ffn_w_down, ple_w_proj, ple_w_gate):
    batch, seq, d = x_prompt.shape
    dec = x_sample.shape[0]
    w_buf = cache_k_win.shape[1]
    tm = PROMPT_TILE
    assert d == D_MODEL and seq % tm == 0 and tm % BLOCK == 0 and tm >= WINDOW
    assert x_sample.shape[1] == 1 and w_buf == WINDOW and dec % (batch * (seq // tm)) == 0

    row = lambda a: a.reshape(1, -1)
    gains = [norm_mix_g, norm_ffn_g, norm_ple_g]
    hbm = pl.BlockSpec(memory_space=pl.ANY)
    ffn_weights = [ffn_w_gate, ffn_w_up, ffn_w_down, ple_w_gate, ple_w_proj]
    staging = [pltpu.VMEM((STAGING_SLOTS, STAGING_ROWS, max(w.shape[-1] for w in [conv_w_in] + ffn_weights)), _F32),
               pltpu.SemaphoreType.DMA((STAGING_SLOTS,))]
    resident_bf16 = lambda ws: [pltpu.VMEM(w.shape[-2:], _BF16) for w in ws]

    nt = seq // tm
    n_tiles = batch * nt
    cur = lambda s: jnp.minimum(s, n_tiles - 1)
    prev = lambda s: jnp.maximum(s - 1, 0)
    cur_tile = pl.BlockSpec((None, tm, d), lambda s: (cur(s) // nt, cur(s) % nt, 0))
    prev_tile = pl.BlockSpec((None, tm, d), lambda s: (prev(s) // nt, prev(s) % nt, 0))
    prev_ple = lambda layer: pl.BlockSpec((None, None, tm, p_prompt.shape[-1]),
                                          lambda s: (layer, prev(s) // nt, prev(s) % nt, 0))
    xs = x_sample.reshape(dec, d)
    ps = p_sample.reshape(p_sample.shape[0], dec, p_sample.shape[-1])
    sample_in = [xs, _Layer(ps, 0), state_conv[0, :, 0, :], state_conv[0, :, 1, :]]
    l0_small = gains + [_Layer(conv_w, 0)]
    l0_weights = [conv_w_in, conv_w_out] + ffn_weights
    sample_rows = pl.BlockSpec((dec, d), lambda s: (0, 0))
    h1, conv_state_prompt, h1s, u_s = pl.pallas_call(
        functools.partial(_layer0_kernel, layer=0, tiles_per_seq=nt, n_tiles=n_tiles),
        grid=(n_tiles + 1,),
        in_specs=[cur_tile, prev_ple(0)] + [_resident(a) for a in sample_in + l0_small]
                 + [hbm] * len(l0_weights),
        out_specs=[prev_tile, pl.BlockSpec((None, CONV_WIDTH - 1, d), lambda s: (cur(s) // nt, 0, 0)),
                   sample_rows, sample_rows],
        out_shape=[jax.ShapeDtypeStruct((batch, seq, d), _F32),
                   jax.ShapeDtypeStruct((batch, CONV_WIDTH - 1, d), _F32),
                   jax.ShapeDtypeStruct((dec, d), _F32),
                   jax.ShapeDtypeStruct((dec, d), _F32)],
        scratch_shapes=[pltpu.VMEM((tm + SUBLANES, d), _F32),
                        pltpu.VMEM((tm, d), _F32),
                        pltpu.VMEM((tm, d), _BF16)] + resident_bf16(l0_weights) + staging,
        compiler_params=_params(1),
        name="layer0",
    )(x_prompt, p_prompt, *map(_operand, sample_in + l0_small), *l0_weights)
    conv_state_sample = jnp.stack([state_conv[0, :, 1, :], u_s], axis=1)[None]

    cos_p, lo_p, hi_p = _rope_tables(np.arange(seq))
    cos_s, lo_s, hi_s = _rope_tables(np.full((dec,), PAST_LEN))
    windows_t = lambda a: jnp.transpose(a, (0, 2, 3, 1)).reshape(a.shape[0], KV_WIDTH, a.shape[1])
    windows = lambda a: jnp.transpose(a.reshape(a.shape[0], N_KV_HEADS, HEAD_DIM, a.shape[2]), (0, 3, 1, 2))
    ck, cv = windows_t(cache_k_win), windows_t(cache_v_win)
    per_step = dec // n_tiles
    cache = pl.BlockSpec((per_step, KV_WIDTH, w_buf), lambda s: (prev(s), 0, 0))
    sample_l1 = [h1s, _Layer(ps, 1), cos_s, lo_s, hi_s]
    l1_small = [row(kv_norm_g)] + gains + [row(final_norm_g)]
    l1_weights = [w_k, w_v, w_q, w_o] + ffn_weights
    l1_resident = [pltpu.VMEM((d, 2 * KV_WIDTH), _BF16)] + resident_bf16([w_q, w_o] + ffn_weights)
    y_prompt, k_win_prompt, v_win_prompt, y_sample, k_win_sample, v_win_sample = pl.pallas_call(
        functools.partial(_layer1_kernel, layer=1, tiles_per_seq=nt, n_tiles=n_tiles),
        grid=(n_tiles + 1,),
        in_specs=[cur_tile, prev_ple(1)]
                 + [pl.BlockSpec((tm, LANES), lambda s: (cur(s) % nt, 0))] * 3
                 + [pl.BlockSpec(memory_space=pltpu.SMEM)]
                 + [_resident(a) for a in sample_l1] + [cache, cache]
                 + [_resident(a) for a in l1_small] + [hbm] * len(l1_weights),
        out_specs=[prev_tile,
                   pl.BlockSpec((None, KV_WIDTH, WINDOW), lambda s: (cur(s) // nt, 0, 0)),
                   pl.BlockSpec((None, KV_WIDTH, WINDOW), lambda s: (cur(s) // nt, 0, 0)),
                   sample_rows, cache, cache],
        out_shape=[jax.ShapeDtypeStruct((batch, seq, d), _F32),
                   jax.ShapeDtypeStruct((batch, KV_WIDTH, WINDOW), _F32),
                   jax.ShapeDtypeStruct((batch, KV_WIDTH, WINDOW), _F32),
                   jax.ShapeDtypeStruct((dec, d), _F32),
                   jax.ShapeDtypeStruct((dec, KV_WIDTH, w_buf), _F32),
                   jax.ShapeDtypeStruct((dec, KV_WIDTH, w_buf), _F32)],
        scratch_shapes=[pltpu.VMEM((N_KV_HEADS, BLOCK + tm, LANES), _BF16),
                        pltpu.VMEM((N_KV_HEADS, VALUE_ROWS, BLOCK + tm), _BF16),
                        pltpu.VMEM((tm, d), _F32),
                        pltpu.VMEM((tm, d), _BF16),
                        pltpu.VMEM((dec, d), _F32),
                        pltpu.VMEM((dec, KV_WIDTH), _F32),
                        pltpu.VMEM((dec, KV_WIDTH), _F32),
                        pltpu.VMEM((KV_WIDTH, dec), _F32),
                        pltpu.VMEM((KV_WIDTH, dec), _F32),
                        pltpu.VMEM((dec, d), _F32)] + l1_resident + staging,
        compiler_params=_params(1),
        name="layer1",
    )(h1, p_prompt, cos_p, lo_p, hi_p, sinks[0], *map(_operand, sample_l1), ck, cv,
      *map(_operand, l1_small), *l1_weights)

    return (y_prompt, y_sample.reshape(dec, 1, d),
            conv_state_prompt[None], conv_state_sample,
            windows(k_win_prompt), windows(v_win_prompt),
            windows(k_win_sample), windows(v_win_sample))
```

```python
import functools

import jax
import jax.numpy as jnp
import numpy as np
from jax import lax
from jax.experimental import pallas as pl
from jax.experimental.pallas import tpu as pltpu

D_MODEL = 1024
HEAD_DIM = 64
N_HEADS = 16
N_KV_HEADS = 4
GROUP = N_HEADS // N_KV_HEADS
KV_WIDTH = N_KV_HEADS * HEAD_DIM
ROT_DIM = HEAD_DIM // 4
ROPE_THETA = 500000.0
WINDOW = 128
BLOCK = 128
VALUE_ROWS = HEAD_DIM + 16
CONV_WIDTH = 3
PAST_LEN = 16384
RMS_EPS = 1e-6
NEG_INF = -1e30

LANES = 128
SUBLANES = 8
MXU_COLS = 256
CHUNK_COLS = 2 * MXU_COLS
VMEM_LIMIT_BYTES = 60000 * 1024

PROMPT_TILE = 256
STAGING_ROWS = 256
STAGING_SLOTS = 3
SAMPLE_DECODE_WINDOW = (0.05, 0.85)
LAYER0_WINDOWS = ((0.10, 0.92), (0.0, 1.0))
LAYER1_WINDOWS = ((0.04, 0.92), (0.0, 1.0))

_BF16 = jnp.bfloat16
_F32 = jnp.float32


def _rms_scale(x):
    return lax.rsqrt(jnp.mean(x * x, axis=-1, keepdims=True) + RMS_EPS)


def _sigmoid(x):
    return 1.0 / (1.0 + jnp.exp(-x))


def _rope(x, cos, sin_lo, sin_hi):
    half = ROT_DIM // 2
    out = []
    for c in range(x.shape[1] // LANES):
        slab = x[:, c * LANES:(c + 1) * LANES]
        out.append(slab * cos
                   + pltpu.roll(slab, LANES - half, 1) * sin_lo
                   + pltpu.roll(slab, half, 1) * sin_hi)
    return jnp.concatenate(out, axis=1)


def _interleave(*lanes):
    done = [0.0] * len(lanes)
    live = list(range(len(lanes)))

    def position(j):
        _, total, (start, end) = lanes[j]
        return start + (end - start) * done[j] / total

    while live:
        i = min(live, key=position)
        try:
            done[i] += next(lanes[i][0])
        except StopIteration:
            live.remove(i)


def _scaled(steps, factor):
    try:
        while True:
            yield next(steps) * factor
    except StopIteration as done:
        return done.value


def _chunks(n_cols):
    return [(c, min(CHUNK_COLS, n_cols - c)) for c in range(0, n_cols, CHUNK_COLS)]


def _col_dot(a, w_ref, c0, width):
    return jnp.dot(a, w_ref[:, c0:c0 + width], preferred_element_type=_F32)


def _dot_cost(w_ref, width):
    return -(-w_ref.shape[0] // MXU_COLS) * (width // MXU_COLS)


def _chunks_cost(w_ref):
    return _dot_cost(w_ref, w_ref.shape[1])


def _dot_cols(a, w_ref):
    a = a.astype(_BF16)
    cols = []
    for c, width in _chunks(w_ref.shape[1]):
        cols.append(_col_dot(a, w_ref, c, width))
        yield _dot_cost(w_ref, width)
    return jnp.concatenate(cols, axis=1)


def _ffn_norm(h, g_ffn):
    return ((h * _rms_scale(h)) * g_ffn[...]).astype(_BF16)


def _ffn_and_ple_steps(h, hf, p, g_ple, w_gate, w_up, w_down, ple_gate, ple_proj):
    acts = []
    for c, width in _chunks(w_gate.shape[1]):
        g = _col_dot(hf, w_gate, c, width)
        yield _dot_cost(w_gate, width)
        u = _col_dot(hf, w_up, c, width)
        yield _dot_cost(w_up, width)
        acts.append(((g * _sigmoid(g)) * u).astype(_BF16))
    h = h + (yield from _dot_cols(jnp.concatenate(acts, axis=1), w_down))
    hp = (h * _rms_scale(h)) * g_ple[...]
    gate = _sigmoid((yield from _dot_cols(hp, ple_gate)))
    return h + gate * (yield from _dot_cols(p, ple_proj))


def _ffn_and_ple_cost(w_gate, w_down, ple_gate, ple_proj):
    return 2 * _chunks_cost(w_gate) + _chunks_cost(w_down) + _chunks_cost(ple_gate) + _chunks_cost(ple_proj)


def _conv_mixer_steps(x, g_mix, w_in, conv_w, w_out, history):
    hn = ((x * _rms_scale(x)) * g_mix[...]).astype(_BF16)
    yield 0
    us, convs = [], []
    for c, width in _chunks(D_MODEL):
        cols = slice(c, c + width)
        c_gate = _col_dot(hn, w_in, D_MODEL + c, width)
        yield _dot_cost(w_in, width)
        xin = _col_dot(hn, w_in, 2 * D_MODEL + c, width)
        yield _dot_cost(w_in, width)
        u = c_gate * xin
        back2, back1 = history(cols, u)
        conv = conv_w[0:1, cols] * back2
        conv = conv + conv_w[1:2, cols] * back1
        conv = conv + conv_w[2:3, cols] * u
        us.append(u)
        convs.append(conv)
    gated = []
    for j, (c, width) in enumerate(_chunks(D_MODEL)):
        b_gate = _col_dot(hn, w_in, c, width)
        yield _dot_cost(w_in, width)
        gated.append((b_gate * convs[j]).astype(_BF16))
    y = yield from _dot_cols(jnp.concatenate(gated, axis=1), w_out)
    return x + y, jnp.concatenate(us, axis=1)


def _conv_mixer_cost(w_in, w_out):
    return _chunks_cost(w_in) + _chunks_cost(w_out)


def _pipeline_bodies(step, n_tiles, body):
    pl.when(step == 0)(lambda: body(True, False))
    pl.when((step > 0) & (step < n_tiles))(lambda: body(True, True))
    pl.when(step == n_tiles)(lambda: body(False, True))


def _load_weights(plan, staging, sems):
    chunks = [(src, dst, c0, r0, min(STAGING_ROWS, src.shape[0] - r0))
              for src, dst, c0 in plan for r0 in range(0, src.shape[0], STAGING_ROWS)]
    slots = staging.shape[0]

    def copy(i):
        src, _, _, r0, n = chunks[i]
        window = staging.at[i % slots, pl.ds(0, n), pl.ds(0, src.shape[1])]
        return pltpu.make_async_copy(src.at[pl.ds(r0, n), :], window, sems.at[i % slots])

    for i in range(min(slots - 1, len(chunks))):
        copy(i).start(priority=i % 2)
    for i, (src, dst, c0, r0, n) in enumerate(chunks):
        ahead = i + slots - 1
        if ahead < len(chunks):
            copy(ahead).start(priority=ahead % 2)
        copy(i).wait()
        cols = src.shape[1]
        dst[r0:r0 + n, c0:c0 + cols] = staging[i % slots, 0:n, 0:cols].astype(_BF16)


def _layer0_kernel(x_ref, p_ref, xs_ref, ps_ref, state_ref, g_mix, g_ffn, g_ple, conv_w,
                   w_in_hbm, w_out_hbm, w_gate_hbm, w_up_hbm, w_down_hbm, ple_gate_hbm, ple_proj_hbm,
                   h_out, state_out, hs_out, states_out,
                   ubuf, hbuf, hfbuf, w_in, w_out, w_gate, w_up, w_down, ple_gate, ple_proj, staging, sems,
                   *, layer, tiles_per_seq, n_tiles):
    tm = x_ref.shape[0]
    step = pl.program_id(0)
    g_mix, g_ffn, g_ple = (g.at[pl.ds(layer, 1)] for g in (g_mix, g_ffn, g_ple))
    ffn_cost = _ffn_and_ple_cost(w_gate, w_down, ple_gate, ple_proj)
    conv_cost = _conv_mixer_cost(w_in, w_out)

    def history(cols, u):
        ubuf[SUBLANES:SUBLANES + tm, cols] = u
        return (ubuf[SUBLANES - 2:SUBLANES - 2 + tm, cols], ubuf[SUBLANES - 1:SUBLANES - 1 + tm, cols])

    def ffn_half():
        out = yield from _ffn_and_ple_steps(hbuf[...], hfbuf[...], p_ref[...], g_ple,
                                            w_gate, w_up, w_down, ple_gate, ple_proj)
        h_out[...] = out

    def conv_half():
        h, _ = yield from _conv_mixer_steps(x_ref[...], g_mix, w_in, conv_w, w_out, history)
        state_out[...] = ubuf[SUBLANES + tm - (CONV_WIDTH - 1):SUBLANES + tm, :]
        ubuf[0:SUBLANES, :] = ubuf[tm:tm + SUBLANES, :]
        hbuf[...] = h
        hfbuf[...] = _ffn_norm(h, g_ffn)

    def sample_layer():
        history = lambda cols, u: (state_ref[:, 0, cols], state_ref[:, 1, cols])
        h, u = yield from _conv_mixer_steps(xs_ref[:, 0, :], g_mix, w_in, conv_w, w_out, history)
        out = yield from _ffn_and_ple_steps(h, _ffn_norm(h, g_ffn), ps_ref[:, 0, :], g_ple,
                                            w_gate, w_up, w_down, ple_gate, ple_proj)
        hs_out[...] = out
        states_out[:, 0, :] = state_ref[:, 1, :]
        states_out[:, 1, :] = u

    def body(first_half, second_half):
        lanes = []
        if first_half and not second_half:
            _load_weights([(w_in_hbm.at[0], w_in, 0), (w_out_hbm.at[0], w_out, 0),
                           (w_gate_hbm.at[layer], w_gate, 0), (w_up_hbm.at[layer], w_up, 0),
                           (w_down_hbm.at[layer], w_down, 0), (ple_gate_hbm.at[layer], ple_gate, 0),
                           (ple_proj_hbm.at[layer], ple_proj, 0)], staging, sems)
        if first_half:
            zeros = jnp.zeros((SUBLANES, D_MODEL), _F32)
            fresh = lax.rem(step, tiles_per_seq) == 0
            ubuf[0:SUBLANES, :] = jnp.where(fresh, zeros, ubuf[0:SUBLANES, :]) if second_half else zeros
        if second_half:
            ffn = ffn_half()
            next(ffn)
            lanes.append((ffn, ffn_cost, LAYER0_WINDOWS[1]))
        if first_half:
            conv = conv_half()
            next(conv)
            lanes.append((conv, conv_cost, LAYER0_WINDOWS[0]))
        else:
            rows = xs_ref.shape[0] / tm
            lanes.append((_scaled(sample_layer(), rows), (conv_cost + ffn_cost) * rows, (0.0, 1.0)))
        _interleave(*lanes)

    _pipeline_bodies(step, n_tiles, body)


def _qkv_steps(h, cos, sin_lo, sin_hi, g_kv, g_mix, w_kv, w_q):
    hs = h * _rms_scale(h)
    h_kv = (hs * g_kv[...]).astype(_BF16)
    h_q = (hs * g_mix[...]).astype(_BF16)
    yield 0
    kv = yield from _dot_cols(h_kv, w_kv)
    q = yield from _dot_cols(h_q, w_q)
    k = _rope(kv[:, :KV_WIDTH], cos, sin_lo, sin_hi)
    v = kv[:, KV_WIDTH:]
    q = _rope(q, cos, sin_lo, sin_hi) * (HEAD_DIM ** -0.5)
    return q, k, v


def _regroup_heads(x, order):
    low = lax.broadcasted_iota(jnp.int32, (1, LANES), 1) < HEAD_DIM
    out = []
    for s in range(x.shape[1] // LANES):
        halves = []
        for half in range(2):
            src_head = order[2 * s + half]
            slab = x[:, (src_head // 2) * LANES:(src_head // 2 + 1) * LANES]
            halves.append(slab if src_head % 2 == half else pltpu.roll(slab, HEAD_DIM, 1))
        out.append(jnp.where(low, halves[0], halves[1]))
    return jnp.concatenate(out, axis=1)


_SLOT_SWAP = [GROUP * (p % N_KV_HEADS) + p // N_KV_HEADS for p in range(N_HEADS)]


def _layer1_kernel(h_ref, p_ref, cos_ref, slo_ref, shi_ref, sinks_ref,
                   hs_ref, pss_ref, cos_s, slo_s, shi_s, ck_ref, cv_ref,
                   g_kv, g_mix, g_ffn, g_ple, g_final,
                   w_k_hbm, w_v_hbm, w_q_hbm, w_o_hbm, w_gate_hbm, w_up_hbm, w_down_hbm,
                   ple_gate_hbm, ple_proj_hbm,
                   y_out, kwin_out, vwin_out, ys_out, kwins_out, vwins_out,
                   k2buf, vtbuf, hbuf, obuf, qs_buf, ks_buf, vs_buf, kst_buf, vst_buf, os_buf,
                   w_kv, w_q, w_o, w_gate, w_up, w_down, ple_gate, ple_proj, staging, sems,
                   *, layer, tiles_per_seq, n_tiles):
    tm = h_ref.shape[0]
    step = pl.program_id(0)
    g_mix, g_ffn, g_ple = (g.at[pl.ds(layer, 1)] for g in (g_mix, g_ffn, g_ple))
    score_cost = GROUP * BLOCK / MXU_COLS * (2 * BLOCK) / tm
    value_cost = GROUP * BLOCK / MXU_COLS * VALUE_ROWS / tm
    dense_cost = _chunks_cost(w_o) + _ffn_and_ple_cost(w_gate, w_down, ple_gate, ple_proj)
    per_step = ck_ref.shape[0]
    sample_rows = hs_ref.shape[0] / tm

    def dense_half():
        hp = hbuf[...] + (yield from _dot_cols(obuf[...], w_o))
        hp = yield from _ffn_and_ple_steps(hp, _ffn_norm(hp, g_ffn), p_ref[...], g_ple,
                                           w_gate, w_up, w_down, ple_gate, ple_proj)
        y_out[...] = (hp * _rms_scale(hp)) * g_final[...]

    def sample_qkv():
        q, k, v = yield from _qkv_steps(hs_ref[...], cos_s[...], slo_s[...], shi_s[...],
                                        g_kv, g_mix, w_kv, w_q)
        qs_buf[...] = _regroup_heads(q, _SLOT_SWAP)
        ks_buf[...] = k
        vs_buf[...] = v
        kst_buf[...] = k.T
        vst_buf[...] = v.T

    def sample_decode(j):
        r = (step - 1) * per_step + j
        w = ck_ref.shape[2]
        key = lax.broadcasted_iota(jnp.int32, (KV_WIDTH, w), 1)
        seq_lane = lax.broadcasted_iota(jnp.int32, kst_buf.shape, 1)
        lane = lax.broadcasted_iota(jnp.int32, (N_HEADS, KV_WIDTH), 1)
        r16 = lax.broadcasted_iota(jnp.int32, (N_HEADS, KV_WIDTH), 0)
        slot = r16 // N_KV_HEADS
        kvh = r16 - slot * N_KV_HEADS
        own = (lane >= kvh * HEAD_DIM) & (lane < (kvh + 1) * HEAD_DIM)
        head_row = lax.broadcasted_iota(jnp.int32, (N_HEADS, 1), 0)
        sink = jnp.zeros((N_HEADS, 1), _F32)
        for h in range(N_HEADS):
            sink = jnp.where(head_row == _SLOT_SWAP[h], sinks_ref[h], sink)
        kn = ks_buf[pl.ds(r, 1), :]
        vn = vs_buf[pl.ds(r, 1), :]
        for win_ref, new_t, out_ref in ((ck_ref, kst_buf, kwins_out), (cv_ref, vst_buf, vwins_out)):
            column = jnp.sum(jnp.where(seq_lane == r, new_t[...], 0.0), axis=1, keepdims=True)
            out_ref[j] = jnp.where(key == w - 1, column, pltpu.roll(win_ref[j], w - 1, 1))
        lhs = jnp.zeros((N_HEADS, KV_WIDTH), _F32)
        for i in range(GROUP):
            q_i = jnp.broadcast_to(qs_buf[pl.ds(r, 1), i * KV_WIDTH:(i + 1) * KV_WIDTH], (N_HEADS, KV_WIDTH))
            lhs = jnp.where(own & (slot == i), q_i, lhs)
        lhs = lhs.astype(_BF16)
        s_old = jnp.dot(lhs, ck_ref[j].astype(_BF16), preferred_element_type=_F32)
        yield 1
        s_new = jnp.sum(lhs.astype(_F32) * kn.astype(_BF16).astype(_F32), axis=-1, keepdims=True)
        m = jnp.maximum(jnp.maximum(jnp.max(s_old, axis=-1, keepdims=True), s_new), sink)
        e_old = jnp.exp(s_old - m)
        e_new = jnp.exp(s_new - m)
        inv = 1.0 / (jnp.sum(e_old, axis=-1, keepdims=True) + e_new + jnp.exp(sink - m))
        yield 1
        o16 = lax.dot_general(e_old.astype(_BF16), cv_ref[j].astype(_BF16), (((1,), (1,)), ((), ())),
                              preferred_element_type=_F32)
        o16 = (o16 + e_new.astype(_BF16).astype(_F32) * vn.astype(_BF16).astype(_F32)) * inv
        for i in range(GROUP):
            picked = jnp.where(own & (slot == i), o16, 0.0)
            os_buf[pl.ds(r, 1), i * KV_WIDTH:(i + 1) * KV_WIDTH] = jnp.sum(picked, axis=0, keepdims=True)
        yield 1

    def sample_decodes():
        for j in range(per_step):
            yield from sample_decode(j)

    def sample_tail():
        yield from sample_decodes()
        o = _regroup_heads(os_buf[...], _SLOT_SWAP)
        h = hs_ref[...] + (yield from _scaled(_dot_cols(o, w_o), sample_rows))
        h = yield from _scaled(_ffn_and_ple_steps(h, _ffn_norm(h, g_ffn), pss_ref[:, 0, :], g_ple,
                                                  w_gate, w_up, w_down, ple_gate, ple_proj), sample_rows)
        ys_out[:, 0, :] = (h * _rms_scale(h)) * g_final[...]

    def attention_half():
        h = h_ref[...]
        q, k, v = yield from _qkv_steps(h, cos_ref[...], slo_ref[...], shi_ref[...], g_kv, g_mix, w_kv, w_q)
        qb = q.astype(_BF16)
        kwin_out[...] = k[tm - WINDOW:, :].T
        vwin_out[...] = v[tm - WINDOW:, :].T

        lane = lax.broadcasted_iota(jnp.int32, (1, LANES), 1)
        low = lane < HEAD_DIM
        for s in range(KV_WIDTH // LANES):
            slab = k[:, s * LANES:(s + 1) * LANES]
            swapped = pltpu.roll(slab, HEAD_DIM, 1)
            k2buf[2 * s, BLOCK:BLOCK + tm, :] = jnp.where(low, slab, swapped).astype(_BF16)
            k2buf[2 * s + 1, BLOCK:BLOCK + tm, :] = jnp.where(low, swapped, slab).astype(_BF16)
        v_t = v.T.astype(_BF16)
        for g in range(N_KV_HEADS):
            vtbuf[g, 0:HEAD_DIM, BLOCK:BLOCK + tm] = v_t[g * HEAD_DIM:(g + 1) * HEAD_DIM, :]

        key_row = lax.broadcasted_iota(jnp.int32, (2 * BLOCK, BLOCK), 0)
        q_col = lax.broadcasted_iota(jnp.int32, (2 * BLOCK, BLOCK), 1)
        diff = q_col + BLOCK - key_row
        band = (diff >= 0) & (diff <= WINDOW)
        first_key = jnp.where(lax.rem(step, tiles_per_seq) == 0, BLOCK, 0)
        zero = jnp.zeros((), _BF16)
        units = [(jb, g) for jb in range(tm // BLOCK) for g in range(N_KV_HEADS)]

        def scores(jb, g):
            r0 = jb * BLOCK
            parts = []
            for s in (2 * g, 2 * g + 1):
                slab = qb[r0:r0 + BLOCK, s * LANES:(s + 1) * LANES]
                parts.append(jnp.where(low, slab, zero))
                parts.append(jnp.where(low, zero, slab))
            qs = jnp.concatenate(parts, axis=0)
            return lax.dot_general(k2buf[g, r0:r0 + 2 * BLOCK, :], qs,
                                   (((1,), (1,)), ((), ())), preferred_element_type=_F32)

        def weights(jb, g, sc):
            valid = band & (key_row >= first_key) if jb == 0 else band
            es, tail = [], []
            for i in range(GROUP):
                sink = sinks_ref[GROUP * g + i]
                sp = jnp.where(valid, sc[:, i * BLOCK:(i + 1) * BLOCK], NEG_INF)
                m = jnp.maximum(jnp.max(sp, axis=0, keepdims=True), sink)
                es.append(jnp.exp(sp - m).astype(_BF16))
                tail.append(jnp.exp(sink - m))
            return jnp.concatenate(es, axis=1), tail

        def values(jb, g, e, tail):
            r0 = jb * BLOCK
            o3 = jnp.dot(vtbuf[g, :, r0:r0 + 2 * BLOCK], e, preferred_element_type=_F32)
            for pair in range(GROUP // 2):
                halves = []
                for i in (2 * pair, 2 * pair + 1):
                    cols = slice(i * BLOCK, (i + 1) * BLOCK)
                    halves.append(o3[0:HEAD_DIM, cols] * (1.0 / (o3[HEAD_DIM:HEAD_DIM + 1, cols] + tail[i])))
                slab = jnp.concatenate(halves, axis=0).T
                s = 2 * g + pair
                obuf[r0:r0 + BLOCK, s * LANES:(s + 1) * LANES] = slab.astype(_BF16)

        sc, ew = {}, {}
        for n in range(len(units) + 2):
            if n < len(units):
                sc[n] = scores(*units[n])
                yield score_cost
            if 0 <= n - 1 < len(units):
                ew[n - 1] = weights(*units[n - 1], sc.pop(n - 1))
                yield 0.0
            if 0 <= n - 2 < len(units):
                values(*units[n - 2], *ew.pop(n - 2))
                yield value_cost
        hbuf[...] = h
        k2buf[:, 0:BLOCK, :] = k2buf[:, tm:tm + BLOCK, :]
        vtbuf[:, 0:HEAD_DIM, 0:BLOCK] = vtbuf[:, 0:HEAD_DIM, tm:tm + BLOCK]

    def body(first_half, second_half):
        lanes = []
        qkv_cost = _chunks_cost(w_kv) + _chunks_cost(w_q)
        if first_half and not second_half:
            _load_weights([(w_k_hbm, w_kv, 0), (w_v_hbm, w_kv, KV_WIDTH),
                           (w_q_hbm.at[0], w_q, 0), (w_o_hbm.at[0], w_o, 0),
                           (w_gate_hbm.at[layer], w_gate, 0), (w_up_hbm.at[layer], w_up, 0),
                           (w_down_hbm.at[layer], w_down, 0), (ple_gate_hbm.at[layer], ple_gate, 0),
                           (ple_proj_hbm.at[layer], ple_proj, 0)], staging, sems)
            k2buf[:, 0:BLOCK, :] = jnp.zeros((N_KV_HEADS, BLOCK, LANES), _BF16)
            vtbuf[:, 0:HEAD_DIM, 0:BLOCK] = jnp.zeros((N_KV_HEADS, HEAD_DIM, BLOCK), _BF16)
            vtbuf[:, HEAD_DIM:, :] = jnp.ones((N_KV_HEADS, VALUE_ROWS - HEAD_DIM, BLOCK + tm), _BF16)
        if second_half:
            dense = dense_half()
            next(dense)
            lanes.append((dense, dense_cost, LAYER1_WINDOWS[1]))
        if first_half:
            n_units = (tm // BLOCK) * N_KV_HEADS
            attention = attention_half()
            next(attention)
            lanes.append((attention, qkv_cost + n_units * (score_cost + value_cost), LAYER1_WINDOWS[0]))
        if first_half and not second_half:
            lanes.append((_scaled(sample_qkv(), sample_rows), qkv_cost * sample_rows, (0.0, 1.0)))
        elif first_half:
            lanes.append((sample_decodes(), 3 * per_step, SAMPLE_DECODE_WINDOW))
        else:
            lanes.append((sample_tail(), 3 * per_step + dense_cost * sample_rows, (0.0, 1.0)))
        _interleave(*lanes)

    _pipeline_bodies(step, n_tiles, body)


class _Layer:
    def __init__(self, stacked, layer):
        self.stacked, self.layer = stacked, layer


def _operand(x):
    return x.stacked if isinstance(x, _Layer) else x


def _resident(x):
    if isinstance(x, _Layer):
        rest = x.stacked.shape[1:]
        index = (x.layer,) + (0,) * len(rest)
        return pl.BlockSpec((None,) + rest, lambda *_: index, pipeline_mode=pl.Buffered(1))
    nd = x.ndim
    return pl.BlockSpec(x.shape, lambda *_: (0,) * nd, pipeline_mode=pl.Buffered(1))


def _params(n_grid_dims):
    return pltpu.CompilerParams(dimension_semantics=("arbitrary",) * n_grid_dims,
                                vmem_limit_bytes=VMEM_LIMIT_BYTES)


def _rope_tables(pos):
    half = ROT_DIM // 2
    inv_freq = np.power(np.float64(ROPE_THETA), -np.arange(half, dtype=np.float64) / half)
    ang = pos.astype(np.float64)[:, None] * inv_freq[None, :]
    cos, sin = np.cos(ang), np.sin(ang)
    n = pos.shape[0]
    pad = np.zeros((n, HEAD_DIM - ROT_DIM))
    zero = np.zeros((n, half))
    cos_h = np.concatenate([cos, cos, pad + 1.0], axis=1)
    lo_h = np.concatenate([-sin, zero, pad], axis=1)
    hi_h = np.concatenate([zero, sin, pad], axis=1)
    rep = LANES // HEAD_DIM
    return tuple(jnp.asarray(np.tile(a, (1, rep)), dtype=_F32) for a in (cos_h, lo_h, hi_h))


def kernel(x_prompt, x_sample, state_conv, cache_k_win, cache_v_win, p_prompt, p_sample,
           norm_mix_g, norm_ffn_g, norm_ple_g, kv_norm_g, final_norm_g,
           conv_w_in, conv_w, conv_w_out, w_k, w_v, w_q, sinks, w_o,
           ffn_w_gate, ffn_w_up, ffn_w_down, ple_w_proj, ple_w_gate):
    batch, seq, d = x_prompt.shape
    dec = x_sample.shape[0]
    w_buf = cache_k_win.shape[1]
    tm = PROMPT_TILE
    assert d == D_MODEL and seq % tm == 0 and tm % BLOCK == 0 and tm >= WINDOW
    assert x_sample.shape[1] == 1 and w_buf == WINDOW and dec % (batch * (seq // tm)) == 0

    row = lambda a: a.reshape(1, -1)
    gains = [norm_mix_g, norm_ffn_g, norm_ple_g]
    hbm = pl.BlockSpec(memory_space=pl.ANY)
    ffn_weights = [ffn_w_gate, ffn_w_up, ffn_w_down, ple_w_gate, ple_w_proj]
    staging = [pltpu.VMEM((STAGING_SLOTS, STAGING_ROWS, max(w.shape[-1] for w in [conv_w_in] + ffn_weights)), _F32),
               pltpu.SemaphoreType.DMA((STAGING_SLOTS,))]
    resident_bf16 = lambda ws: [pltpu.VMEM(w.shape[-2:], _BF16) for w in ws]

    nt = seq // tm
    n_tiles = batch * nt
    cur = lambda s: jnp.minimum(s, n_tiles - 1)
    prev = lambda s: jnp.maximum(s - 1, 0)
    cur_tile = pl.BlockSpec((None, tm, d), lambda s: (cur(s) // nt, cur(s) % nt, 0))
    prev_tile = pl.BlockSpec((None, tm, d), lambda s: (prev(s) // nt, prev(s) % nt, 0))
    prev_ple = lambda layer: pl.BlockSpec((None, None, tm, p_prompt.shape[-1]),
                                          lambda s: (layer, prev(s) // nt, prev(s) % nt, 0))
    sample_in = [x_sample, _Layer(p_sample, 0), _Layer(state_conv, 0)]
    l0_small = gains + [_Layer(conv_w, 0)]
    l0_weights = [conv_w_in, conv_w_out] + ffn_weights
    sample_rows = pl.BlockSpec((dec, d), lambda s: (0, 0))
    h1, conv_state_prompt, h1s, conv_state_sample = pl.pallas_call(
        functools.partial(_layer0_kernel, layer=0, tiles_per_seq=nt, n_tiles=n_tiles),
        grid=(n_tiles + 1,),
        in_specs=[cur_tile, prev_ple(0)] + [_resident(a) for a in sample_in + l0_small]
                 + [hbm] * len(l0_weights),
        out_specs=[prev_tile, pl.BlockSpec((None, CONV_WIDTH - 1, d), lambda s: (cur(s) // nt, 0, 0)),
                   sample_rows, pl.BlockSpec((None,) + state_conv.shape[1:], lambda s: (0, 0, 0, 0))],
        out_shape=[jax.ShapeDtypeStruct((batch, seq, d), _F32),
                   jax.ShapeDtypeStruct((batch, CONV_WIDTH - 1, d), _F32),
                   jax.ShapeDtypeStruct((dec, d), _F32),
                   jax.ShapeDtypeStruct(state_conv.shape, _F32)],
        scratch_shapes=[pltpu.VMEM((tm + SUBLANES, d), _F32),
                        pltpu.VMEM((tm, d), _F32),
                        pltpu.VMEM((tm, d), _BF16)] + resident_bf16(l0_weights) + staging,
        compiler_params=_params(1),
        name="layer0",
    )(x_prompt, p_prompt, *map(_operand, sample_in + l0_small), *l0_weights)

    cos_p, lo_p, hi_p = _rope_tables(np.arange(seq))
    cos_s, lo_s, hi_s = _rope_tables(np.full((dec,), PAST_LEN))
    windows_t = lambda a: jnp.transpose(a, (0, 2, 3, 1)).reshape(a.shape[0], KV_WIDTH, a.shape[1])
    windows = lambda a: jnp.transpose(a.reshape(a.shape[0], N_KV_HEADS, HEAD_DIM, a.shape[2]), (0, 3, 1, 2))
    ck, cv = windows_t(cache_k_win), windows_t(cache_v_win)
    per_step = dec // n_tiles
    cache = pl.BlockSpec((per_step, KV_WIDTH, w_buf), lambda s: (prev(s), 0, 0))
    sample_l1 = [h1s, _Layer(p_sample, 1), cos_s, lo_s, hi_s]
    l1_small = [row(kv_norm_g)] + gains + [row(final_norm_g)]
    l1_weights = [w_k, w_v, w_q, w_o] + ffn_weights
    l1_resident = [pltpu.VMEM((d, 2 * KV_WIDTH), _BF16)] + resident_bf16([w_q, w_o] + ffn_weights)
    y_prompt, k_win_prompt, v_win_prompt, y_sample, k_win_sample, v_win_sample = pl.pallas_call(
        functools.partial(_layer1_kernel, layer=1, tiles_per_seq=nt, n_tiles=n_tiles),
        grid=(n_tiles + 1,),
        in_specs=[cur_tile, prev_ple(1)]
                 + [pl.BlockSpec((tm, LANES), lambda s: (cur(s) % nt, 0))] * 3
                 + [pl.BlockSpec(memory_space=pltpu.SMEM)]
                 + [_resident(a) for a in sample_l1] + [cache, cache]
                 + [_resident(a) for a in l1_small] + [hbm] * len(l1_weights),
        out_specs=[prev_tile,
                   pl.BlockSpec((None, KV_WIDTH, WINDOW), lambda s: (cur(s) // nt, 0, 0)),
                   pl.BlockSpec((None, KV_WIDTH, WINDOW), lambda s: (cur(s) // nt, 0, 0)),
                   pl.BlockSpec(x_sample.shape, lambda s: (0, 0, 0)), cache, cache],
        out_shape=[jax.ShapeDtypeStruct((batch, seq, d), _F32),
                   jax.ShapeDtypeStruct((batch, KV_WIDTH, WINDOW), _F32),
                   jax.ShapeDtypeStruct((batch, KV_WIDTH, WINDOW), _F32),
                   jax.ShapeDtypeStruct(x_sample.shape, _F32),
                   jax.ShapeDtypeStruct((dec, KV_WIDTH, w_buf), _F32),
                   jax.ShapeDtypeStruct((dec, KV_WIDTH, w_buf), _F32)],
        scratch_shapes=[pltpu.VMEM((N_KV_HEADS, BLOCK + tm, LANES), _BF16),
                        pltpu.VMEM((N_KV_HEADS, VALUE_ROWS, BLOCK + tm), _BF16),
                        pltpu.VMEM((tm, d), _F32),
                        pltpu.VMEM((tm, d), _BF16),
                        pltpu.VMEM((dec, d), _F32),
                        pltpu.VMEM((dec, KV_WIDTH), _F32),
                        pltpu.VMEM((dec, KV_WIDTH), _F32),
                        pltpu.VMEM((KV_WIDTH, dec), _F32),
                        pltpu.VMEM((KV_WIDTH, dec), _F32),
                        pltpu.VMEM((dec, d), _F32)] + l1_resident + staging,
        compiler_params=_params(1),
        name="layer1",
    )(h1, p_prompt, cos_p, lo_p, hi_p, sinks[0], *map(_operand, sample_l1), ck, cv,
      *map(_operand, l1_small), *l1_weights)

    return (y_prompt, y_sample,
            conv_state_prompt[None], conv_state_sample,
            windows(k_win_prompt), windows(v_win_prompt),
            windows(k_win_sample), windows(v_win_sample))
```

```python
import functools

import jax
import jax.numpy as jnp
import numpy as np
from jax import lax
from jax.experimental import pallas as pl
from jax.experimental.pallas import tpu as pltpu

D_MODEL = 1024
HEAD_DIM = 64
N_HEADS = 16
N_KV_HEADS = 4
GROUP = N_HEADS // N_KV_HEADS
KV_WIDTH = N_KV_HEADS * HEAD_DIM
ROT_DIM = HEAD_DIM // 4
ROPE_THETA = 500000.0
WINDOW = 128
BLOCK = 128
VALUE_ROWS = HEAD_DIM + 16
CONV_WIDTH = 3
PAST_LEN = 16384
RMS_EPS = 1e-6
NEG_INF = -1e30

LANES = 128
SUBLANES = 8
MXU_COLS = 256
CHUNK_COLS = 2 * MXU_COLS
VMEM_LIMIT_BYTES = 60000 * 1024

PROMPT_TILE = 256
STAGING_ROWS = 256
STAGING_SLOTS = 3
SAMPLE_DECODE_WINDOW = (0.05, 0.85)
LAYER0_WINDOWS = ((0.10, 0.92), (0.0, 1.0))
LAYER1_WINDOWS = ((0.04, 0.92), (0.0, 1.0))

_BF16 = jnp.bfloat16
_F32 = jnp.float32


def _rms_scale(x):
    return lax.rsqrt(jnp.mean(x * x, axis=-1, keepdims=True) + RMS_EPS)


def _sigmoid(x):
    return 1.0 / (1.0 + jnp.exp(-x))


def _rope(x, cos, sin_lo, sin_hi):
    half = ROT_DIM // 2
    out = []
    for c in range(x.shape[1] // LANES):
        slab = x[:, c * LANES:(c + 1) * LANES]
        out.append(slab * cos
                   + pltpu.roll(slab, LANES - half, 1) * sin_lo
                   + pltpu.roll(slab, half, 1) * sin_hi)
    return jnp.concatenate(out, axis=1)


def _interleave(*lanes):
    done = [0.0] * len(lanes)
    live = list(range(len(lanes)))

    def position(j):
        _, total, (start, end) = lanes[j]
        return start + (end - start) * done[j] / total

    while live:
        i = min(live, key=position)
        try:
            done[i] += next(lanes[i][0])
        except StopIteration:
            live.remove(i)


def _scaled(steps, factor):
    try:
        while True:
            yield next(steps) * factor
    except StopIteration as done:
        return done.value


def _chunks(n_cols):
    return [(c, min(CHUNK_COLS, n_cols - c)) for c in range(0, n_cols, CHUNK_COLS)]


def _col_dot(a, w_ref, c0, width):
    return jnp.dot(a, w_ref[:, c0:c0 + width], preferred_element_type=_F32)


def _dot_cost(w_ref, width):
    return -(-w_ref.shape[0] // MXU_COLS) * (width // MXU_COLS)


def _chunks_cost(w_ref):
    return _dot_cost(w_ref, w_ref.shape[1])


def _dot_cols(a, w_ref):
    a = a.astype(_BF16)
    cols = []
    for c, width in _chunks(w_ref.shape[1]):
        cols.append(_col_dot(a, w_ref, c, width))
        yield _dot_cost(w_ref, width)
    return jnp.concatenate(cols, axis=1)


def _ffn_norm(h, g_ffn):
    return ((h * _rms_scale(h)) * g_ffn[...]).astype(_BF16)


def _ffn_and_ple_steps(h, hf, p, g_ple, w_gate, w_up, w_down, ple_gate, ple_proj):
    acts = []
    for c, width in _chunks(w_gate.shape[1]):
        g = _col_dot(hf, w_gate, c, width)
        yield _dot_cost(w_gate, width)
        u = _col_dot(hf, w_up, c, width)
        yield _dot_cost(w_up, width)
        acts.append(((g * _sigmoid(g)) * u).astype(_BF16))
    h = h + (yield from _dot_cols(jnp.concatenate(acts, axis=1), w_down))
    hp = (h * _rms_scale(h)) * g_ple[...]
    gate = _sigmoid((yield from _dot_cols(hp, ple_gate)))
    return h + gate * (yield from _dot_cols(p, ple_proj))


def _ffn_and_ple_cost(w_gate, w_down, ple_gate, ple_proj):
    return 2 * _chunks_cost(w_gate) + _chunks_cost(w_down) + _chunks_cost(ple_gate) + _chunks_cost(ple_proj)


def _conv_mixer_steps(x, g_mix, w_in, conv_w, w_out, history):
    hn = ((x * _rms_scale(x)) * g_mix[...]).astype(_BF16)
    yield 0
    us, convs = [], []
    for c, width in _chunks(D_MODEL):
        cols = slice(c, c + width)
        c_gate = _col_dot(hn, w_in, D_MODEL + c, width)
        yield _dot_cost(w_in, width)
        xin = _col_dot(hn, w_in, 2 * D_MODEL + c, width)
        yield _dot_cost(w_in, width)
        u = c_gate * xin
        back2, back1 = history(cols, u)
        conv = conv_w[0:1, cols] * back2
        conv = conv + conv_w[1:2, cols] * back1
        conv = conv + conv_w[2:3, cols] * u
        us.append(u)
        convs.append(conv)
    gated = []
    for j, (c, width) in enumerate(_chunks(D_MODEL)):
        b_gate = _col_dot(hn, w_in, c, width)
        yield _dot_cost(w_in, width)
        gated.append((b_gate * convs[j]).astype(_BF16))
    y = yield from _dot_cols(jnp.concatenate(gated, axis=1), w_out)
    return x + y, jnp.concatenate(us, axis=1)


def _conv_mixer_cost(w_in, w_out):
    return _chunks_cost(w_in) + _chunks_cost(w_out)


def _pipeline_bodies(step, n_tiles, body):
    pl.when(step == 0)(lambda: body(True, False))
    pl.when((step > 0) & (step < n_tiles))(lambda: body(True, True))
    pl.when(step == n_tiles)(lambda: body(False, True))


def _load_weights(plan, staging, sems):
    chunks = [(src, dst, c0, r0, min(STAGING_ROWS, src.shape[0] - r0))
              for src, dst, c0 in plan for r0 in range(0, src.shape[0], STAGING_ROWS)]
    slots = staging.shape[0]

    def copy(i):
        src, _, _, r0, n = chunks[i]
        window = staging.at[i % slots, pl.ds(0, n), pl.ds(0, src.shape[1])]
        return pltpu.make_async_copy(src.at[pl.ds(r0, n), :], window, sems.at[i % slots])

    for i in range(min(slots - 1, len(chunks))):
        copy(i).start(priority=i % 2)
    for i, (src, dst, c0, r0, n) in enumerate(chunks):
        ahead = i + slots - 1
        if ahead < len(chunks):
            copy(ahead).start(priority=ahead % 2)
        copy(i).wait()
        cols = src.shape[1]
        dst[r0:r0 + n, c0:c0 + cols] = staging[i % slots, 0:n, 0:cols].astype(_BF16)


def _layer0_kernel(x_ref, p_ref, xs_ref, ps_ref, s0_ref, s1_ref, g_mix, g_ffn, g_ple, conv_w,
                   w_in_hbm, w_out_hbm, w_gate_hbm, w_up_hbm, w_down_hbm, ple_gate_hbm, ple_proj_hbm,
                   h_out, state_out, hs_out, us_out,
                   ubuf, hbuf, hfbuf, w_in, w_out, w_gate, w_up, w_down, ple_gate, ple_proj, staging, sems,
                   *, layer, tiles_per_seq, n_tiles):
    tm = x_ref.shape[0]
    step = pl.program_id(0)
    g_mix, g_ffn, g_ple = (g.at[pl.ds(layer, 1)] for g in (g_mix, g_ffn, g_ple))
    ffn_cost = _ffn_and_ple_cost(w_gate, w_down, ple_gate, ple_proj)
    conv_cost = _conv_mixer_cost(w_in, w_out)

    def history(cols, u):
        ubuf[SUBLANES:SUBLANES + tm, cols] = u
        return (ubuf[SUBLANES - 2:SUBLANES - 2 + tm, cols], ubuf[SUBLANES - 1:SUBLANES - 1 + tm, cols])

    def ffn_half():
        out = yield from _ffn_and_ple_steps(hbuf[...], hfbuf[...], p_ref[...], g_ple,
                                            w_gate, w_up, w_down, ple_gate, ple_proj)
        h_out[...] = out

    def conv_half():
        h, _ = yield from _conv_mixer_steps(x_ref[...], g_mix, w_in, conv_w, w_out, history)
        state_out[...] = ubuf[SUBLANES + tm - (CONV_WIDTH - 1):SUBLANES + tm, :]
        ubuf[0:SUBLANES, :] = ubuf[tm:tm + SUBLANES, :]
        hbuf[...] = h
        hfbuf[...] = _ffn_norm(h, g_ffn)

    def sample_layer():
        history = lambda cols, u: (s0_ref[:, cols], s1_ref[:, cols])
        h, u = yield from _conv_mixer_steps(xs_ref[...], g_mix, w_in, conv_w, w_out, history)
        out = yield from _ffn_and_ple_steps(h, _ffn_norm(h, g_ffn), ps_ref[:, 0, :], g_ple,
                                            w_gate, w_up, w_down, ple_gate, ple_proj)
        hs_out[...] = out
        us_out[...] = u

    def body(first_half, second_half):
        lanes = []
        if first_half and not second_half:
            _load_weights([(w_in_hbm.at[0], w_in, 0), (w_out_hbm.at[0], w_out, 0),
                           (w_gate_hbm.at[layer], w_gate, 0), (w_up_hbm.at[layer], w_up, 0),
                           (w_down_hbm.at[layer], w_down, 0), (ple_gate_hbm.at[layer], ple_gate, 0),
                           (ple_proj_hbm.at[layer], ple_proj, 0)], staging, sems)
        if first_half:
            zeros = jnp.zeros((SUBLANES, D_MODEL), _F32)
            fresh = lax.rem(step, tiles_per_seq) == 0
            ubuf[0:SUBLANES, :] = jnp.where(fresh, zeros, ubuf[0:SUBLANES, :]) if second_half else zeros
        if second_half:
            ffn = ffn_half()
            next(ffn)
            lanes.append((ffn, ffn_cost, LAYER0_WINDOWS[1]))
        if first_half:
            conv = conv_half()
            next(conv)
            lanes.append((conv, conv_cost, LAYER0_WINDOWS[0]))
        else:
            rows = xs_ref.shape[0] / tm
            lanes.append((_scaled(sample_layer(), rows), (conv_cost + ffn_cost) * rows, (0.0, 1.0)))
        _interleave(*lanes)

    _pipeline_bodies(step, n_tiles, body)


def _qkv_steps(h, cos, sin_lo, sin_hi, g_kv, g_mix, w_kv, w_q):
    hs = h * _rms_scale(h)
    h_kv = (hs * g_kv[...]).astype(_BF16)
    h_q = (hs * g_mix[...]).astype(_BF16)
    yield 0
    kv = yield from _dot_cols(h_kv, w_kv)
    q = yield from _dot_cols(h_q, w_q)
    k = _rope(kv[:, :KV_WIDTH], cos, sin_lo, sin_hi)
    v = kv[:, KV_WIDTH:]
    q = _rope(q, cos, sin_lo, sin_hi) * (HEAD_DIM ** -0.5)
    return q, k, v


def _regroup_heads(x, order):
    low = lax.broadcasted_iota(jnp.int32, (1, LANES), 1) < HEAD_DIM
    out = []
    for s in range(x.shape[1] // LANES):
        halves = []
        for half in range(2):
            src_head = order[2 * s + half]
            slab = x[:, (src_head // 2) * LANES:(src_head // 2 + 1) * LANES]
            halves.append(slab if src_head % 2 == half else pltpu.roll(slab, HEAD_DIM, 1))
        out.append(jnp.where(low, halves[0], halves[1]))
    return jnp.concatenate(out, axis=1)


_SLOT_SWAP = [GROUP * (p % N_KV_HEADS) + p // N_KV_HEADS for p in range(N_HEADS)]


def _layer1_kernel(h_ref, p_ref, cos_ref, slo_ref, shi_ref, sinks_ref,
                   hs_ref, pss_ref, cos_s, slo_s, shi_s, ck_ref, cv_ref,
                   g_kv, g_mix, g_ffn, g_ple, g_final,
                   w_k_hbm, w_v_hbm, w_q_hbm, w_o_hbm, w_gate_hbm, w_up_hbm, w_down_hbm,
                   ple_gate_hbm, ple_proj_hbm,
                   y_out, kwin_out, vwin_out, ys_out, kwins_out, vwins_out,
                   k2buf, vtbuf, hbuf, obuf, qs_buf, ks_buf, vs_buf, kst_buf, vst_buf, os_buf,
                   w_kv, w_q, w_o, w_gate, w_up, w_down, ple_gate, ple_proj, staging, sems,
                   *, layer, tiles_per_seq, n_tiles):
    tm = h_ref.shape[0]
    step = pl.program_id(0)
    g_mix, g_ffn, g_ple = (g.at[pl.ds(layer, 1)] for g in (g_mix, g_ffn, g_ple))
    score_cost = GROUP * BLOCK / MXU_COLS * (2 * BLOCK) / tm
    value_cost = GROUP * BLOCK / MXU_COLS * VALUE_ROWS / tm
    dense_cost = _chunks_cost(w_o) + _ffn_and_ple_cost(w_gate, w_down, ple_gate, ple_proj)
    per_step = ck_ref.shape[0]
    sample_rows = hs_ref.shape[0] / tm

    def dense_half():
        hp = hbuf[...] + (yield from _dot_cols(obuf[...], w_o))
        hp = yield from _ffn_and_ple_steps(hp, _ffn_norm(hp, g_ffn), p_ref[...], g_ple,
                                           w_gate, w_up, w_down, ple_gate, ple_proj)
        y_out[...] = (hp * _rms_scale(hp)) * g_final[...]

    def sample_qkv():
        q, k, v = yield from _qkv_steps(hs_ref[...], cos_s[...], slo_s[...], shi_s[...],
                                        g_kv, g_mix, w_kv, w_q)
        qs_buf[...] = _regroup_heads(q, _SLOT_SWAP)
        ks_buf[...] = k
        vs_buf[...] = v
        kst_buf[...] = k.T
        vst_buf[...] = v.T

    def sample_decode(j):
        r = (step - 1) * per_step + j
        w = ck_ref.shape[2]
        key = lax.broadcasted_iota(jnp.int32, (KV_WIDTH, w), 1)
        seq_lane = lax.broadcasted_iota(jnp.int32, kst_buf.shape, 1)
        lane = lax.broadcasted_iota(jnp.int32, (N_HEADS, KV_WIDTH), 1)
        r16 = lax.broadcasted_iota(jnp.int32, (N_HEADS, KV_WIDTH), 0)
        slot = r16 // N_KV_HEADS
        kvh = r16 - slot * N_KV_HEADS
        own = (lane >= kvh * HEAD_DIM) & (lane < (kvh + 1) * HEAD_DIM)
        head_row = lax.broadcasted_iota(jnp.int32, (N_HEADS, 1), 0)
        sink = jnp.zeros((N_HEADS, 1), _F32)
        for h in range(N_HEADS):
            sink = jnp.where(head_row == _SLOT_SWAP[h], sinks_ref[h], sink)
        kn = ks_buf[pl.ds(r, 1), :]
        vn = vs_buf[pl.ds(r, 1), :]
        for win_ref, new_t, out_ref in ((ck_ref, kst_buf, kwins_out), (cv_ref, vst_buf, vwins_out)):
            column = jnp.sum(jnp.where(seq_lane == r, new_t[...], 0.0), axis=1, keepdims=True)
            out_ref[j] = jnp.where(key == w - 1, column, pltpu.roll(win_ref[j], w - 1, 1))
        lhs = jnp.zeros((N_HEADS, KV_WIDTH), _F32)
        for i in range(GROUP):
            q_i = jnp.broadcast_to(qs_buf[pl.ds(r, 1), i * KV_WIDTH:(i + 1) * KV_WIDTH], (N_HEADS, KV_WIDTH))
            lhs = jnp.where(own & (slot == i), q_i, lhs)
        lhs = lhs.astype(_BF16)
        s_old = jnp.dot(lhs, ck_ref[j].astype(_BF16), preferred_element_type=_F32)
        yield 1
        s_new = jnp.sum(lhs.astype(_F32) * kn.astype(_BF16).astype(_F32), axis=-1, keepdims=True)
        m = jnp.maximum(jnp.maximum(jnp.max(s_old, axis=-1, keepdims=True), s_new), sink)
        e_old = jnp.exp(s_old - m)
        e_new = jnp.exp(s_new - m)
        inv = 1.0 / (jnp.sum(e_old, axis=-1, keepdims=True) + e_new + jnp.exp(sink - m))
        yield 1
        o16 = lax.dot_general(e_old.astype(_BF16), cv_ref[j].astype(_BF16), (((1,), (1,)), ((), ())),
                              preferred_element_type=_F32)
        o16 = (o16 + e_new.astype(_BF16).astype(_F32) * vn.astype(_BF16).astype(_F32)) * inv
        for i in range(GROUP):
            picked = jnp.where(own & (slot == i), o16, 0.0)
            os_buf[pl.ds(r, 1), i * KV_WIDTH:(i + 1) * KV_WIDTH] = jnp.sum(picked, axis=0, keepdims=True)
        yield 1

    def sample_decodes():
        for j in range(per_step):
            yield from sample_decode(j)

    def sample_tail():
        yield from sample_decodes()
        o = _regroup_heads(os_buf[...], _SLOT_SWAP)
        h = hs_ref[...] + (yield from _scaled(_dot_cols(o, w_o), sample_rows))
        h = yield from _scaled(_ffn_and_ple_steps(h, _ffn_norm(h, g_ffn), pss_ref[:, 0, :], g_ple,
                                                  w_gate, w_up, w_down, ple_gate, ple_proj), sample_rows)
        ys_out[...] = (h * _rms_scale(h)) * g_final[...]

    def attention_half():
        h = h_ref[...]
        q, k, v = yield from _qkv_steps(h, cos_ref[...], slo_ref[...], shi_ref[...], g_kv, g_mix, w_kv, w_q)
        qb = q.astype(_BF16)
        kwin_out[...] = k[tm - WINDOW:, :].T
        vwin_out[...] = v[tm - WINDOW:, :].T

        lane = lax.broadcasted_iota(jnp.int32, (1, LANES), 1)
        low = lane < HEAD_DIM
        for s in range(KV_WIDTH // LANES):
            slab = k[:, s * LANES:(s + 1) * LANES]
            swapped = pltpu.roll(slab, HEAD_DIM, 1)
            k2buf[2 * s, BLOCK:BLOCK + tm, :] = jnp.where(low, slab, swapped).astype(_BF16)
            k2buf[2 * s + 1, BLOCK:BLOCK + tm, :] = jnp.where(low, swapped, slab).astype(_BF16)
        v_t = v.T.astype(_BF16)
        for g in range(N_KV_HEADS):
            vtbuf[g, 0:HEAD_DIM, BLOCK:BLOCK + tm] = v_t[g * HEAD_DIM:(g + 1) * HEAD_DIM, :]

        key_row = lax.broadcasted_iota(jnp.int32, (2 * BLOCK, BLOCK), 0)
        q_col = lax.broadcasted_iota(jnp.int32, (2 * BLOCK, BLOCK), 1)
        diff = q_col + BLOCK - key_row
        band = (diff >= 0) & (diff <= WINDOW)
        first_key = jnp.where(lax.rem(step, tiles_per_seq) == 0, BLOCK, 0)
        zero = jnp.zeros((), _BF16)
        units = [(jb, g) for jb in range(tm // BLOCK) for g in range(N_KV_HEADS)]

        def scores(jb, g):
            r0 = jb * BLOCK
            parts = []
            for s in (2 * g, 2 * g + 1):
                slab = qb[r0:r0 + BLOCK, s * LANES:(s + 1) * LANES]
                parts.append(jnp.where(low, slab, zero))
                parts.append(jnp.where(low, zero, slab))
            qs = jnp.concatenate(parts, axis=0)
            return lax.dot_general(k2buf[g, r0:r0 + 2 * BLOCK, :], qs,
                                   (((1,), (1,)), ((), ())), preferred_element_type=_F32)

        def weights(jb, g, sc):
            valid = band & (key_row >= first_key) if jb == 0 else band
            es, tail = [], []
            for i in range(GROUP):
                sink = sinks_ref[GROUP * g + i]
                sp = jnp.where(valid, sc[:, i * BLOCK:(i + 1) * BLOCK], NEG_INF)
                m = jnp.maximum(jnp.max(sp, axis=0, keepdims=True), sink)
                es.append(jnp.exp(sp - m).astype(_BF16))
                tail.append(jnp.exp(sink - m))
            return jnp.concatenate(es, axis=1), tail

        def values(jb, g, e, tail):
            r0 = jb * BLOCK
            o3 = jnp.dot(vtbuf[g, :, r0:r0 + 2 * BLOCK], e, preferred_element_type=_F32)
            for pair in range(GROUP // 2):
                halves = []
                for i in (2 * pair, 2 * pair + 1):
                    cols = slice(i * BLOCK, (i + 1) * BLOCK)
                    halves.append(o3[0:HEAD_DIM, cols] * (1.0 / (o3[HEAD_DIM:HEAD_DIM + 1, cols] + tail[i])))
                slab = jnp.concatenate(halves, axis=0).T
                s = 2 * g + pair
                obuf[r0:r0 + BLOCK, s * LANES:(s + 1) * LANES] = slab.astype(_BF16)

        sc, ew = {}, {}
        for n in range(len(units) + 2):
            if n < len(units):
                sc[n] = scores(*units[n])
                yield score_cost
            if 0 <= n - 1 < len(units):
                ew[n - 1] = weights(*units[n - 1], sc.pop(n - 1))
                yield 0.0
            if 0 <= n - 2 < len(units):
                values(*units[n - 2], *ew.pop(n - 2))
                yield value_cost
        hbuf[...] = h
        k2buf[:, 0:BLOCK, :] = k2buf[:, tm:tm + BLOCK, :]
        vtbuf[:, 0:HEAD_DIM, 0:BLOCK] = vtbuf[:, 0:HEAD_DIM, tm:tm + BLOCK]

    def body(first_half, second_half):
        lanes = []
        qkv_cost = _chunks_cost(w_kv) + _chunks_cost(w_q)
        if first_half and not second_half:
            _load_weights([(w_k_hbm, w_kv, 0), (w_v_hbm, w_kv, KV_WIDTH),
                           (w_q_hbm.at[0], w_q, 0), (w_o_hbm.at[0], w_o, 0),
                           (w_gate_hbm.at[layer], w_gate, 0), (w_up_hbm.at[layer], w_up, 0),
                           (w_down_hbm.at[layer], w_down, 0), (ple_gate_hbm.at[layer], ple_gate, 0),
                           (ple_proj_hbm.at[layer], ple_proj, 0)], staging, sems)
            k2buf[:, 0:BLOCK, :] = jnp.zeros((N_KV_HEADS, BLOCK, LANES), _BF16)
            vtbuf[:, 0:HEAD_DIM, 0:BLOCK] = jnp.zeros((N_KV_HEADS, HEAD_DIM, BLOCK), _BF16)
            vtbuf[:, HEAD_DIM:, :] = jnp.ones((N_KV_HEADS, VALUE_ROWS - HEAD_DIM, BLOCK + tm), _BF16)
        if second_half:
            dense = dense_half()
            next(dense)
            lanes.append((dense, dense_cost, LAYER1_WINDOWS[1]))
        if first_half:
            n_units = (tm // BLOCK) * N_KV_HEADS
            attention = attention_half()
            next(attention)
            lanes.append((attention, qkv_cost + n_units * (score_cost + value_cost), LAYER1_WINDOWS[0]))
        if first_half and not second_half:
            lanes.append((_scaled(sample_qkv(), sample_rows), qkv_cost * sample_rows, (0.0, 1.0)))
        elif first_half:
            lanes.append((sample_decodes(), 3 * per_step, SAMPLE_DECODE_WINDOW))
        else:
            lanes.append((sample_tail(), 3 * per_step + dense_cost * sample_rows, (0.0, 1.0)))
        _interleave(*lanes)

    _pipeline_bodies(step, n_tiles, body)


class _Layer:
    def __init__(self, stacked, layer):
        self.stacked, self.layer = stacked, layer


def _operand(x):
    return x.stacked if isinstance(x, _Layer) else x


def _resident(x):
    if isinstance(x, _Layer):
        rest = x.stacked.shape[1:]
        index = (x.layer,) + (0,) * len(rest)
        return pl.BlockSpec((None,) + rest, lambda *_: index, pipeline_mode=pl.Buffered(1))
    nd = x.ndim
    return pl.BlockSpec(x.shape, lambda *_: (0,) * nd, pipeline_mode=pl.Buffered(1))


def _params(n_grid_dims):
    return pltpu.CompilerParams(dimension_semantics=("arbitrary",) * n_grid_dims,
                                vmem_limit_bytes=VMEM_LIMIT_BYTES)


def _rope_tables(pos):
    half = ROT_DIM // 2
    inv_freq = np.power(np.float64(ROPE_THETA), -np.arange(half, dtype=np.float64) / half)
    ang = pos.astype(np.float64)[:, None] * inv_freq[None, :]
    cos, sin = np.cos(ang), np.sin(ang)
    n = pos.shape[0]
    pad = np.zeros((n, HEAD_DIM - ROT_DIM))
    zero = np.zeros((n, half))
    cos_h = np.concatenate([cos, cos, pad + 1.0], axis=1)
    lo_h = np.concatenate([-sin, zero, pad], axis=1)
    hi_h = np.concatenate([zero, sin, pad], axis=1)
    rep = LANES // HEAD_DIM
    return tuple(jnp.asarray(np.tile(a, (1, rep)), dtype=_F32) for a in (cos_h, lo_h, hi_h))


def kernel(x_prompt, x_sample, state_conv, cache_k_win, cache_v_win, p_prompt, p_sample,
           norm_mix_g, norm_ffn_g, norm_ple_g, kv_norm_g, final_norm_g,
           conv_w_in, conv_w, conv_w_out, w_k, w_v, w_q, sinks, w_o,
           ffn_w_gate, ffn_w_up, ffn_w_down, ple_w_proj, ple_w_gate):
    batch, seq, d = x_prompt.shape
    dec = x_sample.shape[0]
    w_buf = cache_k_win.shape[1]
    tm = PROMPT_TILE
    assert d == D_MODEL and seq % tm == 0 and tm % BLOCK == 0 and tm >= WINDOW
    assert x_sample.shape[1] == 1 and w_buf == WINDOW and dec % (batch * (seq // tm)) == 0

    row = lambda a: a.reshape(1, -1)
    gains = [norm_mix_g, norm_ffn_g, norm_ple_g]
    hbm = pl.BlockSpec(memory_space=pl.ANY)
    ffn_weights = [ffn_w_gate, ffn_w_up, ffn_w_down, ple_w_gate, ple_w_proj]
    staging = [pltpu.VMEM((STAGING_SLOTS, STAGING_ROWS, max(w.shape[-1] for w in [conv_w_in] + ffn_weights)), _F32),
               pltpu.SemaphoreType.DMA((STAGING_SLOTS,))]
    resident_bf16 = lambda ws: [pltpu.VMEM(w.shape[-2:], _BF16) for w in ws]

    nt = seq // tm
    n_tiles = batch * nt
    cur = lambda s: jnp.minimum(s, n_tiles - 1)
    prev = lambda s: jnp.maximum(s - 1, 0)
    cur_tile = pl.BlockSpec((None, tm, d), lambda s: (cur(s) // nt, cur(s) % nt, 0))
    prev_tile = pl.BlockSpec((None, tm, d), lambda s: (prev(s) // nt, prev(s) % nt, 0))
    prev_ple = lambda layer: pl.BlockSpec((None, None, tm, p_prompt.shape[-1]),
                                          lambda s: (layer, prev(s) // nt, prev(s) % nt, 0))
    sample_in = [x_sample.reshape(dec, d), _Layer(p_sample, 0), state_conv[0, :, 0, :], state_conv[0, :, 1, :]]
    l0_small = gains + [_Layer(conv_w, 0)]
    l0_weights = [conv_w_in, conv_w_out] + ffn_weights
    sample_rows = pl.BlockSpec((dec, d), lambda s: (0, 0))
    h1, conv_state_prompt, h1s, u_s = pl.pallas_call(
        functools.partial(_layer0_kernel, layer=0, tiles_per_seq=nt, n_tiles=n_tiles),
        grid=(n_tiles + 1,),
        in_specs=[cur_tile, prev_ple(0)] + [_resident(a) for a in sample_in + l0_small]
                 + [hbm] * len(l0_weights),
        out_specs=[prev_tile, pl.BlockSpec((None, CONV_WIDTH - 1, d), lambda s: (cur(s) // nt, 0, 0)),
                   sample_rows, sample_rows],
        out_shape=[jax.ShapeDtypeStruct((batch, seq, d), _F32),
                   jax.ShapeDtypeStruct((batch, CONV_WIDTH - 1, d), _F32),
                   jax.ShapeDtypeStruct((dec, d), _F32),
                   jax.ShapeDtypeStruct((dec, d), _F32)],
        scratch_shapes=[pltpu.VMEM((tm + SUBLANES, d), _F32),
                        pltpu.VMEM((tm, d), _F32),
                        pltpu.VMEM((tm, d), _BF16)] + resident_bf16(l0_weights) + staging,
        compiler_params=_params(1),
        name="layer0",
    )(x_prompt, p_prompt, *map(_operand, sample_in + l0_small), *l0_weights)
    conv_state_sample = jnp.stack([state_conv[0, :, 1, :], u_s], axis=1)[None]

    cos_p, lo_p, hi_p = _rope_tables(np.arange(seq))
    cos_s, lo_s, hi_s = _rope_tables(np.full((dec,), PAST_LEN))
    windows_t = lambda a: jnp.transpose(a, (0, 2, 3, 1)).reshape(a.shape[0], KV_WIDTH, a.shape[1])
    windows = lambda a: jnp.transpose(a.reshape(a.shape[0], N_KV_HEADS, HEAD_DIM, a.shape[2]), (0, 3, 1, 2))
    ck, cv = windows_t(cache_k_win), windows_t(cache_v_win)
    per_step = dec // n_tiles
    cache = pl.BlockSpec((per_step, KV_WIDTH, w_buf), lambda s: (prev(s), 0, 0))
    sample_l1 = [h1s, _Layer(p_sample, 1), cos_s, lo_s, hi_s]
    l1_small = [row(kv_norm_g)] + gains + [row(final_norm_g)]
    l1_weights = [w_k, w_v, w_q, w_o] + ffn_weights
    l1_resident = [pltpu.VMEM((d, 2 * KV_WIDTH), _BF16)] + resident_bf16([w_q, w_o] + ffn_weights)
    y_prompt, k_win_prompt, v_win_prompt, y_sample, k_win_sample, v_win_sample = pl.pallas_call(
        functools.partial(_layer1_kernel, layer=1, tiles_per_seq=nt, n_tiles=n_tiles),
        grid=(n_tiles + 1,),
        in_specs=[cur_tile, prev_ple(1)]
                 + [pl.BlockSpec((tm, LANES), lambda s: (cur(s) % nt, 0))] * 3
                 + [pl.BlockSpec(memory_space=pltpu.SMEM)]
                 + [_resident(a) for a in sample_l1] + [cache, cache]
                 + [_resident(a) for a in l1_small] + [hbm] * len(l1_weights),
        out_specs=[prev_tile,
                   pl.BlockSpec((None, KV_WIDTH, WINDOW), lambda s: (cur(s) // nt, 0, 0)),
                   pl.BlockSpec((None, KV_WIDTH, WINDOW), lambda s: (cur(s) // nt, 0, 0)),
                   sample_rows, cache, cache],
        out_shape=[jax.ShapeDtypeStruct((batch, seq, d), _F32),
                   jax.ShapeDtypeStruct((batch, KV_WIDTH, WINDOW), _F32),
                   jax.ShapeDtypeStruct((batch, KV_WIDTH, WINDOW), _F32),
                   jax.ShapeDtypeStruct((dec, d), _F32),
                   jax.ShapeDtypeStruct((dec, KV_WIDTH, w_buf), _F32),
                   jax.ShapeDtypeStruct((dec, KV_WIDTH, w_buf), _F32)],
        scratch_shapes=[pltpu.VMEM((N_KV_HEADS, BLOCK + tm, LANES), _BF16),
                        pltpu.VMEM((N_KV_HEADS, VALUE_ROWS, BLOCK + tm), _BF16),
                        pltpu.VMEM((tm, d), _F32),
                        pltpu.VMEM((tm, d), _BF16),
                        pltpu.VMEM((dec, d), _F32),
                        pltpu.VMEM((dec, KV_WIDTH), _F32),
                        pltpu.VMEM((dec, KV_WIDTH), _F32),
                        pltpu.VMEM((KV_WIDTH, dec), _F32),
                        pltpu.VMEM((KV_WIDTH, dec), _F32),
                        pltpu.VMEM((dec, d), _F32)] + l1_resident + staging,
        compiler_params=_params(1),
        name="layer1",
    )(h1, p_prompt, cos_p, lo_p, hi_p, sinks[0], *map(_operand, sample_l1), ck, cv,
      *map(_operand, l1_small), *l1_weights)

    return (y_prompt, y_sample.reshape(x_sample.shape),
            conv_state_prompt[None], conv_state_sample,
            windows(k_win_prompt), windows(v_win_prompt),
            windows(k_win_sample), windows(v_win_sample))
```

```python
import functools

import jax
import jax.numpy as jnp
import numpy as np
from jax import lax
from jax.experimental import pallas as pl
from jax.experimental.pallas import tpu as pltpu

D_MODEL = 1024
HEAD_DIM = 64
N_HEADS = 16
N_KV_HEADS = 4
GROUP = N_HEADS // N_KV_HEADS
KV_WIDTH = N_KV_HEADS * HEAD_DIM
ROT_DIM = HEAD_DIM // 4
ROPE_THETA = 500000.0
WINDOW = 128
BLOCK = 128
VALUE_ROWS = HEAD_DIM + 16
CONV_WIDTH = 3
PAST_LEN = 16384
RMS_EPS = 1e-6
NEG_INF = -1e30

LANES = 128
SUBLANES = 8
MXU_COLS = 256
CHUNK_COLS = 2 * MXU_COLS
VMEM_LIMIT_BYTES = 60000 * 1024

PROMPT_TILE = 256
STAGING_ROWS = 256
STAGING_SLOTS = 3
SAMPLE_DECODE_WINDOW = (0.05, 0.85)
LAYER0_WINDOWS = ((0.10, 0.92), (0.0, 1.0))
LAYER1_WINDOWS = ((0.04, 0.92), (0.0, 1.0))

_BF16 = jnp.bfloat16
_F32 = jnp.float32


def _rms_scale(x):
    return lax.rsqrt(jnp.mean(x * x, axis=-1, keepdims=True) + RMS_EPS)


def _sigmoid(x):
    return 1.0 / (1.0 + jnp.exp(-x))


def _rope(x, cos, sin_lo, sin_hi):
    half = ROT_DIM // 2
    out = []
    for c in range(x.shape[1] // LANES):
        slab = x[:, c * LANES:(c + 1) * LANES]
        out.append(slab * cos
                   + pltpu.roll(slab, LANES - half, 1) * sin_lo
                   + pltpu.roll(slab, half, 1) * sin_hi)
    return jnp.concatenate(out, axis=1)


def _interleave(*lanes):
    done = [0.0] * len(lanes)
    live = list(range(len(lanes)))

    def position(j):
        _, total, (start, end) = lanes[j]
        return start + (end - start) * done[j] / total

    while live:
        i = min(live, key=position)
        try:
            done[i] += next(lanes[i][0])
        except StopIteration:
            live.remove(i)


def _scaled(steps, factor):
    try:
        while True:
            yield next(steps) * factor
    except StopIteration as done:
        return done.value


def _chunks(n_cols):
    return [(c, min(CHUNK_COLS, n_cols - c)) for c in range(0, n_cols, CHUNK_COLS)]


def _col_dot(a, w_ref, c0, width):
    return jnp.dot(a, w_ref[:, c0:c0 + width], preferred_element_type=_F32)


def _dot_cost(w_ref, width):
    return -(-w_ref.shape[0] // MXU_COLS) * (width // MXU_COLS)


def _chunks_cost(w_ref):
    return _dot_cost(w_ref, w_ref.shape[1])


def _dot_cols(a, w_ref):
    a = a.astype(_BF16)
    cols = []
    for c, width in _chunks(w_ref.shape[1]):
        cols.append(_col_dot(a, w_ref, c, width))
        yield _dot_cost(w_ref, width)
    return jnp.concatenate(cols, axis=1)


def _ffn_norm(h, g_ffn):
    return ((h * _rms_scale(h)) * g_ffn[...]).astype(_BF16)


def _ffn_and_ple_steps(h, hf, p, g_ple, w_gate, w_up, w_down, ple_gate, ple_proj):
    acts = []
    for c, width in _chunks(w_gate.shape[1]):
        g = _col_dot(hf, w_gate, c, width)
        yield _dot_cost(w_gate, width)
        u = _col_dot(hf, w_up, c, width)
        yield _dot_cost(w_up, width)
        acts.append(((g * _sigmoid(g)) * u).astype(_BF16))
    h = h + (yield from _dot_cols(jnp.concatenate(acts, axis=1), w_down))
    hp = (h * _rms_scale(h)) * g_ple[...]
    gate = _sigmoid((yield from _dot_cols(hp, ple_gate)))
    return h + gate * (yield from _dot_cols(p, ple_proj))


def _ffn_and_ple_cost(w_gate, w_down, ple_gate, ple_proj):
    return 2 * _chunks_cost(w_gate) + _chunks_cost(w_down) + _chunks_cost(ple_gate) + _chunks_cost(ple_proj)


def _conv_mixer_steps(x, g_mix, w_in, conv_w, w_out, history):
    hn = ((x * _rms_scale(x)) * g_mix[...]).astype(_BF16)
    yield 0
    us, convs = [], []
    for c, width in _chunks(D_MODEL):
        cols = slice(c, c + width)
        c_gate = _col_dot(hn, w_in, D_MODEL + c, width)
        yield _dot_cost(w_in, width)
        xin = _col_dot(hn, w_in, 2 * D_MODEL + c, width)
        yield _dot_cost(w_in, width)
        u = c_gate * xin
        back2, back1 = history(cols, u)
        conv = conv_w[0:1, cols] * back2
        conv = conv + conv_w[1:2, cols] * back1
        conv = conv + conv_w[2:3, cols] * u
        us.append(u)
        convs.append(conv)
    gated = []
    for j, (c, width) in enumerate(_chunks(D_MODEL)):
        b_gate = _col_dot(hn, w_in, c, width)
        yield _dot_cost(w_in, width)
        gated.append((b_gate * convs[j]).astype(_BF16))
    y = yield from _dot_cols(jnp.concatenate(gated, axis=1), w_out)
    return x + y, jnp.concatenate(us, axis=1)


def _conv_mixer_cost(w_in, w_out):
    return _chunks_cost(w_in) + _chunks_cost(w_out)


def _pipeline_bodies(step, n_tiles, body):
    pl.when(step == 0)(lambda: body(True, False))
    pl.when((step > 0) & (step < n_tiles))(lambda: body(True, True))
    pl.when(step == n_tiles)(lambda: body(False, True))


def _load_weights(plan, staging, sems):
    chunks = [(src, dst, c0, r0, min(STAGING_ROWS, src.shape[0] - r0))
              for src, dst, c0 in plan for r0 in range(0, src.shape[0], STAGING_ROWS)]
    slots = staging.shape[0]

    def copy(i):
        src, _, _, r0, n = chunks[i]
        window = staging.at[i % slots, pl.ds(0, n), pl.ds(0, src.shape[1])]
        return pltpu.make_async_copy(src.at[pl.ds(r0, n), :], window, sems.at[i % slots])

    for i in range(min(slots - 1, len(chunks))):
        copy(i).start(priority=i % 2)
    for i, (src, dst, c0, r0, n) in enumerate(chunks):
        ahead = i + slots - 1
        if ahead < len(chunks):
            copy(ahead).start(priority=ahead % 2)
        copy(i).wait()
        cols = src.shape[1]
        dst[r0:r0 + n, c0:c0 + cols] = staging[i % slots, 0:n, 0:cols].astype(_BF16)


def _layer0_kernel(x_ref, p_ref, xs_ref, ps_ref, state_ref, g_mix, g_ffn, g_ple, conv_w,
                   w_in_hbm, w_out_hbm, w_gate_hbm, w_up_hbm, w_down_hbm, ple_gate_hbm, ple_proj_hbm,
                   h_out, state_out, hs_out, states_out,
                   ubuf, hbuf, hfbuf, w_in, w_out, w_gate, w_up, w_down, ple_gate, ple_proj, staging, sems,
                   *, layer, tiles_per_seq, n_tiles):
    tm = x_ref.shape[0]
    step = pl.program_id(0)
    g_mix, g_ffn, g_ple = (g.at[pl.ds(layer, 1)] for g in (g_mix, g_ffn, g_ple))
    ffn_cost = _ffn_and_ple_cost(w_gate, w_down, ple_gate, ple_proj)
    conv_cost = _conv_mixer_cost(w_in, w_out)

    def history(cols, u):
        ubuf[SUBLANES:SUBLANES + tm, cols] = u
        return (ubuf[SUBLANES - 2:SUBLANES - 2 + tm, cols], ubuf[SUBLANES - 1:SUBLANES - 1 + tm, cols])

    def ffn_half():
        out = yield from _ffn_and_ple_steps(hbuf[...], hfbuf[...], p_ref[...], g_ple,
                                            w_gate, w_up, w_down, ple_gate, ple_proj)
        h_out[...] = out

    def conv_half():
        h, _ = yield from _conv_mixer_steps(x_ref[...], g_mix, w_in, conv_w, w_out, history)
        state_out[...] = ubuf[SUBLANES + tm - (CONV_WIDTH - 1):SUBLANES + tm, :]
        ubuf[0:SUBLANES, :] = ubuf[tm:tm + SUBLANES, :]
        hbuf[...] = h
        hfbuf[...] = _ffn_norm(h, g_ffn)

    def sample_layer():
        history = lambda cols, u: (state_ref[:, 0, cols], state_ref[:, 1, cols])
        h, u = yield from _conv_mixer_steps(xs_ref[:, 0, :], g_mix, w_in, conv_w, w_out, history)
        out = yield from _ffn_and_ple_steps(h, _ffn_norm(h, g_ffn), ps_ref[:, 0, :], g_ple,
                                            w_gate, w_up, w_down, ple_gate, ple_proj)
        hs_out[...] = out
        states_out[:, 0, :] = state_ref[:, 1, :]
        states_out[:, 1, :] = u

    def body(first_half, second_half):
        lanes = []
        if first_half and not second_half:
            _load_weights([(w_in_hbm.at[0], w_in, 0), (w_out_hbm.at[0], w_out, 0),
                           (w_gate_hbm.at[layer], w_gate, 0), (w_up_hbm.at[layer], w_up, 0),
                           (w_down_hbm.at[layer], w_down, 0), (ple_gate_hbm.at[layer], ple_gate, 0),
                           (ple_proj_hbm.at[layer], ple_proj, 0)], staging, sems)
        if first_half:
            zeros = jnp.zeros((SUBLANES, D_MODEL), _F32)
            fresh = lax.rem(step, tiles_per_seq) == 0
            ubuf[0:SUBLANES, :] = jnp.where(fresh, zeros, ubuf[0:SUBLANES, :]) if second_half else zeros
        if second_half:
            ffn = ffn_half()
            next(ffn)
            lanes.append((ffn, ffn_cost, LAYER0_WINDOWS[1]))
        if first_half:
            conv = conv_half()
            next(conv)
            lanes.append((conv, conv_cost, LAYER0_WINDOWS[0]))
        else:
            rows = xs_ref.shape[0] / tm
            lanes.append((_scaled(sample_layer(), rows), (conv_cost + ffn_cost) * rows, (0.0, 1.0)))
        _interleave(*lanes)

    _pipeline_bodies(step, n_tiles, body)


def _qkv_steps(h, cos, sin_lo, sin_hi, g_kv, g_mix, w_kv, w_q):
    hs = h * _rms_scale(h)
    h_kv = (hs * g_kv[...]).astype(_BF16)
    h_q = (hs * g_mix[...]).astype(_BF16)
    yield 0
    kv = yield from _dot_cols(h_kv, w_kv)
    q = yield from _dot_cols(h_q, w_q)
    k = _rope(kv[:, :KV_WIDTH], cos, sin_lo, sin_hi)
    v = kv[:, KV_WIDTH:]
    q = _rope(q, cos, sin_lo, sin_hi) * (HEAD_DIM ** -0.5)
    return q, k, v


def _regroup_heads(x, order):
    low = lax.broadcasted_iota(jnp.int32, (1, LANES), 1) < HEAD_DIM
    out = []
    for s in range(x.shape[1] // LANES):
        halves = []
        for half in range(2):
            src_head = order[2 * s + half]
            slab = x[:, (src_head // 2) * LANES:(src_head // 2 + 1) * LANES]
            halves.append(slab if src_head % 2 == half else pltpu.roll(slab, HEAD_DIM, 1))
        out.append(jnp.where(low, halves[0], halves[1]))
    return jnp.concatenate(out, axis=1)


_SLOT_SWAP = [GROUP * (p % N_KV_HEADS) + p // N_KV_HEADS for p in range(N_HEADS)]


def _layer1_kernel(h_ref, p_ref, cos_ref, slo_ref, shi_ref, sinks_ref,
                   hs_ref, pss_ref, cos_s, slo_s, shi_s, ck_ref, cv_ref,
                   g_kv, g_mix, g_ffn, g_ple, g_final,
                   w_k_hbm, w_v_hbm, w_q_hbm, w_o_hbm, w_gate_hbm, w_up_hbm, w_down_hbm,
                   ple_gate_hbm, ple_proj_hbm,
                   y_out, kwin_out, vwin_out, ys_out, kwins_out, vwins_out,
                   k2buf, vtbuf, hbuf, obuf, qs_buf, ks_buf, vs_buf, kst_buf, vst_buf, os_buf,
                   w_kv, w_q, w_o, w_gate, w_up, w_down, ple_gate, ple_proj, staging, sems,
                   *, layer, tiles_per_seq, n_tiles):
    tm = h_ref.shape[0]
    step = pl.program_id(0)
    g_mix, g_ffn, g_ple = (g.at[pl.ds(layer, 1)] for g in (g_mix, g_ffn, g_ple))
    score_cost = GROUP * BLOCK / MXU_COLS * (2 * BLOCK) / tm
    value_cost = GROUP * BLOCK / MXU_COLS * VALUE_ROWS / tm
    dense_cost = _chunks_cost(w_o) + _ffn_and_ple_cost(w_gate, w_down, ple_gate, ple_proj)
    per_step = ck_ref.shape[0]
    sample_rows = hs_ref.shape[0] / tm

    def dense_half():
        hp = hbuf[...] + (yield from _dot_cols(obuf[...], w_o))
        hp = yield from _ffn_and_ple_steps(hp, _ffn_norm(hp, g_ffn), p_ref[...], g_ple,
                                           w_gate, w_up, w_down, ple_gate, ple_proj)
        y_out[...] = (hp * _rms_scale(hp)) * g_final[...]

    def sample_qkv():
        q, k, v = yield from _qkv_steps(hs_ref[...], cos_s[...], slo_s[...], shi_s[...],
                                        g_kv, g_mix, w_kv, w_q)
        qs_buf[...] = _regroup_heads(q, _SLOT_SWAP)
        ks_buf[...] = k
        vs_buf[...] = v
        kst_buf[...] = k.T
        vst_buf[...] = v.T

    def sample_decode(j):
        r = (step - 1) * per_step + j
        w = ck_ref.shape[2]
        key = lax.broadcasted_iota(jnp.int32, (KV_WIDTH, w), 1)
        seq_lane = lax.broadcasted_iota(jnp.int32, kst_buf.shape, 1)
        lane = lax.broadcasted_iota(jnp.int32, (N_HEADS, KV_WIDTH), 1)
        r16 = lax.broadcasted_iota(jnp.int32, (N_HEADS, KV_WIDTH), 0)
        slot = r16 // N_KV_HEADS
        kvh = r16 - slot * N_KV_HEADS
        own = (lane >= kvh * HEAD_DIM) & (lane < (kvh + 1) * HEAD_DIM)
        head_row = lax.broadcasted_iota(jnp.int32, (N_HEADS, 1), 0)
        sink = jnp.zeros((N_HEADS, 1), _F32)
        for h in range(N_HEADS):
            sink = jnp.where(head_row == _SLOT_SWAP[h], sinks_ref[h], sink)
        kn = ks_buf[pl.ds(r, 1), :]
        vn = vs_buf[pl.ds(r, 1), :]
        for win_ref, new_t, out_ref in ((ck_ref, kst_buf, kwins_out), (cv_ref, vst_buf, vwins_out)):
            column = jnp.sum(jnp.where(seq_lane == r, new_t[...], 0.0), axis=1, keepdims=True)
            out_ref[j] = jnp.where(key == w - 1, column, pltpu.roll(win_ref[j], w - 1, 1))
        lhs = jnp.zeros((N_HEADS, KV_WIDTH), _F32)
        for i in range(GROUP):
            q_i = jnp.broadcast_to(qs_buf[pl.ds(r, 1), i * KV_WIDTH:(i + 1) * KV_WIDTH], (N_HEADS, KV_WIDTH))
            lhs = jnp.where(own & (slot == i), q_i, lhs)
        lhs = lhs.astype(_BF16)
        s_old = jnp.dot(lhs, ck_ref[j].astype(_BF16), preferred_element_type=_F32)
        yield 1
        s_new = jnp.sum(lhs.astype(_F32) * kn.astype(_BF16).astype(_F32), axis=-1, keepdims=True)
        m = jnp.maximum(jnp.maximum(jnp.max(s_old, axis=-1, keepdims=True), s_new), sink)
        e_old = jnp.exp(s_old - m)
        e_new = jnp.exp(s_new - m)
        inv = 1.0 / (jnp.sum(e_old, axis=-1, keepdims=True) + e_new + jnp.exp(sink - m))
        yield 1
        o16 = lax.dot_general(e_old.astype(_BF16), cv_ref[j].astype(_BF16), (((1,), (1,)), ((), ())),
                              preferred_element_type=_F32)
        o16 = (o16 + e_new.astype(_BF16).astype(_F32) * vn.astype(_BF16).astype(_F32)) * inv
        for i in range(GROUP):
            picked = jnp.where(own & (slot == i), o16, 0.0)
            os_buf[pl.ds(r, 1), i * KV_WIDTH:(i + 1) * KV_WIDTH] = jnp.sum(picked, axis=0, keepdims=True)
        yield 1

    def sample_decodes():
        for j in range(per_step):
            yield from sample_decode(j)

    def sample_tail():
        yield from sample_decodes()
        o = _regroup_heads(os_buf[...], _SLOT_SWAP)
        h = hs_ref[...] + (yield from _scaled(_dot_cols(o, w_o), sample_rows))
        h = yield from _scaled(_ffn_and_ple_steps(h, _ffn_norm(h, g_ffn), pss_ref[:, 0, :], g_ple,
                                                  w_gate, w_up, w_down, ple_gate, ple_proj), sample_rows)
        ys_out[:, 0, :] = (h * _rms_scale(h)) * g_final[...]

    def attention_half():
        h = h_ref[...]
        q, k, v = yield from _qkv_steps(h, cos_ref[...], slo_ref[...], shi_ref[...], g_kv, g_mix, w_kv, w_q)
        qb = q.astype(_BF16)
        kwin_out[...] = k[tm - WINDOW:, :].T
        vwin_out[...] = v[tm - WINDOW:, :].T

        lane = lax.broadcasted_iota(jnp.int32, (1, LANES), 1)
        low = lane < HEAD_DIM
        for s in range(KV_WIDTH // LANES):
            slab = k[:, s * LANES:(s + 1) * LANES]
            swapped = pltpu.roll(slab, HEAD_DIM, 1)
            k2buf[2 * s, BLOCK:BLOCK + tm, :] = jnp.where(low, slab, swapped).astype(_BF16)
            k2buf[2 * s + 1, BLOCK:BLOCK + tm, :] = jnp.where(low, swapped, slab).astype(_BF16)
        v_t = v.T.astype(_BF16)
        for g in range(N_KV_HEADS):
            vtbuf[g, 0:HEAD_DIM, BLOCK:BLOCK + tm] = v_t[g * HEAD_DIM:(g + 1) * HEAD_DIM, :]

        key_row = lax.broadcasted_iota(jnp.int32, (2 * BLOCK, BLOCK), 0)
        q_col = lax.broadcasted_iota(jnp.int32, (2 * BLOCK, BLOCK), 1)
        diff = q_col + BLOCK - key_row
        band = (diff >= 0) & (diff <= WINDOW)
        first_key = jnp.where(lax.rem(step, tiles_per_seq) == 0, BLOCK, 0)
        zero = jnp.zeros((), _BF16)
        units = [(jb, g) for jb in range(tm // BLOCK) for g in range(N_KV_HEADS)]

        def scores(jb, g):
            r0 = jb * BLOCK
            parts = []
            for s in (2 * g, 2 * g + 1):
                slab = qb[r0:r0 + BLOCK, s * LANES:(s + 1) * LANES]
                parts.append(jnp.where(low, slab, zero))
                parts.append(jnp.where(low, zero, slab))
            qs = jnp.concatenate(parts, axis=0)
            return lax.dot_general(k2buf[g, r0:r0 + 2 * BLOCK, :], qs,
                                   (((1,), (1,)), ((), ())), preferred_element_type=_F32)

        def weights(jb, g, sc):
            valid = band & (key_row >= first_key) if jb == 0 else band
            es, tail = [], []
            for i in range(GROUP):
                sink = sinks_ref[GROUP * g + i]
                sp = jnp.where(valid, sc[:, i * BLOCK:(i + 1) * BLOCK], NEG_INF)
                m = jnp.maximum(jnp.max(sp, axis=0, keepdims=True), sink)
                es.append(jnp.exp(sp - m).astype(_BF16))
                tail.append(jnp.exp(sink - m))
            return jnp.concatenate(es, axis=1), tail

        def values(jb, g, e, tail):
            r0 = jb * BLOCK
            o3 = jnp.dot(vtbuf[g, :, r0:r0 + 2 * BLOCK], e, preferred_element_type=_F32)
            for pair in range(GROUP // 2):
                halves = []
                for i in (2 * pair, 2 * pair + 1):
                    cols = slice(i * BLOCK, (i + 1) * BLOCK)
                    halves.append(o3[0:HEAD_DIM, cols] * (1.0 / (o3[HEAD_DIM:HEAD_DIM + 1, cols] + tail[i])))
                slab = jnp.concatenate(halves, axis=0).T
                s = 2 * g + pair
                obuf[r0:r0 + BLOCK, s * LANES:(s + 1) * LANES] = slab.astype(_BF16)

        sc, ew = {}, {}
        for n in range(len(units) + 2):
            if n < len(units):
                sc[n] = scores(*units[n])
                yield score_cost
            if 0 <= n - 1 < len(units):
                ew[n - 1] = weights(*units[n - 1], sc.pop(n - 1))
                yield 0.0
            if 0 <= n - 2 < len(units):
                values(*units[n - 2], *ew.pop(n - 2))
                yield value_cost
        hbuf[...] = h
        k2buf[:, 0:BLOCK, :] = k2buf[:, tm:tm + BLOCK, :]
        vtbuf[:, 0:HEAD_DIM, 0:BLOCK] = vtbuf[:, 0:HEAD_DIM, tm:tm + BLOCK]

    def body(first_half, second_half):
        lanes = []
        qkv_cost = _chunks_cost(w_kv) + _chunks_cost(w_q)
        if first_half and not second_half:
            _load_weights([(w_k_hbm, w_kv, 0), (w_v_hbm, w_kv, KV_WIDTH),
                           (w_q_hbm.at[0], w_q, 0), (w_o_hbm.at[0], w_o, 0),
                           (w_gate_hbm.at[layer], w_gate, 0), (w_up_hbm.at[layer], w_up, 0),
                           (w_down_hbm.at[layer], w_down, 0), (ple_gate_hbm.at[layer], ple_gate, 0),
                           (ple_proj_hbm.at[layer], ple_proj, 0)], staging, sems)
            k2buf[:, 0:BLOCK, :] = jnp.zeros((N_KV_HEADS, BLOCK, LANES), _BF16)
            vtbuf[:, 0:HEAD_DIM, 0:BLOCK] = jnp.zeros((N_KV_HEADS, HEAD_DIM, BLOCK), _BF16)
            vtbuf[:, HEAD_DIM:, :] = jnp.ones((N_KV_HEADS, VALUE_ROWS - HEAD_DIM, BLOCK + tm), _BF16)
        if second_half:
            dense = dense_half()
            next(dense)
            lanes.append((dense, dense_cost, LAYER1_WINDOWS[1]))
        if first_half:
            n_units = (tm // BLOCK) * N_KV_HEADS
            attention = attention_half()
            next(attention)
            lanes.append((attention, qkv_cost + n_units * (score_cost + value_cost), LAYER1_WINDOWS[0]))
        if first_half and not second_half:
            lanes.append((_scaled(sample_qkv(), sample_rows), qkv_cost * sample_rows, (0.0, 1.0)))
        elif first_half:
            lanes.append((sample_decodes(), 3 * per_step, SAMPLE_DECODE_WINDOW))
        else:
            lanes.append((sample_tail(), 3 * per_step + dense_cost * sample_rows, (0.0, 1.0)))
        _interleave(*lanes)

    _pipeline_bodies(step, n_tiles, body)


class _Layer:
    def __init__(self, stacked, layer):
        self.stacked, self.layer = stacked, layer


def _operand(x):
    return x.stacked if isinstance(x, _Layer) else x


def _resident(x):
    if isinstance(x, _Layer):
        rest = x.stacked.shape[1:]
        index = (x.layer,) + (0,) * len(rest)
        return pl.BlockSpec((None,) + rest, lambda *_: index, pipeline_mode=pl.Buffered(1))
    nd = x.ndim
    return pl.BlockSpec(x.shape, lambda *_: (0,) * nd, pipeline_mode=pl.Buffered(1))


def _params(n_grid_dims):
    return pltpu.CompilerParams(dimension_semantics=("arbitrary",) * n_grid_dims,
                                vmem_limit_bytes=VMEM_LIMIT_BYTES)


def _rope_tables(pos):
    half = ROT_DIM // 2
    inv_freq = np.power(np.float64(ROPE_THETA), -np.arange(half, dtype=np.float64) / half)
    ang = pos.astype(np.float64)[:, None] * inv_freq[None, :]
    cos, sin = np.cos(ang), np.sin(ang)
    n = pos.shape[0]
    pad = np.zeros((n, HEAD_DIM - ROT_DIM))
    zero = np.zeros((n, half))
    cos_h = np.concatenate([cos, cos, pad + 1.0], axis=1)
    lo_h = np.concatenate([-sin, zero, pad], axis=1)
    hi_h = np.concatenate([zero, sin, pad], axis=1)
    rep = LANES // HEAD_DIM
    return tuple(jnp.asarray(np.tile(a, (1, rep)), dtype=_F32) for a in (cos_h, lo_h, hi_h))


def kernel(x_prompt, x_sample, state_conv, cache_k_win, cache_v_win, p_prompt, p_sample,
           norm_mix_g, norm_ffn_g, norm_ple_g, kv_norm_g, final_norm_g,
           conv_w_in, conv_w, conv_w_out, w_k, w_v, w_q, sinks, w_o,
           ffn_w_gate, ffn_w_up, ffn_w_down, ple_w_proj, ple_w_gate):
    batch, seq, d = x_prompt.shape
    dec = x_sample.shape[0]
    w_buf = cache_k_win.shape[1]
    tm = PROMPT_TILE
    assert d == D_MODEL and seq % tm == 0 and tm % BLOCK == 0 and tm >= WINDOW
    assert x_sample.shape[1] == 1 and w_buf == WINDOW and dec % (batch * (seq // tm)) == 0

    row = lambda a: a.reshape(1, -1)
    gains = [norm_mix_g, norm_ffn_g, norm_ple_g]
    hbm = pl.BlockSpec(memory_space=pl.ANY)
    ffn_weights = [ffn_w_gate, ffn_w_up, ffn_w_down, ple_w_gate, ple_w_proj]
    staging = [pltpu.VMEM((STAGING_SLOTS, STAGING_ROWS, max(w.shape[-1] for w in [conv_w_in] + ffn_weights)), _F32),
               pltpu.SemaphoreType.DMA((STAGING_SLOTS,))]
    resident_bf16 = lambda ws: [pltpu.VMEM(w.shape[-2:], _BF16) for w in ws]

    nt = seq // tm
    n_tiles = batch * nt
    cur = lambda s: jnp.minimum(s, n_tiles - 1)
    prev = lambda s: jnp.maximum(s - 1, 0)
    seq_of = lambda tile: lax.div(tile, jnp.int32(nt))
    pos_of = lambda tile: lax.rem(tile, jnp.int32(nt))
    cur_tile = pl.BlockSpec((None, tm, d), lambda s: (seq_of(cur(s)), pos_of(cur(s)), 0))
    prev_tile = pl.BlockSpec((None, tm, d), lambda s: (seq_of(prev(s)), pos_of(prev(s)), 0))
    prev_ple = lambda layer: pl.BlockSpec((None, None, tm, p_prompt.shape[-1]),
                                          lambda s: (layer, seq_of(prev(s)), pos_of(prev(s)), 0))
    sample_in = [x_sample, _Layer(p_sample, 0), _Layer(state_conv, 0)]
    l0_small = gains + [_Layer(conv_w, 0)]
    l0_weights = [conv_w_in, conv_w_out] + ffn_weights
    sample_rows = pl.BlockSpec((dec, d), lambda s: (0, 0))
    h1, conv_state_prompt, h1s, conv_state_sample = pl.pallas_call(
        functools.partial(_layer0_kernel, layer=0, tiles_per_seq=nt, n_tiles=n_tiles),
        grid=(n_tiles + 1,),
        in_specs=[cur_tile, prev_ple(0)] + [_resident(a) for a in sample_in + l0_small]
                 + [hbm] * len(l0_weights),
        out_specs=[prev_tile, pl.BlockSpec((None, CONV_WIDTH - 1, d), lambda s: (seq_of(cur(s)), 0, 0)),
                   sample_rows, pl.BlockSpec((None,) + state_conv.shape[1:], lambda s: (0, 0, 0, 0))],
        out_shape=[jax.ShapeDtypeStruct((batch, seq, d), _F32),
                   jax.ShapeDtypeStruct((batch, CONV_WIDTH - 1, d), _F32),
                   jax.ShapeDtypeStruct((dec, d), _F32),
                   jax.ShapeDtypeStruct(state_conv.shape, _F32)],
        scratch_shapes=[pltpu.VMEM((tm + SUBLANES, d), _F32),
                        pltpu.VMEM((tm, d), _F32),
                        pltpu.VMEM((tm, d), _BF16)] + resident_bf16(l0_weights) + staging,
        compiler_params=_params(1),
        name="layer0",
    )(x_prompt, p_prompt, *map(_operand, sample_in + l0_small), *l0_weights)

    cos_p, lo_p, hi_p = _rope_tables(np.arange(seq))
    cos_s, lo_s, hi_s = _rope_tables(np.full((dec,), PAST_LEN))
    windows_t = lambda a: jnp.transpose(a, (0, 2, 3, 1)).reshape(a.shape[0], KV_WIDTH, a.shape[1])
    windows = lambda a: jnp.transpose(a.reshape(a.shape[0], N_KV_HEADS, HEAD_DIM, a.shape[2]), (0, 3, 1, 2))
    ck, cv = windows_t(cache_k_win), windows_t(cache_v_win)
    per_step = dec // n_tiles
    cache = pl.BlockSpec((per_step, KV_WIDTH, w_buf), lambda s: (prev(s), 0, 0))
    sample_l1 = [h1s, _Layer(p_sample, 1), cos_s, lo_s, hi_s]
    l1_small = [row(kv_norm_g)] + gains + [row(final_norm_g)]
    l1_weights = [w_k, w_v, w_q, w_o] + ffn_weights
    l1_resident = [pltpu.VMEM((d, 2 * KV_WIDTH), _BF16)] + resident_bf16([w_q, w_o] + ffn_weights)
    y_prompt, k_win_prompt, v_win_prompt, y_sample, k_win_sample, v_win_sample = pl.pallas_call(
        functools.partial(_layer1_kernel, layer=1, tiles_per_seq=nt, n_tiles=n_tiles),
        grid=(n_tiles + 1,),
        in_specs=[cur_tile, prev_ple(1)]
                 + [pl.BlockSpec((tm, LANES), lambda s: (pos_of(cur(s)), 0))] * 3
                 + [pl.BlockSpec(memory_space=pltpu.SMEM)]
                 + [_resident(a) for a in sample_l1] + [cache, cache]
                 + [_resident(a) for a in l1_small] + [hbm] * len(l1_weights),
        out_specs=[prev_tile,
                   pl.BlockSpec((None, KV_WIDTH, WINDOW), lambda s: (seq_of(cur(s)), 0, 0)),
                   pl.BlockSpec((None, KV_WIDTH, WINDOW), lambda s: (seq_of(cur(s)), 0, 0)),
                   pl.BlockSpec(x_sample.shape, lambda s: (0, 0, 0)), cache, cache],
        out_shape=[jax.ShapeDtypeStruct((batch, seq, d), _F32),
                   jax.ShapeDtypeStruct((batch, KV_WIDTH, WINDOW), _F32),
                   jax.ShapeDtypeStruct((batch, KV_WIDTH, WINDOW), _F32),
                   jax.ShapeDtypeStruct(x_sample.shape, _F32),
                   jax.ShapeDtypeStruct((dec, KV_WIDTH, w_buf), _F32),
                   jax.ShapeDtypeStruct((dec, KV_WIDTH, w_buf), _F32)],
        scratch_shapes=[pltpu.VMEM((N_KV_HEADS, BLOCK + tm, LANES), _BF16),
                        pltpu.VMEM((N_KV_HEADS, VALUE_ROWS, BLOCK + tm), _BF16),
                        pltpu.VMEM((tm, d), _F32),
                        pltpu.VMEM((tm, d), _BF16),
                        pltpu.VMEM((dec, d), _F32),
                        pltpu.VMEM((dec, KV_WIDTH), _F32),
                        pltpu.VMEM((dec, KV_WIDTH), _F32),
                        pltpu.VMEM((KV_WIDTH, dec), _F32),
                        pltpu.VMEM((KV_WIDTH, dec), _F32),
                        pltpu.VMEM((dec, d), _F32)] + l1_resident + staging,
        compiler_params=_params(1),
        name="layer1",
    )(h1, p_prompt, cos_p, lo_p, hi_p, sinks[0], *map(_operand, sample_l1), ck, cv,
      *map(_operand, l1_small), *l1_weights)

    return (y_prompt, y_sample,
            conv_state_prompt[None], conv_state_sample,
            windows(k_win_prompt), windows(v_win_prompt),
            windows(k_win_sample), windows(v_win_sample))
```
